```python
import math
import jax, jax.numpy as jnp
from jax import lax
import numpy as np

D_MODEL = 2048
BATCH = 2
SEQ = 16384
DEPTH = 1
DEC_BATCH = 8
DEC_SEQ = 32
PAST_LEN = 2048

CHUNK = 64
Q_BLOCK = 128
NORM_EPS = 1e-5
N_ATT_HEADS = 8
ATT_HEAD_DIM = 128
D_ATT = N_ATT_HEADS * 2 * ATT_HEAD_DIM
N_REL_BUCKETS = 32
REL_MAX_DIST = 128
D_SSM = 2048
SSM_HEAD_DIM = 64
N_SSM_HEADS = D_SSM // SSM_HEAD_DIM
N_SSM_GROUPS = 4
SSM_STATE = 128
CONV_WIDTH = 4
D_CONV = D_SSM + 2 * N_SSM_GROUPS * SSM_STATE
SSD_CHUNK = CHUNK
D_MIX = D_ATT + D_SSM
D_IN_PROJ = 4 * D_ATT + D_SSM + D_CONV + N_SSM_HEADS
SPLITS = [D_ATT, 2 * D_ATT, 3 * D_ATT, 4 * D_ATT, 4 * D_ATT + D_SSM, 4 * D_ATT + D_SSM + D_CONV]

kernel_name = 'hymba_diffattn_ssd_streaming_step'


def rms_norm(x, w):
    xf = x.astype(jnp.float32)
    y = xf * lax.rsqrt(jnp.mean(xf * xf, axis=-1, keepdims=True) + NORM_EPS)
    return (y * w.astype(jnp.float32)).astype(x.dtype)


def lambda_init(layer):
    return 0.8 - 0.6 * math.exp(-0.3 * layer)


def rel_bucket(rel):
    half = N_REL_BUCKETS // 2
    max_exact = half // 2
    ret = jnp.where(rel > 0, half, 0)
    n = jnp.abs(rel)
    nf = jnp.maximum(n, 1).astype(jnp.float32)
    large = max_exact + (jnp.log(nf / max_exact) / math.log(REL_MAX_DIST / max_exact)
                         * (half - max_exact)).astype(jnp.int32)
    large = jnp.minimum(large, half - 1)
    return ret + jnp.where(n < max_exact, n, large)


def diff_attn_block(q, k, v, q_pos, k_pos, rel_table, lam):
    logits = jnp.einsum('bqhmd,bkhmd->bhmqk', q, k,
                        preferred_element_type=jnp.float32) * (ATT_HEAD_DIM ** -0.5)
    bias = rel_table.astype(jnp.float32)[rel_bucket(k_pos[None, :] - q_pos[:, None])]
    bias = jnp.transpose(bias, (2, 0, 1))
    visible = (k_pos[None, :] // CHUNK) <= (q_pos[:, None] // CHUNK)
    logits = jnp.where(visible, logits + bias[None, :, None], -jnp.inf)
    p = jax.nn.softmax(logits, axis=-1)
    w = p[:, :, 0] - lam * p[:, :, 1]
    return jnp.einsum('bhqk,bkhe->bqhe', w.astype(v.dtype), v)


def diff_attention(q, k, v, q_pos, k_pos, rel_table, lam):
    b, s = q.shape[:2]
    if s > Q_BLOCK and s % Q_BLOCK == 0:
        nb = s // Q_BLOCK
        qb = jnp.moveaxis(q.reshape(b, nb, Q_BLOCK, *q.shape[2:]), 1, 0)
        pb = q_pos.reshape(nb, Q_BLOCK)
        out = lax.map(lambda a: diff_attn_block(a[0], k, v, a[1], k_pos, rel_table, lam), (qb, pb))
        return jnp.moveaxis(out, 0, 1).reshape(b, s, *out.shape[3:])
    return diff_attn_block(q, k, v, q_pos, k_pos, rel_table, lam)


def causal_conv(xbc, hist, w, bias):
    xp = jnp.concatenate([hist.astype(xbc.dtype), xbc], axis=1)
    out = lax.conv_general_dilated(xp, w[:, None, :].astype(xbc.dtype), window_strides=(1,),
                                   padding='VALID', dimension_numbers=('NWC', 'WIO', 'NWC'),
                                   feature_group_count=xbc.shape[-1])
    return out + bias.astype(xbc.dtype), xp[:, -(CONV_WIDTH - 1):]


def ssd_scan(x, dt, A, B, C, h0):
    f32 = jnp.float32
    b, s = x.shape[:2]
    L = min(SSD_CHUNK, s)
    nc = s // L
    G, R, P, N = N_SSM_GROUPS, N_SSM_HEADS // N_SSM_GROUPS, SSM_HEAD_DIM, SSM_STATE
    xdt = (x.astype(f32) * dt[..., None]).reshape(b, nc, L, G, R, P)
    a = (dt * A).reshape(b, nc, L, G, R)
    Bc = B.astype(f32).reshape(b, nc, L, G, N)
    Cc = C.astype(f32).reshape(b, nc, L, G, N)
    tri = jnp.tril(jnp.ones((L, L), dtype=bool))[None, :, :, None, None]

    def step(h, inp):
        xdt_c, a_c, B_c, C_c = inp
        a_cs = jnp.cumsum(a_c, axis=1)
        seg = a_cs[:, :, None] - a_cs[:, None, :]
        decay = jnp.exp(jnp.where(tri, seg, -jnp.inf))
        cb = jnp.einsum('btgn,bsgn->btsg', C_c, B_c)
        y = jnp.einsum('btsgr,bsgrp->btgrp', cb[..., None] * decay, xdt_c)
        y = y + jnp.einsum('btgn,bgrpn->btgrp', C_c, h) * jnp.exp(a_cs)[..., None]
        to_end = jnp.exp(a_cs[:, -1:] - a_cs)
        h = h * jnp.exp(a_cs[:, -1])[..., None, None] + jnp.einsum(
            'bsgn,bsgr,bsgrp->bgrpn', B_c, to_end, xdt_c)
        return h, y

    sw = lambda t: jnp.moveaxis(t, 1, 0)
    hN, ys = lax.scan(step, h0.astype(f32).reshape(b, G, R, P, N), (sw(xdt), sw(a), sw(Bc), sw(Cc)))
    y = jnp.moveaxis(ys, 0, 1).reshape(b, s, N_SSM_HEADS, P)
    return y, hN.reshape(b, N_SSM_HEADS, P, N)


def hybrid_layer(h, q_pos, k_past, v_past, conv_past, ssm_past, layer, rel_table,
                 norm_w, w_in, lam_q1, lam_k1, lam_q2, lam_k2, subln_w,
                 conv_w, conv_b, dt_bias, A_log, D_skip, ssm_norm_w, w_out):
    b, s, _ = h.shape
    f32 = jnp.float32
    u = rms_norm(h, norm_w)
    proj = u @ w_in.astype(u.dtype)
    q, k, v, g, z, xbc, dt = jnp.split(proj, SPLITS, axis=-1)
    q = q.reshape(b, s, N_ATT_HEADS, 2, ATT_HEAD_DIM)
    k = k.reshape(b, s, N_ATT_HEADS, 2, ATT_HEAD_DIM)
    v = v.reshape(b, s, N_ATT_HEADS, 2 * ATT_HEAD_DIM)
    if k_past is None:
        k_all, v_all, k_pos = k, v, q_pos
    else:
        k_all = jnp.concatenate([k_past.astype(k.dtype), k], axis=1)
        v_all = jnp.concatenate([v_past.astype(v.dtype), v], axis=1)
        k_pos = jnp.concatenate([jnp.arange(k_past.shape[1], dtype=jnp.int32), q_pos])
    lam0 = lambda_init(layer)
    lam = (jnp.exp(jnp.sum(lam_q1.astype(f32) * lam_k1.astype(f32)))
           - jnp.exp(jnp.sum(lam_q2.astype(f32) * lam_k2.astype(f32))) + lam0)
    att = diff_attention(q, k_all, v_all, q_pos, k_pos, rel_table, lam)
    att = rms_norm(att, subln_w) * (1.0 - lam0)
    att = att.reshape(b, s, D_ATT) * jax.nn.silu(g)
    xbc_c, conv_new = causal_conv(xbc, conv_past, conv_w, conv_b)
    xbc_c = jax.nn.silu(xbc_c)
    xs, Bm, Cm = jnp.split(xbc_c, [D_SSM, D_SSM + N_SSM_GROUPS * SSM_STATE], axis=-1)
    xs = xs.reshape(b, s, N_SSM_HEADS, SSM_HEAD_DIM)
    dt = jax.nn.softplus(dt.astype(f32) + dt_bias.astype(f32))
    A = -jnp.exp(A_log.astype(f32))
    y, ssm_new = ssd_scan(xs, dt, A, Bm.reshape(b, s, N_SSM_GROUPS, SSM_STATE),
                          Cm.reshape(b, s, N_SSM_GROUPS, SSM_STATE), ssm_past)
    y = y + D_skip.astype(f32)[:, None] * xs.astype(f32)
    y = y.reshape(b, s, D_SSM).astype(h.dtype) * jax.nn.silu(z)
    gs = D_SSM // N_SSM_GROUPS
    y = rms_norm(y.reshape(b, s, N_SSM_GROUPS, gs),
                 ssm_norm_w.reshape(N_SSM_GROUPS, gs)).reshape(b, s, D_SSM)
    out = jnp.concatenate([att, y], axis=-1) @ w_out.astype(h.dtype)
    return h + out, k, v, conv_new, ssm_new.astype(ssm_past.dtype)


def setup_inputs(seed: int = 0) -> dict:
    key = jax.random.key(seed)
    ks = jax.random.split(key, 24)
    nrm = jax.random.normal
    Hs = N_SSM_HEADS
    dt0 = jnp.exp(jax.random.uniform(ks[13], (DEPTH, Hs)) * (math.log(0.1) - math.log(0.001))
                  + math.log(0.001))
    return {
        'x_prompt': nrm(ks[0], (BATCH, SEQ, D_MODEL), jnp.float32),
        'x_sample': nrm(ks[1], (DEC_BATCH, DEC_SEQ, D_MODEL), jnp.float32),
        'cache_k': nrm(ks[2], (DEPTH, DEC_BATCH, PAST_LEN, N_ATT_HEADS, 2, ATT_HEAD_DIM), jnp.float32),
        'cache_v': nrm(ks[3], (DEPTH, DEC_BATCH, PAST_LEN, N_ATT_HEADS, 2 * ATT_HEAD_DIM), jnp.float32),
        'cache_conv': nrm(ks[4], (DEPTH, DEC_BATCH, CONV_WIDTH - 1, D_CONV), jnp.float32),
        'state_ssm': 0.1 * nrm(ks[5], (DEPTH, DEC_BATCH, Hs, SSM_HEAD_DIM, SSM_STATE), jnp.float32),
        'rel_bias': 0.5 * nrm(ks[6], (N_REL_BUCKETS, N_ATT_HEADS), jnp.float32),
        'norm_w': 1.0 + 0.01 * nrm(ks[7], (DEPTH, D_MODEL), jnp.float32),
        'w_in': nrm(ks[8], (DEPTH, D_MODEL, D_IN_PROJ), jnp.float32) * D_MODEL ** -0.5,
        'lambda_q1': 0.1 * nrm(ks[9], (DEPTH, ATT_HEAD_DIM), jnp.float32),
        'lambda_k1': 0.1 * nrm(ks[10], (DEPTH, ATT_HEAD_DIM), jnp.float32),
        'lambda_q2': 0.1 * nrm(ks[11], (DEPTH, ATT_HEAD_DIM), jnp.float32),
        'lambda_k2': 0.1 * nrm(ks[12], (DEPTH, ATT_HEAD_DIM), jnp.float32),
        'subln_w': 1.0 + 0.01 * nrm(ks[14], (DEPTH, 2 * ATT_HEAD_DIM), jnp.float32),
        'conv_w': 0.5 * nrm(ks[15], (DEPTH, CONV_WIDTH, D_CONV), jnp.float32),
        'conv_b': 0.01 * nrm(ks[16], (DEPTH, D_CONV), jnp.float32),
        'dt_bias': dt0 + jnp.log(-jnp.expm1(-dt0)),
        'A_log': jnp.log(jax.random.uniform(ks[17], (DEPTH, Hs), minval=1.0, maxval=16.0)),
        'D_skip': 1.0 + 0.01 * nrm(ks[18], (DEPTH, Hs), jnp.float32),
        'ssm_norm_w': 1.0 + 0.01 * nrm(ks[19], (DEPTH, D_SSM), jnp.float32),
        'w_out': nrm(ks[20], (DEPTH, D_MIX, D_MODEL), jnp.float32) * D_MIX ** -0.5,
        'final_norm_w': 1.0 + 0.01 * nrm(ks[21], (D_MODEL,), jnp.float32),
    }


def reference(x_prompt, x_sample, cache_k, cache_v, cache_conv, state_ssm, rel_bias, norm_w, w_in,
              lambda_q1, lambda_k1, lambda_q2, lambda_k2, subln_w, conv_w, conv_b, dt_bias, A_log,
              D_skip, ssm_norm_w, w_out, final_norm_w):
    bp, sp = x_prompt.shape[:2]
    pos_p = jnp.arange(sp, dtype=jnp.int32)
    conv0 = jnp.zeros((bp, CONV_WIDTH - 1, D_CONV), x_prompt.dtype)
    ssm0 = jnp.zeros((bp, N_SSM_HEADS, SSM_HEAD_DIM, SSM_STATE), state_ssm.dtype)
    past_len = cache_k.shape[2]
    pos_s = past_len + jnp.arange(x_sample.shape[1], dtype=jnp.int32)
    hp, hs = x_prompt, x_sample
    kp, vp, cp, sp_l, ksl, vsl, csl, ssl = [], [], [], [], [], [], [], []
    for l in range(DEPTH):
        params = (norm_w[l], w_in[l], lambda_q1[l], lambda_k1[l], lambda_q2[l], lambda_k2[l],
                  subln_w[l], conv_w[l], conv_b[l], dt_bias[l], A_log[l], D_skip[l],
                  ssm_norm_w[l], w_out[l])
        hp, k1, v1, c1, s1 = hybrid_layer(hp, pos_p, None, None, conv0, ssm0, l, rel_bias, *params)
        hs, k2, v2, c2, s2 = hybrid_layer(hs, pos_s, cache_k[l], cache_v[l], cache_conv[l],
                                          state_ssm[l], l, rel_bias, *params)
        kp.append(k1); vp.append(v1); cp.append(c1); sp_l.append(s1)
        ksl.append(k2); vsl.append(v2); csl.append(c2); ssl.append(s2)
    y_prompt = rms_norm(hp, final_norm_w)
    y_sample = rms_norm(hs, final_norm_w)
    return (y_prompt, y_sample, jnp.stack(kp), jnp.stack(vp), jnp.stack(cp), jnp.stack(sp_l),
            jnp.stack(ksl), jnp.stack(vsl), jnp.stack(csl), jnp.stack(ssl))
```

```python
import functools
import math

import jax
import jax.numpy as jnp
from jax import lax
from jax.experimental import pallas as pl
from jax.experimental.pallas import tpu as pltpu

F32 = jnp.float32
BF16 = jnp.bfloat16

D_MODEL = 2048
CHUNK = 64
NORM_EPS = 1e-5
N_ATT_HEADS = 8
ATT_HEAD_DIM = 128
D_ATT = N_ATT_HEADS * 2 * ATT_HEAD_DIM
N_REL_BUCKETS = 32
REL_MAX_DIST = 128
D_SSM = 2048
SSM_HEAD_DIM = 64
N_SSM_HEADS = D_SSM // SSM_HEAD_DIM
N_SSM_GROUPS = 4
HEADS_PER_GROUP = N_SSM_HEADS // N_SSM_GROUPS
SSM_STATE = 128
CONV_WIDTH = 4
D_BC = N_SSM_GROUPS * SSM_STATE
D_CONV = D_SSM + 2 * D_BC
D_MIX = D_ATT + D_SSM
D_QKV = 3 * D_ATT
D_P32 = 4 * D_ATT + D_SSM + D_CONV - D_ATT
COL_K, COL_V, COL_G, COL_Z, COL_XS = 0, D_ATT, 2 * D_ATT, 3 * D_ATT, 3 * D_ATT + D_SSM
COL_B, COL_C = COL_XS + D_SSM, COL_XS + D_SSM + D_BC

LANES = 128
SUBLANES = 8
VMEM_LIMIT = 56 * 1024 * 1024
MASK_VALUE = -1e30
SSD_L = 128


def _silu(x):
    return x * (1.0 / (1.0 + jnp.exp(-x)))


def _softplus(x):
    return jnp.maximum(x, 0.0) + jnp.log1p(jnp.exp(-jnp.abs(x)))


def _inproj_kernel(x_ref, nw_ref, w_ref, wdt_ref, p_ref, qkv_ref, dt_ref, u_ref, *, n_qkv_tiles, slab):
    j = pl.program_id(1)
    tm = x_ref.shape[0]

    @pl.when(j == 0)
    def _():
        def body(r, carry):
            rows = pl.ds(pl.multiple_of(r * slab, slab), slab)
            x = x_ref[rows, :]
            ms = jnp.mean(x * x, axis=-1, keepdims=True)
            u = x * lax.rsqrt(ms + NORM_EPS) * nw_ref[...]
            u_ref[rows, :] = u.astype(BF16)
            return carry
        lax.fori_loop(0, tm // slab, body, 0)
        dt_ref[...] = jnp.dot(u_ref[...], wdt_ref[...], preferred_element_type=F32)

    res = jnp.dot(u_ref[...], w_ref[...], preferred_element_type=F32)
    p_ref[...] = res

    @pl.when(j < n_qkv_tiles)
    def _():
        qkv_ref[...] = res.astype(BF16)


def _in_proj(x2d, norm_w, w_main, w_dt, *, tm, tn):
    m = x2d.shape[0]
    n_tiles = w_main.shape[1] // tn
    n_qkv_tiles = D_QKV // tn
    n_q_tiles = D_ATT // tn
    slab = min(tm, 256)
    kern = functools.partial(_inproj_kernel, n_qkv_tiles=n_qkv_tiles, slab=slab)
    return pl.pallas_call(
        kern,
        grid=(m // tm, n_tiles),
        in_specs=[
            pl.BlockSpec((tm, D_MODEL), lambda i, j: (i, 0)),
            pl.BlockSpec((1, D_MODEL), lambda i, j: (0, 0)),
            pl.BlockSpec((D_MODEL, tn), lambda i, j: (0, j)),
            pl.BlockSpec((D_MODEL, LANES), lambda i, j: (0, 0)),
        ],
        out_specs=[
            pl.BlockSpec((tm, tn), lambda i, j: (i, jnp.maximum(j - n_q_tiles, 0))),
            pl.BlockSpec((tm, tn), lambda i, j: (i, jnp.minimum(j, n_qkv_tiles - 1))),
            pl.BlockSpec((tm, LANES), lambda i, j: (i, 0)),
        ],
        out_shape=[
            jax.ShapeDtypeStruct((m, D_P32), F32),
            jax.ShapeDtypeStruct((m, D_QKV), BF16),
            jax.ShapeDtypeStruct((m, LANES), F32),
        ],
        scratch_shapes=[pltpu.VMEM((tm, D_MODEL), BF16)],
        compiler_params=pltpu.CompilerParams(
            dimension_semantics=("parallel", "arbitrary"), vmem_limit_bytes=VMEM_LIMIT),
        name="in_proj",
    )(x2d, norm_w, w_main, w_dt)


def _rel_bucket(rel):
    half = N_REL_BUCKETS // 2
    max_exact = half // 2
    ret = jnp.where(rel > 0, half, 0)
    n = jnp.abs(rel)
    nf = jnp.maximum(n, 1).astype(F32)
    large = max_exact + (jnp.log(nf / max_exact) / math.log(REL_MAX_DIST / max_exact)
                         * (half - max_exact)).astype(jnp.int32)
    large = jnp.minimum(large, half - 1)
    return ret + jnp.where(n < max_exact, n, large)


def _lambda_init(layer):
    return 0.8 - 0.6 * math.exp(-0.3 * layer)


def _lam_from_refs(lq1, lk1, lq2, lk2, lam0):
    return (jnp.exp(jnp.sum(lq1[...] * lk1[...], axis=-1, keepdims=True))
            - jnp.exp(jnp.sum(lq2[...] * lk2[...], axis=-1, keepdims=True)) + lam0)


def _attn_epilogue(a1, a2, lam, g, subln_w, lam0):
    o = a1 - lam * a2
    ms = jnp.mean(o * o, axis=-1, keepdims=True)
    o = o * lax.rsqrt(ms + NORM_EPS) * subln_w
    o = o * (1.0 - lam0)
    return o * _silu(g)


def _prompt_attn_kernel(q_ref, k_ref, v_ref, g_ref, bias_ref, sw_ref, lq1, lk1, lq2, lk2,
                        o_ref, m_ref, l_ref, acc_ref, *, tile, lam0):
    qi = pl.program_id(2)
    d = ATT_HEAD_DIM
    scale = d ** -0.5
    q = q_ref[0]
    qs = (q[:, :d], q[:, d:])
    m_ref[...] = jnp.full(m_ref.shape, MASK_VALUE, F32)
    l_ref[...] = jnp.zeros(l_ref.shape, F32)
    acc_ref[...] = jnp.zeros(acc_ref.shape, F32)

    def kv_tile(j, bias):
        rows = pl.ds(pl.multiple_of(j * tile, tile), tile)
        k = k_ref[0, rows, :]
        v = v_ref[0, rows, :]
        for mi in range(2):
            s = lax.dot_general(qs[mi], k[:, mi * d:(mi + 1) * d], (((1,), (1,)), ((), ())),
                                preferred_element_type=F32) * scale
            if bias is not None:
                s = s + bias
            m_old = m_ref[mi]
            m_new = jnp.maximum(m_old, jnp.max(s, axis=-1, keepdims=True))
            alpha = jnp.exp(m_old - m_new)
            p = jnp.exp(s - m_new)
            l_ref[mi] = alpha * l_ref[mi] + jnp.sum(p, axis=-1, keepdims=True)
            acc_ref[mi] = alpha * acc_ref[mi] + jnp.dot(p.astype(BF16), v, preferred_element_type=F32)
            m_ref[mi] = m_new

    def far_body(j, carry):
        kv_tile(j, None)
        return carry
    lax.fori_loop(0, jnp.maximum(qi - 1, 0), far_body, 0)

    @pl.when(qi > 0)
    def _():
        kv_tile(qi - 1, bias_ref[1, 0])

    kv_tile(qi, bias_ref[0, 0])

    lam = _lam_from_refs(lq1, lk1, lq2, lk2, lam0)
    a1 = acc_ref[0] / l_ref[0]
    a2 = acc_ref[1] / l_ref[1]
    o_ref[0] = _attn_epilogue(a1, a2, lam, g_ref[0], sw_ref[...], lam0).astype(BF16)


def _prompt_bias_tiles(rel_table, tile):
    assert tile >= REL_MAX_DIST and tile % CHUNK == 0
    qpos = jnp.arange(tile, dtype=jnp.int32)[:, None]
    kpos = jnp.arange(tile, dtype=jnp.int32)[None, :]
    table = rel_table.astype(F32)
    far = table[_rel_bucket(jnp.full((1, 1), -(tile + 1), jnp.int32))][0, 0]
    diag = table[_rel_bucket(kpos - qpos)] - far
    off = table[_rel_bucket(kpos - qpos - tile)] - far
    visible = (kpos // CHUNK) <= (qpos // CHUNK)
    diag = jnp.where(visible[:, :, None], diag, MASK_VALUE)
    return jnp.transpose(jnp.stack([diag, off]), (0, 3, 1, 2))


def _prompt_attention(qkv, p32, bias_tiles, subln_w, lq1, lk1, lq2, lk2, *, tile, lam0):
    b, s, _ = qkv.shape
    hw = 2 * ATT_HEAD_DIM
    nh = N_ATT_HEADS
    kern = functools.partial(_prompt_attn_kernel, tile=tile, lam0=lam0)
    vec = pl.BlockSpec((1, ATT_HEAD_DIM), lambda bi, h, qi: (0, 0))
    return pl.pallas_call(
        kern,
        grid=(b, nh, s // tile),
        in_specs=[
            pl.BlockSpec((1, tile, hw), lambda bi, h, qi: (bi, qi, h)),
            pl.BlockSpec((1, s, hw), lambda bi, h, qi: (bi, 0, nh + h)),
            pl.BlockSpec((1, s, hw), lambda bi, h, qi: (bi, 0, 2 * nh + h)),
            pl.BlockSpec((1, tile, hw), lambda bi, h, qi: (bi, qi, COL_G // hw + h)),
            pl.BlockSpec((2, 1, tile, tile), lambda bi, h, qi: (0, h, 0, 0)),
            pl.BlockSpec((1, hw), lambda bi, h, qi: (0, 0)),
            vec, vec, vec, vec,
        ],
        out_specs=pl.BlockSpec((1, tile, hw), lambda bi, h, qi: (bi, qi, h)),
        out_shape=jax.ShapeDtypeStruct((b, s, D_ATT), BF16),
        scratch_shapes=[
            pltpu.VMEM((2, tile, 1), F32),
            pltpu.VMEM((2, tile, 1), F32),
            pltpu.VMEM((2, tile, hw), F32),
        ],
        compiler_params=pltpu.CompilerParams(
            dimension_semantics=("parallel", "parallel", "arbitrary"), vmem_limit_bytes=VMEM_LIMIT),
        name="prompt_attention",
    )(qkv, qkv, qkv, p32, bias_tiles, subln_w, lq1, lk1, lq2, lk2)


def _sample_attn_kernel(q_ref, kn_ref, vn_ref, kp_ref, vp_ref, g_ref, bp_ref, bn_ref, sw_ref,
                        lq1, lk1, lq2, lk2, o_ref, *, lam0):
    d = ATT_HEAD_DIM
    scale = d ** -0.5
    q = q_ref[0]
    kn = kn_ref[0]
    vn = vn_ref[0]
    kp = kp_ref[0].astype(BF16)
    vp = vp_ref[0].astype(BF16)
    nt = (((1,), (1,)), ((), ()))
    outs = []
    for mi in range(2):
        qm = q[:, mi * d:(mi + 1) * d]
        sp = lax.dot_general(qm, kp[:, mi * d:(mi + 1) * d], nt, preferred_element_type=F32) * scale
        sn = lax.dot_general(qm, kn[:, mi * d:(mi + 1) * d], nt, preferred_element_type=F32) * scale
        sp = sp + bp_ref[0]
        sn = sn + bn_ref[0]
        m = jnp.maximum(jnp.max(sp, axis=-1, keepdims=True), jnp.max(sn, axis=-1, keepdims=True))
        pp = jnp.exp(sp - m)
        pn = jnp.exp(sn - m)
        l = jnp.sum(pp, axis=-1, keepdims=True) + jnp.sum(pn, axis=-1, keepdims=True)
        acc = (jnp.dot(pp.astype(BF16), vp, preferred_element_type=F32)
               + jnp.dot(pn.astype(BF16), vn, preferred_element_type=F32))
        outs.append(acc / l)
    lam = _lam_from_refs(lq1, lk1, lq2, lk2, lam0)
    o_ref[0] = _attn_epilogue(outs[0], outs[1], lam, g_ref[0], sw_ref[...], lam0).astype(BF16)


def _sample_bias(rel_table, past_len, s):
    qpos = past_len + jnp.arange(s, dtype=jnp.int32)[:, None]
    kpos = jnp.arange(past_len + s, dtype=jnp.int32)[None, :]
    bias = rel_table.astype(F32)[_rel_bucket(kpos - qpos)]
    visible = (kpos // CHUNK) <= (qpos // CHUNK)
    bias = jnp.where(visible[:, :, None], bias, MASK_VALUE)
    return jnp.transpose(bias, (2, 0, 1))


def _sample_attention(qkv, p32, k_past, v_past, bias, subln_w, lq1, lk1, lq2, lk2, *, lam0):
    b, s, _ = qkv.shape
    past = k_past.shape[1]
    hw = 2 * ATT_HEAD_DIM
    nh = N_ATT_HEADS
    bias_p = bias[:, :, :past]
    bias_n = bias[:, :, past:]
    kern = functools.partial(_sample_attn_kernel, lam0=lam0)
    vec = pl.BlockSpec((1, ATT_HEAD_DIM), lambda bi, h: (0, 0))
    return pl.pallas_call(
        kern,
        grid=(b, nh),
        in_specs=[
            pl.BlockSpec((1, s, hw), lambda bi, h: (bi, 0, h)),
            pl.BlockSpec((1, s, hw), lambda bi, h: (bi, 0, nh + h)),
            pl.BlockSpec((1, s, hw), lambda bi, h: (bi, 0, 2 * nh + h)),
            pl.BlockSpec((1, past, hw), lambda bi, h: (bi, 0, h)),
            pl.BlockSpec((1, past, hw), lambda bi, h: (bi, 0, h)),
            pl.BlockSpec((1, s, hw), lambda bi, h: (bi, 0, COL_G // hw + h)),
            pl.BlockSpec((1, s, past), lambda bi, h: (h, 0, 0)),
            pl.BlockSpec((1, s, s), lambda bi, h: (h, 0, 0)),
            pl.BlockSpec((1, hw), lambda bi, h: (0, 0)),
            vec, vec, vec, vec,
        ],
        out_specs=pl.BlockSpec((1, s, hw), lambda bi, h: (bi, 0, h)),
        out_shape=jax.ShapeDtypeStruct((b, s, D_ATT), BF16),
        compiler_params=pltpu.CompilerParams(
            dimension_semantics=("parallel", "parallel"), vmem_limit_bytes=VMEM_LIMIT),
        name="sample_attention",
    )(qkv, qkv, qkv, k_past, v_past, p32, bias_p, bias_n, subln_w, lq1, lk1, lq2, lk2)


def _ssd_kernel(z_ref, xs_ref, b_ref, c_ref, dtT_ref, hist_ref, h0_ref, cw_ref, cb_ref, dtb_ref,
                alog_ref, dskip_ref, nw_ref, y_ref, hout_ref, h_s, carry_s, xT_s, yT_s, *, valid):
    c = pl.program_id(1)
    L = xs_ref.shape[1]
    P, N, R = SSM_HEAD_DIM, SSM_STATE, HEADS_PER_GROUP

    @pl.when(c == 0)
    def _():
        h_s[...] = h0_ref[0]
        carry_s[...] = hist_ref[0]

    row8 = lax.broadcasted_iota(jnp.int32, (SUBLANES, 1), 0)

    def conv_silu(x, lo, hi):
        carry = carry_s[:, lo:hi]
        acc = x * cw_ref[CONV_WIDTH - 1:CONV_WIDTH, lo:hi] + cb_ref[:, lo:hi]
        for k in range(1, CONV_WIDTH):
            xk = pltpu.roll(x, k, axis=0)
            ck = pltpu.roll(carry, k, axis=0)
            head = jnp.where(row8 < k, ck, xk[:SUBLANES])
            xk = jnp.concatenate([head, xk[SUBLANES:]], axis=0)
            acc = acc + xk * cw_ref[CONV_WIDTH - 1 - k:CONV_WIDTH - k, lo:hi]
        carry_s[:, lo:hi] = x[L - SUBLANES:]
        return _silu(acc)

    xs = conv_silu(xs_ref[0], 0, D_SSM)
    bm = conv_silu(b_ref[0], D_SSM, D_SSM + D_BC)
    cm = conv_silu(c_ref[0], D_SSM + D_BC, D_CONV)

    dt = _softplus(dtT_ref[0] + dtb_ref[...])
    if valid < L:
        dt = jnp.where(lax.broadcasted_iota(jnp.int32, dt.shape, 1) < valid, dt, 0.0)
    a = dt * (-jnp.exp(alog_ref[...]))
    s_idx = lax.broadcasted_iota(jnp.int32, (L, L), 0)
    t_idx = lax.broadcasted_iota(jnp.int32, (L, L), 1)
    causal = s_idx <= t_idx
    upper = jnp.where(causal, 1.0, 0.0).astype(BF16)
    a1 = a.astype(BF16)
    r1 = a - a1.astype(F32)
    a2 = r1.astype(BF16)
    a3 = (r1 - a2.astype(F32)).astype(BF16)
    acs = (jnp.dot(a1, upper, preferred_element_type=F32)
           + jnp.dot(a2, upper, preferred_element_type=F32)
           + jnp.dot(a3, upper, preferred_element_type=F32))
    tot = acs[:, L - 1:L]
    e_row = jnp.exp(acs)
    w_row = dt * jnp.exp(tot - acs)
    d_row = jnp.broadcast_to(jnp.exp(tot), (N_SSM_HEADS, N))
    acs_col = jnp.concatenate([acs, jnp.zeros((L - N_SSM_HEADS, L), F32)], axis=0).T

    xT_s[...] = xs.T

    for g in range(N_SSM_GROUPS):
        bg = bm[:, g * N:(g + 1) * N].astype(BF16)
        cgT = cm[:, g * N:(g + 1) * N].T.astype(BF16)
        cbT = jnp.dot(bg, cgT, preferred_element_type=F32)
        for r8 in range(R):
            r = g * R + r8
            rows = slice(r * P, (r + 1) * P)
            seg = acs[r:r + 1, :] - acs_col[:, r:r + 1]
            decay = jnp.exp(jnp.where(causal, seg, MASK_VALUE))
            mT = (cbT * decay).astype(BF16)
            xr = xT_s[rows, :]
            hr = h_s[rows, :]
            y_intra = jnp.dot((xr * dt[r:r + 1, :]).astype(BF16), mT, preferred_element_type=F32)
            y_inter = jnp.dot(hr.astype(BF16), cgT, preferred_element_type=F32)
            yT_s[rows, :] = y_intra + y_inter * e_row[r:r + 1, :]
            upd = jnp.dot((xr * w_row[r:r + 1, :]).astype(BF16), bg, preferred_element_type=F32)
            h_s[rows, :] = hr * d_row[r:r + 1, :] + upd

    y = yT_s[...].T
    y = y + dskip_ref[...] * xs
    y = y * _silu(z_ref[0])
    gs = D_SSM // N_SSM_GROUPS
    outs = []
    for g in range(N_SSM_GROUPS):
        yg = y[:, g * gs:(g + 1) * gs]
        ms = jnp.mean(yg * yg, axis=-1, keepdims=True)
        outs.append(yg * lax.rsqrt(ms + NORM_EPS) * nw_ref[:, g * gs:(g + 1) * gs])
    y_ref[0] = jnp.concatenate(outs, axis=-1).astype(BF16)

    @pl.when(c == pl.num_programs(1) - 1)
    def _():
        hout_ref[0] = h_s[...]


def _ssd(src, cols, dtT, hist8, h0, conv_w, conv_b, dt_bias, a_log, d_full, norm_w, *, valid):
    b, s, _ = src.shape
    L = SSD_L
    col_z, col_xs, col_b, col_c = cols
    kern = functools.partial(_ssd_kernel, valid=valid)
    const2 = lambda shape: pl.BlockSpec(shape, lambda bi, c: (0, 0))
    hp = N_SSM_HEADS * SSM_HEAD_DIM
    return pl.pallas_call(
        kern,
        grid=(b, s // L),
        in_specs=[
            pl.BlockSpec((1, L, D_SSM), lambda bi, c: (bi, c, col_z // D_SSM)),
            pl.BlockSpec((1, L, D_SSM), lambda bi, c: (bi, c, col_xs // D_SSM)),
            pl.BlockSpec((1, L, D_BC), lambda bi, c: (bi, c, col_b // D_BC)),
            pl.BlockSpec((1, L, D_BC), lambda bi, c: (bi, c, col_c // D_BC)),
            pl.BlockSpec((1, N_SSM_HEADS, L), lambda bi, c: (bi, 0, c)),
            pl.BlockSpec((1, SUBLANES, D_CONV), lambda bi, c: (bi, 0, 0)),
            pl.BlockSpec((1, hp, SSM_STATE), lambda bi, c: (bi, 0, 0)),
            const2((CONV_WIDTH, D_CONV)),
            const2((1, D_CONV)),
            const2((N_SSM_HEADS, 1)),
            const2((N_SSM_HEADS, 1)),
            const2((1, D_SSM)),
            const2((1, D_SSM)),
        ],
        out_specs=[
            pl.BlockSpec((1, L, D_SSM), lambda bi, c: (bi, c, 0)),
            pl.BlockSpec((1, hp, SSM_STATE), lambda bi, c: (bi, 0, 0)),
        ],
        out_shape=[
            jax.ShapeDtypeStruct((b, s, D_SSM), BF16),
            jax.ShapeDtypeStruct((b, hp, SSM_STATE), F32),
        ],
        scratch_shapes=[
            pltpu.VMEM((hp, SSM_STATE), F32),
            pltpu.VMEM((SUBLANES, D_CONV), F32),
            pltpu.VMEM((D_SSM, L), F32),
            pltpu.VMEM((D_SSM, L), F32),
        ],
        compiler_params=pltpu.CompilerParams(
            dimension_semantics=("parallel", "arbitrary"), vmem_limit_bytes=VMEM_LIMIT),
        name="ssd",
    )(src, src, src, src, dtT, hist8, h0, conv_w, conv_b, dt_bias, a_log, d_full, norm_w)


def _outproj_kernel(a_ref, y_ref, x_ref, w1_ref, w2_ref, fw_ref, o_ref):
    acc = (jnp.dot(a_ref[...], w1_ref[...], preferred_element_type=F32)
           + jnp.dot(y_ref[...], w2_ref[...], preferred_element_type=F32))
    h = x_ref[...] + acc
    ms = jnp.mean(h * h, axis=-1, keepdims=True)
    o_ref[...] = h * lax.rsqrt(ms + NORM_EPS) * fw_ref[...]


def _out_proj(att, y, x2d, w1, w2, final_w, *, tm):
    m = x2d.shape[0]
    const = pl.BlockSpec((D_ATT, D_MODEL), lambda i: (0, 0), pipeline_mode=pl.Buffered(1))
    return pl.pallas_call(
        _outproj_kernel,
        grid=(m // tm,),
        in_specs=[
            pl.BlockSpec((tm, D_ATT), lambda i: (i, 0)),
            pl.BlockSpec((tm, D_SSM), lambda i: (i, 0)),
            pl.BlockSpec((tm, D_MODEL), lambda i: (i, 0)),
            const, const,
            pl.BlockSpec((1, D_MODEL), lambda i: (0, 0)),
        ],
        out_specs=pl.BlockSpec((tm, D_MODEL), lambda i: (i, 0)),
        out_shape=jax.ShapeDtypeStruct((m, D_MODEL), F32),
        compiler_params=pltpu.CompilerParams(
            dimension_semantics=("parallel",), vmem_limit_bytes=VMEM_LIMIT),
        name="out_proj",
    )(att, y, x2d, w1, w2, final_w)


def _tiles(m):
    tm = min(m, 1024)
    assert m % tm == 0
    return tm


def _layer(h, k_past, v_past, conv_past, ssm_past, layer, rel_bias, norm_w, w_main, w_dt, lq1, lk1, lq2, lk2,
           subln_w, conv_w, conv_b, dt_bias, a_log, d_full, ssm_norm_w, w_out1, w_out2, out_norm_w):
    b, s, _ = h.shape
    m = b * s
    lam0 = _lambda_init(layer)
    x2d = h.reshape(m, D_MODEL)
    p32, qkv, dt_raw = _in_proj(x2d, norm_w, w_main, w_dt, tm=_tiles(m), tn=512)
    k_new = p32[:, COL_K:COL_K + D_ATT].reshape(b, s, N_ATT_HEADS, 2, ATT_HEAD_DIM)
    v_new = p32[:, COL_V:COL_V + D_ATT].reshape(b, s, N_ATT_HEADS, 2 * ATT_HEAD_DIM)
    p32 = p32.reshape(b, s, D_P32)
    qkv = qkv.reshape(b, s, D_QKV)

    if k_past is None:
        tile = min(s, 512)
        att = _prompt_attention(qkv, p32, _prompt_bias_tiles(rel_bias, tile), subln_w, lq1, lk1, lq2, lk2,
                                tile=tile, lam0=lam0)
    else:
        past = k_past.shape[1]
        att = _sample_attention(qkv, p32, k_past.reshape(b, past, D_ATT), v_past.reshape(b, past, D_ATT),
                                _sample_bias(rel_bias, past, s), subln_w, lq1, lk1, lq2, lk2, lam0=lam0)

    conv_new = p32[:, s - (CONV_WIDTH - 1):, COL_XS:COL_XS + D_CONV]
    if CONV_WIDTH - 1 > s:
        conv_new = jnp.concatenate([conv_past, p32[:, :, COL_XS:]], axis=1)[:, -(CONV_WIDTH - 1):]
    hist8 = jnp.pad(conv_past, ((0, 0), (SUBLANES - (CONV_WIDTH - 1), 0), (0, 0)))
    dtT = jnp.transpose(dt_raw[:, :N_SSM_HEADS].reshape(b, s, N_SSM_HEADS), (0, 2, 1))
    h0 = ssm_past.reshape(b, N_SSM_HEADS * SSM_HEAD_DIM, SSM_STATE)
    if s % SSD_L == 0:
        src, cols = p32, (COL_Z, COL_XS, COL_B, COL_C)
    else:
        assert s < SSD_L
        pad = SSD_L - s
        src = jnp.pad(p32[:, :, COL_Z:], ((0, 0), (0, pad), (0, 0)))
        cols = (0, D_SSM, 2 * D_SSM, 2 * D_SSM + D_BC)
        dtT = jnp.pad(dtT, ((0, 0), (0, 0), (0, pad)))
    y, ssm_new = _ssd(src, cols, dtT, hist8, h0, conv_w, conv_b, dt_bias, a_log, d_full, ssm_norm_w,
                      valid=min(s, SSD_L))
    y = y[:, :s].reshape(m, D_SSM)
    ssm_new = ssm_new.reshape(b, N_SSM_HEADS, SSM_HEAD_DIM, SSM_STATE)

    out = _out_proj(att.reshape(m, D_ATT), y, x2d, w_out1, w_out2, out_norm_w, tm=min(m, 512))
    return out.reshape(b, s, D_MODEL), k_new, v_new, conv_new, ssm_new


def kernel(x_prompt, x_sample, cache_k, cache_v, cache_conv, state_ssm, rel_bias, norm_w, w_in, lambda_q1,
           lambda_k1, lambda_q2, lambda_k2, subln_w, conv_w, conv_b, dt_bias, A_log, D_skip, ssm_norm_w, w_out,
           final_norm_w):
    depth = w_in.shape[0]
    assert depth == 1, "the final norm is fused into the (single) layer's output projection"
    bp = x_prompt.shape[0]
    l = 0
    col_q = 0
    col_dt = 4 * D_ATT + D_SSM + D_CONV
    w = w_in[l]
    w_main = w[:, col_q:col_dt].astype(BF16)
    w_dt = jnp.pad(w[:, col_dt:], ((0, 0), (0, LANES - N_SSM_HEADS))).astype(BF16)
    row = lambda t: t.reshape(1, -1).astype(F32)
    col = lambda t: t.reshape(-1, 1).astype(F32)
    params = (rel_bias, row(norm_w[l]), w_main, w_dt, row(lambda_q1[l]), row(lambda_k1[l]), row(lambda_q2[l]),
              row(lambda_k2[l]), row(subln_w[l]), conv_w[l].astype(F32), row(conv_b[l]), col(dt_bias[l]),
              col(A_log[l]), row(jnp.repeat(D_skip[l], SSM_HEAD_DIM)), row(ssm_norm_w[l]),
              w_out[l, :D_ATT].astype(BF16), w_out[l, D_ATT:].astype(BF16), row(final_norm_w))
    conv0 = jnp.zeros((bp, CONV_WIDTH - 1, D_CONV), x_prompt.dtype)
    ssm0 = jnp.zeros((bp, N_SSM_HEADS, SSM_HEAD_DIM, SSM_STATE), state_ssm.dtype)
    yp, k1, v1, c1, s1 = _layer(x_prompt, None, None, conv0, ssm0, l, *params)
    ys, k2, v2, c2, s2 = _layer(x_sample, cache_k[l], cache_v[l], cache_conv[l], state_ssm[l], l, *params)
    return (yp, ys, k1[None], v1[None], c1[None], s1[None], k2[None], v2[None], c2[None], s2[None])
```

```python
import functools
import math

import jax
import jax.numpy as jnp
from jax import lax
from jax.experimental import pallas as pl
from jax.experimental.pallas import tpu as pltpu

F32 = jnp.float32
BF16 = jnp.bfloat16

D_MODEL = 2048
CHUNK = 64
NORM_EPS = 1e-5
N_ATT_HEADS = 8
ATT_HEAD_DIM = 128
D_ATT = N_ATT_HEADS * 2 * ATT_HEAD_DIM
N_REL_BUCKETS = 32
REL_MAX_DIST = 128
D_SSM = 2048
SSM_HEAD_DIM = 64
N_SSM_HEADS = D_SSM // SSM_HEAD_DIM
N_SSM_GROUPS = 4
HEADS_PER_GROUP = N_SSM_HEADS // N_SSM_GROUPS
SSM_STATE = 128
CONV_WIDTH = 4
D_BC = N_SSM_GROUPS * SSM_STATE
D_CONV = D_SSM + 2 * D_BC
D_MIX = D_ATT + D_SSM
D_QKV = 3 * D_ATT
D_P32 = 4 * D_ATT + D_SSM + D_CONV - D_ATT
COL_K, COL_V, COL_G, COL_Z, COL_XS = 0, D_ATT, 2 * D_ATT, 3 * D_ATT, 3 * D_ATT + D_SSM
COL_B, COL_C = COL_XS + D_SSM, COL_XS + D_SSM + D_BC

LANES = 128
SUBLANES = 8
VMEM_LIMIT = 56 * 1024 * 1024
MASK_VALUE = -1e30
LOG2E = math.log2(math.e)
Q_SCALE = ATT_HEAD_DIM ** -0.5 * LOG2E
SSD_L = 128


def _silu(x):
    return x * (1.0 / (1.0 + jnp.exp(-x)))


def _softplus(x):
    return jnp.maximum(x, 0.0) + jnp.log1p(jnp.exp(-jnp.abs(x)))


def _inproj_kernel(x_ref, nw_ref, w_ref, wdt_ref, p_ref, qkv_ref, dt_ref, u_ref, *, n_q_tiles, n_qkv_tiles,
                   slab):
    j = pl.program_id(1)
    tm = x_ref.shape[0]

    @pl.when(j == 0)
    def _():
        def body(r, carry):
            rows = pl.ds(pl.multiple_of(r * slab, slab), slab)
            x = x_ref[rows, :]
            ms = jnp.mean(x * x, axis=-1, keepdims=True)
            u = x * lax.rsqrt(ms + NORM_EPS) * nw_ref[...]
            u_ref[rows, :] = u.astype(BF16)
            return carry
        lax.fori_loop(0, tm // slab, body, 0)
        dt_ref[...] = jnp.dot(u_ref[...], wdt_ref[...], preferred_element_type=F32)

    res = jnp.dot(u_ref[...], w_ref[...], preferred_element_type=F32)
    p_ref[...] = res

    @pl.when(j < n_q_tiles)
    def _():
        qkv_ref[...] = (res * Q_SCALE).astype(BF16)

    @pl.when(jnp.logical_and(j >= n_q_tiles, j < n_qkv_tiles))
    def _():
        qkv_ref[...] = res.astype(BF16)


def _in_proj(x2d, norm_w, w_main, w_dt, *, tm, tn):
    m = x2d.shape[0]
    n_tiles = w_main.shape[1] // tn
    n_qkv_tiles = D_QKV // tn
    n_q_tiles = D_ATT // tn
    slab = min(tm, 256)
    kern = functools.partial(_inproj_kernel, n_q_tiles=n_q_tiles, n_qkv_tiles=n_qkv_tiles, slab=slab)
    return pl.pallas_call(
        kern,
        grid=(m // tm, n_tiles),
        in_specs=[
            pl.BlockSpec((tm, D_MODEL), lambda i, j: (i, 0)),
            pl.BlockSpec((1, D_MODEL), lambda i, j: (0, 0)),
            pl.BlockSpec((D_MODEL, tn), lambda i, j: (0, j)),
            pl.BlockSpec((D_MODEL, LANES), lambda i, j: (0, 0)),
        ],
        out_specs=[
            pl.BlockSpec((tm, tn), lambda i, j: (i, jnp.maximum(j - n_q_tiles, 0))),
            pl.BlockSpec((tm, tn), lambda i, j: (i, jnp.minimum(j, n_qkv_tiles - 1))),
            pl.BlockSpec((tm, LANES), lambda i, j: (i, 0)),
        ],
        out_shape=[
            jax.ShapeDtypeStruct((m, D_P32), F32),
            jax.ShapeDtypeStruct((m, D_QKV), BF16),
            jax.ShapeDtypeStruct((m, LANES), F32),
        ],
        scratch_shapes=[pltpu.VMEM((tm, D_MODEL), BF16)],
        compiler_params=pltpu.CompilerParams(
            dimension_semantics=("parallel", "arbitrary"), vmem_limit_bytes=VMEM_LIMIT),
        name="in_proj",
    )(x2d, norm_w, w_main, w_dt)


def _rel_bucket(rel):
    half = N_REL_BUCKETS // 2
    max_exact = half // 2
    ret = jnp.where(rel > 0, half, 0)
    n = jnp.abs(rel)
    nf = jnp.maximum(n, 1).astype(F32)
    large = max_exact + (jnp.log(nf / max_exact) / math.log(REL_MAX_DIST / max_exact)
                         * (half - max_exact)).astype(jnp.int32)
    large = jnp.minimum(large, half - 1)
    return ret + jnp.where(n < max_exact, n, large)


def _lambda_init(layer):
    return 0.8 - 0.6 * math.exp(-0.3 * layer)


def _lam_from_refs(lq1, lk1, lq2, lk2, lam0):
    return (jnp.exp(jnp.sum(lq1[...] * lk1[...], axis=-1, keepdims=True))
            - jnp.exp(jnp.sum(lq2[...] * lk2[...], axis=-1, keepdims=True)) + lam0)


def _attn_epilogue(a1, a2, lam, g, subln_w, lam0):
    o = a1 - lam * a2
    ms = jnp.mean(o * o, axis=-1, keepdims=True)
    o = o * lax.rsqrt(ms + NORM_EPS) * subln_w
    o = o * (1.0 - lam0)
    return o * _silu(g)


def _prompt_attn_kernel(q_ref, k_ref, vt_ref, g_ref, bias_ref, sw_ref, lq1, lk1, lq2, lk2,
                        o_ref, m_ref, l_ref, acc_ref, qt_ref, *, tile, qblk, lam0):
    qi = pl.program_id(2)
    d = ATT_HEAD_DIM
    m_ref[...] = jnp.full(m_ref.shape, MASK_VALUE, F32)
    l_ref[...] = jnp.zeros(l_ref.shape, F32)
    acc_ref[...] = jnp.zeros(acc_ref.shape, F32)
    units = [(c, mi) for c in range(tile // qblk) for mi in range(2)]
    qt_ref[...] = q_ref[0].astype(F32).T.astype(BF16)

    def kv_tiles(js, bias_idx):
        keys = [pl.ds(pl.multiple_of(j * tile, tile), tile) for j in js]
        ss = []
        for kk in keys:
            k = k_ref[0, kk, :]
            ss.append([jnp.dot(k[:, mi * d:(mi + 1) * d], qt_ref[mi * d:(mi + 1) * d, c * qblk:(c + 1) * qblk],
                               preferred_element_type=F32) for c, mi in units])
        for kk, s_units in zip(keys, ss):
            vt = vt_ref[0, :, kk]
            for (c, mi), s in zip(units, s_units):
                cols = slice(c * qblk, (c + 1) * qblk)
                if bias_idx is not None:
                    s = s + bias_ref[bias_idx, 0, :, cols]
                m_old = m_ref[mi, :, cols]
                m_new = jnp.maximum(m_old, jnp.max(s, axis=0, keepdims=True))
                alpha = jnp.exp2(m_old - m_new)
                p = jnp.exp2(s - m_new)
                l_ref[mi, :, cols] = alpha * l_ref[mi, :, cols] + jnp.sum(p, axis=0, keepdims=True)
                pv = jnp.dot(vt, p.astype(BF16), preferred_element_type=F32)
                acc_ref[mi, :, cols] = alpha * acc_ref[mi, :, cols] + pv
                m_ref[mi, :, cols] = m_new

    def far_pair_body(i, carry):
        kv_tiles((2 * i, 2 * i + 1), None)
        return carry

    def far_body(j, carry):
        kv_tiles((j,), None)
        return carry

    def near_body(j, carry):
        kv_tiles((j,), j - qi + 1)
        return carry

    n_far = jnp.maximum(qi - 1, 0)
    n_pairs = n_far // 2
    lax.fori_loop(0, n_pairs, far_pair_body, 0)
    lax.fori_loop(2 * n_pairs, n_far, far_body, 0)
    lax.fori_loop(n_far, qi + 1, near_body, 0)

    lam = _lam_from_refs(lq1, lk1, lq2, lk2, lam0)
    a1 = (acc_ref[0] / l_ref[0]).T
    a2 = (acc_ref[1] / l_ref[1]).T
    o_ref[0] = _attn_epilogue(a1, a2, lam, g_ref[0], sw_ref[...], lam0).astype(BF16)


def _toeplitz(fn, n):
    d = jnp.arange(2 * n, dtype=jnp.int32)
    g = jnp.moveaxis(fn(jnp.where(d < n, d, d - 2 * n)), 0, -1)
    x = jnp.tile(g, n)[..., :n * (2 * n - 1)].reshape(g.shape[:-1] + (n, 2 * n - 1))
    return x[..., :n]


def _prompt_bias_tiles(rel_table, tile):
    assert tile >= REL_MAX_DIST and tile % CHUNK == 0
    table = rel_table.astype(F32) * LOG2E
    far = table[_rel_bucket(jnp.full((1,), -(tile + 1), jnp.int32))]
    diag = _toeplitz(lambda dd: table[_rel_bucket(-dd)] - far, tile)
    off = _toeplitz(lambda dd: table[_rel_bucket(-dd - tile)] - far, tile)
    kpos = jnp.arange(tile, dtype=jnp.int32)[:, None]
    qpos = jnp.arange(tile, dtype=jnp.int32)[None, :]
    visible = (kpos // CHUNK) <= (qpos // CHUNK)
    diag = jnp.where(visible[None], diag, MASK_VALUE)
    return jnp.stack([off, diag])


def _prompt_attention(qkv, vt, p32, bias_tiles, subln_w, lq1, lk1, lq2, lk2, *, tile, lam0):
    b, s, _ = qkv.shape
    hw = 2 * ATT_HEAD_DIM
    nh = N_ATT_HEADS
    kern = functools.partial(_prompt_attn_kernel, tile=tile, qblk=min(tile, 256), lam0=lam0)
    vec = pl.BlockSpec((1, ATT_HEAD_DIM), lambda bi, h, qi: (0, 0))
    return pl.pallas_call(
        kern,
        grid=(b, nh, s // tile),
        in_specs=[
            pl.BlockSpec((1, tile, hw), lambda bi, h, qi: (bi, qi, h)),
            pl.BlockSpec((1, s, hw), lambda bi, h, qi: (bi, 0, nh + h)),
            pl.BlockSpec((1, hw, s), lambda bi, h, qi: (bi, h, 0)),
            pl.BlockSpec((1, tile, hw), lambda bi, h, qi: (bi, qi, COL_G // hw + h)),
            pl.BlockSpec((2, 1, tile, tile), lambda bi, h, qi: (0, h, 0, 0)),
            pl.BlockSpec((1, hw), lambda bi, h, qi: (0, 0)),
            vec, vec, vec, vec,
        ],
        out_specs=pl.BlockSpec((1, tile, hw), lambda bi, h, qi: (bi, qi, h)),
        out_shape=jax.ShapeDtypeStruct((b, s, D_ATT), BF16),
        scratch_shapes=[
            pltpu.VMEM((2, 1, tile), F32),
            pltpu.VMEM((2, 1, tile), F32),
            pltpu.VMEM((2, hw, tile), F32),
            pltpu.VMEM((hw, tile), BF16),
        ],
        compiler_params=pltpu.CompilerParams(
            dimension_semantics=("parallel", "parallel", "arbitrary"), vmem_limit_bytes=VMEM_LIMIT),
        name="prompt_attention",
    )(qkv, qkv, vt, p32, bias_tiles, subln_w, lq1, lk1, lq2, lk2)


def _sample_attn_kernel(q_ref, kn_ref, vn_ref, kp_ref, vp_ref, g_ref, bp_ref, bn_ref, sw_ref,
                        lq1, lk1, lq2, lk2, o_ref, *, lam0):
    d = ATT_HEAD_DIM
    q = q_ref[0]
    kn = kn_ref[0]
    vn = vn_ref[0]
    kp = kp_ref[0].astype(BF16)
    vp = vp_ref[0].astype(BF16)
    nt = (((1,), (1,)), ((), ()))
    outs = []
    for mi in range(2):
        qm = q[:, mi * d:(mi + 1) * d]
        sp = lax.dot_general(qm, kp[:, mi * d:(mi + 1) * d], nt, preferred_element_type=F32)
        sn = lax.dot_general(qm, kn[:, mi * d:(mi + 1) * d], nt, preferred_element_type=F32)
        sp = sp + bp_ref[0]
        sn = sn + bn_ref[0]
        m = jnp.maximum(jnp.max(sp, axis=-1, keepdims=True), jnp.max(sn, axis=-1, keepdims=True))
        pp = jnp.exp2(sp - m)
        pn = jnp.exp2(sn - m)
        l = jnp.sum(pp, axis=-1, keepdims=True) + jnp.sum(pn, axis=-1, keepdims=True)
        acc = (jnp.dot(pp.astype(BF16), vp, preferred_element_type=F32)
               + jnp.dot(pn.astype(BF16), vn, preferred_element_type=F32))
        outs.append(acc / l)
    lam = _lam_from_refs(lq1, lk1, lq2, lk2, lam0)
    o_ref[0] = _attn_epilogue(outs[0], outs[1], lam, g_ref[0], sw_ref[...], lam0).astype(BF16)


def _sample_bias(rel_table, past_len, s):
    qpos = past_len + jnp.arange(s, dtype=jnp.int32)[:, None]
    kpos = jnp.arange(past_len + s, dtype=jnp.int32)[None, :]
    bias = (rel_table.astype(F32) * LOG2E)[_rel_bucket(kpos - qpos)]
    visible = (kpos // CHUNK) <= (qpos // CHUNK)
    bias = jnp.where(visible[:, :, None], bias, MASK_VALUE)
    return jnp.transpose(bias, (2, 0, 1))


def _sample_attention(qkv, p32, k_past, v_past, bias, subln_w, lq1, lk1, lq2, lk2, *, lam0):
    b, s, _ = qkv.shape
    past = k_past.shape[1]
    hw = 2 * ATT_HEAD_DIM
    nh = N_ATT_HEADS
    bias_p = bias[:, :, :past]
    bias_n = bias[:, :, past:]
    kern = functools.partial(_sample_attn_kernel, lam0=lam0)
    vec = pl.BlockSpec((1, ATT_HEAD_DIM), lambda bi, h: (0, 0))
    return pl.pallas_call(
        kern,
        grid=(b, nh),
        in_specs=[
            pl.BlockSpec((1, s, hw), lambda bi, h: (bi, 0, h)),
            pl.BlockSpec((1, s, hw), lambda bi, h: (bi, 0, nh + h)),
            pl.BlockSpec((1, s, hw), lambda bi, h: (bi, 0, 2 * nh + h)),
            pl.BlockSpec((1, past, hw), lambda bi, h: (bi, 0, h)),
            pl.BlockSpec((1, past, hw), lambda bi, h: (bi, 0, h)),
            pl.BlockSpec((1, s, hw), lambda bi, h: (bi, 0, COL_G // hw + h)),
            pl.BlockSpec((1, s, past), lambda bi, h: (h, 0, 0)),
            pl.BlockSpec((1, s, s), lambda bi, h: (h, 0, 0)),
            pl.BlockSpec((1, hw), lambda bi, h: (0, 0)),
            vec, vec, vec, vec,
        ],
        out_specs=pl.BlockSpec((1, s, hw), lambda bi, h: (bi, 0, h)),
        out_shape=jax.ShapeDtypeStruct((b, s, D_ATT), BF16),
        compiler_params=pltpu.CompilerParams(
            dimension_semantics=("parallel", "parallel"), vmem_limit_bytes=VMEM_LIMIT),
        name="sample_attention",
    )(qkv, qkv, qkv, k_past, v_past, p32, bias_p, bias_n, subln_w, lq1, lk1, lq2, lk2)


def _ssd_kernel(z_ref, xs_ref, b_ref, c_ref, dtT_ref, hist_ref, h0_ref, cw_ref, cb_ref, dtb_ref,
                alog_ref, dskip_ref, nw_ref, y_ref, hout_ref, h_s, carry_s, xT_s, yT_s, *, valid):
    c = pl.program_id(1)
    L = xs_ref.shape[1]
    P, N, R = SSM_HEAD_DIM, SSM_STATE, HEADS_PER_GROUP

    @pl.when(c == 0)
    def _():
        h_s[...] = h0_ref[0]
        carry_s[...] = hist_ref[0]

    row8 = lax.broadcasted_iota(jnp.int32, (SUBLANES, 1), 0)

    def conv_silu(x, lo, hi):
        carry = carry_s[:, lo:hi]
        acc = x * cw_ref[CONV_WIDTH - 1:CONV_WIDTH, lo:hi] + cb_ref[:, lo:hi]
        for k in range(1, CONV_WIDTH):
            xk = pltpu.roll(x, k, axis=0)
            ck = pltpu.roll(carry, k, axis=0)
            head = jnp.where(row8 < k, ck, xk[:SUBLANES])
            xk = jnp.concatenate([head, xk[SUBLANES:]], axis=0)
            acc = acc + xk * cw_ref[CONV_WIDTH - 1 - k:CONV_WIDTH - k, lo:hi]
        carry_s[:, lo:hi] = x[L - SUBLANES:]
        return _silu(acc)

    xs = conv_silu(xs_ref[0], 0, D_SSM)
    bm = conv_silu(b_ref[0], D_SSM, D_SSM + D_BC)
    cm = conv_silu(c_ref[0], D_SSM + D_BC, D_CONV)

    dt = _softplus(dtT_ref[0] + dtb_ref[...])
    if valid < L:
        dt = jnp.where(lax.broadcasted_iota(jnp.int32, dt.shape, 1) < valid, dt, 0.0)
    a = dt * (-jnp.exp(alog_ref[...]))
    s_idx = lax.broadcasted_iota(jnp.int32, (L, L), 0)
    t_idx = lax.broadcasted_iota(jnp.int32, (L, L), 1)
    causal = s_idx <= t_idx
    upper = jnp.where(causal, 1.0, 0.0).astype(BF16)
    a1 = a.astype(BF16)
    r1 = a - a1.astype(F32)
    a2 = r1.astype(BF16)
    a3 = (r1 - a2.astype(F32)).astype(BF16)
    acs = (jnp.dot(a1, upper, preferred_element_type=F32)
           + jnp.dot(a2, upper, preferred_element_type=F32)
           + jnp.dot(a3, upper, preferred_element_type=F32))
    tot = acs[:, L - 1:L]
    e_row = jnp.exp(acs)
    w_row = dt * jnp.exp(tot - acs)
    d_row = jnp.broadcast_to(jnp.exp(tot), (N_SSM_HEADS, N))
    acs_col = jnp.concatenate([acs, jnp.zeros((L - N_SSM_HEADS, L), F32)], axis=0).T

    xT_s[...] = xs.T

    for g in range(N_SSM_GROUPS):
        bg = bm[:, g * N:(g + 1) * N].astype(BF16)
        cgT = cm[:, g * N:(g + 1) * N].T.astype(BF16)
        cbT = jnp.dot(bg, cgT, preferred_element_type=F32)
        for r8 in range(R):
            r = g * R + r8
            rows = slice(r * P, (r + 1) * P)
            seg = acs[r:r + 1, :] - acs_col[:, r:r + 1]
            decay = jnp.exp(jnp.where(causal, seg, MASK_VALUE))
            mT = (cbT * decay).astype(BF16)
            xr = xT_s[rows, :]
            hr = h_s[rows, :]
            y_intra = jnp.dot((xr * dt[r:r + 1, :]).astype(BF16), mT, preferred_element_type=F32)
            y_inter = jnp.dot(hr.astype(BF16), cgT, preferred_element_type=F32)
            yT_s[rows, :] = y_intra + y_inter * e_row[r:r + 1, :]
            upd = jnp.dot((xr * w_row[r:r + 1, :]).astype(BF16), bg, preferred_element_type=F32)
            h_s[rows, :] = hr * d_row[r:r + 1, :] + upd

    y = yT_s[...].T
    y = y + dskip_ref[...] * xs
    y = y * _silu(z_ref[0])
    gs = D_SSM // N_SSM_GROUPS
    outs = []
    for g in range(N_SSM_GROUPS):
        yg = y[:, g * gs:(g + 1) * gs]
        ms = jnp.mean(yg * yg, axis=-1, keepdims=True)
        outs.append(yg * lax.rsqrt(ms + NORM_EPS) * nw_ref[:, g * gs:(g + 1) * gs])
    y_ref[0] = jnp.concatenate(outs, axis=-1).astype(BF16)

    @pl.when(c == pl.num_programs(1) - 1)
    def _():
        hout_ref[0] = h_s[...]


def _ssd(src, cols, dtT, hist8, h0, conv_w, conv_b, dt_bias, a_log, d_full, norm_w, *, valid):
    b, s, _ = src.shape
    L = SSD_L
    col_z, col_xs, col_b, col_c = cols
    kern = functools.partial(_ssd_kernel, valid=valid)
    const2 = lambda shape: pl.BlockSpec(shape, lambda bi, c: (0, 0))
    hp = N_SSM_HEADS * SSM_HEAD_DIM
    return pl.pallas_call(
        kern,
        grid=(b, s // L),
        in_specs=[
            pl.BlockSpec((1, L, D_SSM), lambda bi, c: (bi, c, col_z // D_SSM)),
            pl.BlockSpec((1, L, D_SSM), lambda bi, c: (bi, c, col_xs // D_SSM)),
            pl.BlockSpec((1, L, D_BC), lambda bi, c: (bi, c, col_b // D_BC)),
            pl.BlockSpec((1, L, D_BC), lambda bi, c: (bi, c, col_c // D_BC)),
            pl.BlockSpec((1, N_SSM_HEADS, L), lambda bi, c: (bi, 0, c)),
            pl.BlockSpec((1, SUBLANES, D_CONV), lambda bi, c: (bi, 0, 0)),
            pl.BlockSpec((1, hp, SSM_STATE), lambda bi, c: (bi, 0, 0)),
            const2((CONV_WIDTH, D_CONV)),
            const2((1, D_CONV)),
            const2((N_SSM_HEADS, 1)),
            const2((N_SSM_HEADS, 1)),
            const2((1, D_SSM)),
            const2((1, D_SSM)),
        ],
        out_specs=[
            pl.BlockSpec((1, L, D_SSM), lambda bi, c: (bi, c, 0)),
            pl.BlockSpec((1, hp, SSM_STATE), lambda bi, c: (bi, 0, 0)),
        ],
        out_shape=[
            jax.ShapeDtypeStruct((b, s, D_SSM), BF16),
            jax.ShapeDtypeStruct((b, hp, SSM_STATE), F32),
        ],
        scratch_shapes=[
            pltpu.VMEM((hp, SSM_STATE), F32),
            pltpu.VMEM((SUBLANES, D_CONV), F32),
            pltpu.VMEM((D_SSM, L), F32),
            pltpu.VMEM((D_SSM, L), F32),
        ],
        compiler_params=pltpu.CompilerParams(
            dimension_semantics=("parallel", "arbitrary"), vmem_limit_bytes=VMEM_LIMIT),
        name="ssd",
    )(src, src, src, src, dtT, hist8, h0, conv_w, conv_b, dt_bias, a_log, d_full, norm_w)


def _outproj_kernel(a_ref, y_ref, x_ref, w1_ref, w2_ref, fw_ref, o_ref):
    acc = (jnp.dot(a_ref[...], w1_ref[...], preferred_element_type=F32)
           + jnp.dot(y_ref[...], w2_ref[...], preferred_element_type=F32))
    h = x_ref[...] + acc
    ms = jnp.mean(h * h, axis=-1, keepdims=True)
    o_ref[...] = h * lax.rsqrt(ms + NORM_EPS) * fw_ref[...]


def _out_proj(att, y, x2d, w1, w2, final_w, *, tm):
    m = x2d.shape[0]
    const = pl.BlockSpec((D_ATT, D_MODEL), lambda i: (0, 0), pipeline_mode=pl.Buffered(1))
    return pl.pallas_call(
        _outproj_kernel,
        grid=(m // tm,),
        in_specs=[
            pl.BlockSpec((tm, D_ATT), lambda i: (i, 0)),
            pl.BlockSpec((tm, D_SSM), lambda i: (i, 0)),
            pl.BlockSpec((tm, D_MODEL), lambda i: (i, 0)),
            const, const,
            pl.BlockSpec((1, D_MODEL), lambda i: (0, 0)),
        ],
        out_specs=pl.BlockSpec((tm, D_MODEL), lambda i: (i, 0)),
        out_shape=jax.ShapeDtypeStruct((m, D_MODEL), F32),
        compiler_params=pltpu.CompilerParams(
            dimension_semantics=("parallel",), vmem_limit_bytes=VMEM_LIMIT),
        name="out_proj",
    )(att, y, x2d, w1, w2, final_w)


def _tiles(m):
    tm = min(m, 1024)
    assert m % tm == 0
    return tm


def _layer(h, k_past, v_past, conv_past, ssm_past, layer, rel_bias, norm_w, w_main, w_dt, lq1, lk1, lq2, lk2,
           subln_w, conv_w, conv_b, dt_bias, a_log, d_full, ssm_norm_w, w_out1, w_out2, out_norm_w):
    b, s, _ = h.shape
    m = b * s
    lam0 = _lambda_init(layer)
    x2d = h.reshape(m, D_MODEL)
    p32, qkv, dt_raw = _in_proj(x2d, norm_w, w_main, w_dt, tm=_tiles(m), tn=512)
    k_new = p32[:, COL_K:COL_K + D_ATT].reshape(b, s, N_ATT_HEADS, 2, ATT_HEAD_DIM)
    v_new = p32[:, COL_V:COL_V + D_ATT].reshape(b, s, N_ATT_HEADS, 2 * ATT_HEAD_DIM)
    p32 = p32.reshape(b, s, D_P32)
    qkv = qkv.reshape(b, s, D_QKV)

    if k_past is None:
        tile = min(s, 512)
        vt = jnp.transpose(qkv[:, :, 2 * D_ATT:], (0, 2, 1))
        att = _prompt_attention(qkv, vt, p32, _prompt_bias_tiles(rel_bias, tile), subln_w, lq1, lk1, lq2, lk2,
                                tile=tile, lam0=lam0)
    else:
        past = k_past.shape[1]
        att = _sample_attention(qkv, p32, k_past.reshape(b, past, D_ATT), v_past.reshape(b, past, D_ATT),
                                _sample_bias(rel_bias, past, s), subln_w, lq1, lk1, lq2, lk2, lam0=lam0)

    conv_new = p32[:, s - (CONV_WIDTH - 1):, COL_XS:COL_XS + D_CONV]
    if CONV_WIDTH - 1 > s:
        conv_new = jnp.concatenate([conv_past, p32[:, :, COL_XS:]], axis=1)[:, -(CONV_WIDTH - 1):]
    hist8 = jnp.pad(conv_past, ((0, 0), (SUBLANES - (CONV_WIDTH - 1), 0), (0, 0)))
    dtT = jnp.transpose(dt_raw[:, :N_SSM_HEADS].reshape(b, s, N_SSM_HEADS), (0, 2, 1))
    h0 = ssm_past.reshape(b, N_SSM_HEADS * SSM_HEAD_DIM, SSM_STATE)
    if s % SSD_L == 0:
        src, cols = p32, (COL_Z, COL_XS, COL_B, COL_C)
    else:
        assert s < SSD_L
        pad = SSD_L - s
        src = jnp.pad(p32[:, :, COL_Z:], ((0, 0), (0, pad), (0, 0)))
        cols = (0, D_SSM, 2 * D_SSM, 2 * D_SSM + D_BC)
        dtT = jnp.pad(dtT, ((0, 0), (0, 0), (0, pad)))
    y, ssm_new = _ssd(src, cols, dtT, hist8, h0, conv_w, conv_b, dt_bias, a_log, d_full, ssm_norm_w,
                      valid=min(s, SSD_L))
    y = y[:, :s].reshape(m, D_SSM)
    ssm_new = ssm_new.reshape(b, N_SSM_HEADS, SSM_HEAD_DIM, SSM_STATE)

    out = _out_proj(att.reshape(m, D_ATT), y, x2d, w_out1, w_out2, out_norm_w, tm=min(m, 512))
    return out.reshape(b, s, D_MODEL), k_new, v_new, conv_new, ssm_new


def kernel(x_prompt, x_sample, cache_k, cache_v, cache_conv, state_ssm, rel_bias, norm_w, w_in, lambda_q1,
           lambda_k1, lambda_q2, lambda_k2, subln_w, conv_w, conv_b, dt_bias, A_log, D_skip, ssm_norm_w, w_out,
           final_norm_w):
    depth = w_in.shape[0]
    assert depth == 1, "the final norm is fused into the (single) layer's output projection"
    bp = x_prompt.shape[0]
    l = 0
    col_q = 0
    col_dt = 4 * D_ATT + D_SSM + D_CONV
    w = w_in[l]
    w_main = w[:, col_q:col_dt].astype(BF16)
    w_dt = jnp.pad(w[:, col_dt:], ((0, 0), (0, LANES - N_SSM_HEADS))).astype(BF16)
    row = lambda t: t.reshape(1, -1).astype(F32)
    col = lambda t: t.reshape(-1, 1).astype(F32)
    params = (rel_bias, row(norm_w[l]), w_main, w_dt, row(lambda_q1[l]), row(lambda_k1[l]), row(lambda_q2[l]),
              row(lambda_k2[l]), row(subln_w[l]), conv_w[l].astype(F32), row(conv_b[l]), col(dt_bias[l]),
              col(A_log[l]), row(jnp.repeat(D_skip[l], SSM_HEAD_DIM)), row(ssm_norm_w[l]),
              w_out[l, :D_ATT].astype(BF16), w_out[l, D_ATT:].astype(BF16), row(final_norm_w))
    conv0 = jnp.zeros((bp, CONV_WIDTH - 1, D_CONV), x_prompt.dtype)
    ssm0 = jnp.zeros((bp, N_SSM_HEADS, SSM_HEAD_DIM, SSM_STATE), state_ssm.dtype)
    yp, k1, v1, c1, s1 = _layer(x_prompt, None, None, conv0, ssm0, l, *params)
    ys, k2, v2, c2, s2 = _layer(x_sample, cache_k[l], cache_v[l], cache_conv[l], state_ssm[l], l, *params)
    return (yp, ys, k1[None], v1[None], c1[None], s1[None], k2[None], v2[None], c2[None], s2[None])
```

```python
import functools
import math

import jax
import jax.numpy as jnp
from jax import lax
from jax.experimental import pallas as pl
from jax.experimental.pallas import tpu as pltpu

F32 = jnp.float32
BF16 = jnp.bfloat16

D_MODEL = 2048
CHUNK = 64
NORM_EPS = 1e-5
N_ATT_HEADS = 8
ATT_HEAD_DIM = 128
D_ATT = N_ATT_HEADS * 2 * ATT_HEAD_DIM
N_REL_BUCKETS = 32
REL_MAX_DIST = 128
D_SSM = 2048
SSM_HEAD_DIM = 64
N_SSM_HEADS = D_SSM // SSM_HEAD_DIM
N_SSM_GROUPS = 4
HEADS_PER_GROUP = N_SSM_HEADS // N_SSM_GROUPS
SSM_STATE = 128
CONV_WIDTH = 4
D_BC = N_SSM_GROUPS * SSM_STATE
D_CONV = D_SSM + 2 * D_BC
D_MIX = D_ATT + D_SSM
D_QKV = 3 * D_ATT
D_P32 = D_ATT + D_SSM + D_CONV
COL_G, COL_Z, COL_XS = 0, D_ATT, D_ATT + D_SSM
COL_B, COL_C = COL_XS + D_SSM, COL_XS + D_SSM + D_BC

LANES = 128
SUBLANES = 8
VMEM_LIMIT = 56 * 1024 * 1024
MASK_VALUE = -1e30
LOG2E = math.log2(math.e)
Q_SCALE = ATT_HEAD_DIM ** -0.5 * LOG2E
SSD_L = 128


def _silu(x):
    return x * (1.0 / (1.0 + jnp.exp(-x)))


def _softplus(x):
    return jnp.maximum(x, 0.0) + jnp.log1p(jnp.exp(-jnp.abs(x)))


def _inproj_kernel(x_ref, nw_ref, w_ref, wdt_ref, p_ref, k_ref, v_ref, qkv_ref, dt_ref, u_ref, *, n_q_tiles,
                   slab):
    j = pl.program_id(1)
    tm = x_ref.shape[0]

    @pl.when(j == 0)
    def _():
        def body(r, carry):
            rows = pl.ds(pl.multiple_of(r * slab, slab), slab)
            x = x_ref[rows, :]
            ms = jnp.mean(x * x, axis=-1, keepdims=True)
            u = x * lax.rsqrt(ms + NORM_EPS) * nw_ref[...]
            u_ref[rows, :] = u.astype(BF16)
            return carry
        lax.fori_loop(0, tm // slab, body, 0)
        dt_ref[...] = jnp.dot(u_ref[...], wdt_ref[...], preferred_element_type=F32)

    res = jnp.dot(u_ref[...], w_ref[...], preferred_element_type=F32)

    @pl.when(j < n_q_tiles)
    def _():
        qkv_ref[...] = (res * Q_SCALE).astype(BF16)

    @pl.when(jnp.logical_and(j >= n_q_tiles, j < 2 * n_q_tiles))
    def _():
        k_ref[...] = res
        qkv_ref[...] = res.astype(BF16)

    @pl.when(jnp.logical_and(j >= 2 * n_q_tiles, j < 3 * n_q_tiles))
    def _():
        v_ref[...] = res
        qkv_ref[...] = res.astype(BF16)

    @pl.when(j >= 3 * n_q_tiles)
    def _():
        p_ref[...] = res


def _in_proj(x2d, norm_w, w_main, w_dt, *, tm, tn):
    m = x2d.shape[0]
    n_tiles = w_main.shape[1] // tn
    n_q_tiles = D_ATT // tn
    slab = min(tm, 256)
    kern = functools.partial(_inproj_kernel, n_q_tiles=n_q_tiles, slab=slab)
    clamp = lambda j, lo: jnp.clip(j - lo * n_q_tiles, 0, n_q_tiles - 1)
    return pl.pallas_call(
        kern,
        grid=(m // tm, n_tiles),
        in_specs=[
            pl.BlockSpec((tm, D_MODEL), lambda i, j: (i, 0)),
            pl.BlockSpec((1, D_MODEL), lambda i, j: (0, 0)),
            pl.BlockSpec((D_MODEL, tn), lambda i, j: (0, j)),
            pl.BlockSpec((D_MODEL, LANES), lambda i, j: (0, 0)),
        ],
        out_specs=[
            pl.BlockSpec((tm, tn), lambda i, j: (i, jnp.maximum(j - 3 * n_q_tiles, 0))),
            pl.BlockSpec((tm, tn), lambda i, j: (i, clamp(j, 1))),
            pl.BlockSpec((tm, tn), lambda i, j: (i, clamp(j, 2))),
            pl.BlockSpec((tm, tn), lambda i, j: (i, jnp.minimum(j, 3 * n_q_tiles - 1))),
            pl.BlockSpec((tm, LANES), lambda i, j: (i, 0)),
        ],
        out_shape=[
            jax.ShapeDtypeStruct((m, D_P32), F32),
            jax.ShapeDtypeStruct((m, D_ATT), F32),
            jax.ShapeDtypeStruct((m, D_ATT), F32),
            jax.ShapeDtypeStruct((m, D_QKV), BF16),
            jax.ShapeDtypeStruct((m, LANES), F32),
        ],
        scratch_shapes=[pltpu.VMEM((tm, D_MODEL), BF16)],
        compiler_params=pltpu.CompilerParams(
            dimension_semantics=("parallel", "arbitrary"), vmem_limit_bytes=VMEM_LIMIT),
        name="in_proj",
    )(x2d, norm_w, w_main, w_dt)


def _rel_bucket(rel):
    half = N_REL_BUCKETS // 2
    max_exact = half // 2
    ret = jnp.where(rel > 0, half, 0)
    n = jnp.abs(rel)
    nf = jnp.maximum(n, 1).astype(F32)
    large = max_exact + (jnp.log(nf / max_exact) / math.log(REL_MAX_DIST / max_exact)
                         * (half - max_exact)).astype(jnp.int32)
    large = jnp.minimum(large, half - 1)
    return ret + jnp.where(n < max_exact, n, large)


def _lambda_init(layer):
    return 0.8 - 0.6 * math.exp(-0.3 * layer)


def _lam_from_refs(lq1, lk1, lq2, lk2, lam0):
    return (jnp.exp(jnp.sum(lq1[...] * lk1[...], axis=-1, keepdims=True))
            - jnp.exp(jnp.sum(lq2[...] * lk2[...], axis=-1, keepdims=True)) + lam0)


def _attn_epilogue(a1, a2, lam, g, subln_w, lam0):
    o = a1 - lam * a2
    ms = jnp.mean(o * o, axis=-1, keepdims=True)
    o = o * lax.rsqrt(ms + NORM_EPS) * subln_w
    o = o * (1.0 - lam0)
    return o * _silu(g)


def _prompt_attn_kernel(q_ref, k_ref, vt_ref, g_ref, bias_ref, sw_ref, lq1, lk1, lq2, lk2,
                        o_ref, m_ref, l_ref, acc_ref, qt_ref, *, tile, qblk, lam0):
    qi = pl.program_id(2)
    d = ATT_HEAD_DIM
    m_ref[...] = jnp.full(m_ref.shape, MASK_VALUE, F32)
    l_ref[...] = jnp.zeros(l_ref.shape, F32)
    acc_ref[...] = jnp.zeros(acc_ref.shape, F32)
    units = [(c, mi) for c in range(tile // qblk) for mi in range(2)]
    qt_ref[...] = q_ref[0].astype(F32).T.astype(BF16)

    def kv_tiles(js, bias_idx):
        keys = [pl.ds(pl.multiple_of(j * tile, tile), tile) for j in js]
        ss = []
        for kk in keys:
            k = k_ref[0, kk, :]
            ss.append([jnp.dot(k[:, mi * d:(mi + 1) * d], qt_ref[mi * d:(mi + 1) * d, c * qblk:(c + 1) * qblk],
                               preferred_element_type=F32) for c, mi in units])
        for kk, s_units in zip(keys, ss):
            vt = vt_ref[0, :, kk]
            for (c, mi), s in zip(units, s_units):
                cols = slice(c * qblk, (c + 1) * qblk)
                if bias_idx is not None:
                    s = s + bias_ref[bias_idx, 0, :, cols]
                m_old = m_ref[mi, :, cols]
                m_new = jnp.maximum(m_old, jnp.max(s, axis=0, keepdims=True))
                alpha = jnp.exp2(m_old - m_new)
                p = jnp.exp2(s - m_new)
                l_ref[mi, :, cols] = alpha * l_ref[mi, :, cols] + jnp.sum(p, axis=0, keepdims=True)
                pv = jnp.dot(vt, p.astype(BF16), preferred_element_type=F32)
                acc_ref[mi, :, cols] = alpha * acc_ref[mi, :, cols] + pv
                m_ref[mi, :, cols] = m_new

    def far_quad_body(i, carry):
        kv_tiles((4 * i, 4 * i + 1, 4 * i + 2, 4 * i + 3), None)
        return carry

    def far_pair_body(i, carry):
        kv_tiles((2 * i, 2 * i + 1), None)
        return carry

    def far_body(j, carry):
        kv_tiles((j,), None)
        return carry

    def near_body(j, carry):
        kv_tiles((j,), j - qi + 1)
        return carry

    n_far = jnp.maximum(qi - 1, 0)
    n_quads = n_far // 4
    n_pairs = n_far // 2
    lax.fori_loop(0, n_quads, far_quad_body, 0)
    lax.fori_loop(2 * n_quads, n_pairs, far_pair_body, 0)
    lax.fori_loop(2 * n_pairs, n_far, far_body, 0)
    lax.fori_loop(n_far, qi + 1, near_body, 0)

    lam = _lam_from_refs(lq1, lk1, lq2, lk2, lam0)
    a1 = (acc_ref[0] / l_ref[0]).T
    a2 = (acc_ref[1] / l_ref[1]).T
    o_ref[0] = _attn_epilogue(a1, a2, lam, g_ref[0], sw_ref[...], lam0).astype(BF16)


def _toeplitz(fn, rows, cols):
    period = rows + cols
    d = jnp.arange(period, dtype=jnp.int32)
    g = jnp.moveaxis(fn(jnp.where(d < cols, d, d - period)), 0, -1)
    x = jnp.tile(g, rows)[..., :rows * (period - 1)].reshape(g.shape[:-1] + (rows, period - 1))
    return x[..., :cols]


def _prompt_bias_tiles(rel_table, tile):
    assert tile >= REL_MAX_DIST and tile % CHUNK == 0
    table = rel_table.astype(F32) * LOG2E
    far = table[_rel_bucket(jnp.full((1,), -(tile + 1), jnp.int32))]
    diag = _toeplitz(lambda dd: table[_rel_bucket(-dd)] - far, tile, tile)
    off = _toeplitz(lambda dd: table[_rel_bucket(-dd - tile)] - far, tile, tile)
    kpos = jnp.arange(tile, dtype=jnp.int32)[:, None]
    qpos = jnp.arange(tile, dtype=jnp.int32)[None, :]
    visible = (kpos // CHUNK) <= (qpos // CHUNK)
    diag = jnp.where(visible[None], diag, MASK_VALUE)
    return jnp.stack([off, diag])


def _prompt_attention(qkv, vt, p32, bias_tiles, subln_w, lq1, lk1, lq2, lk2, *, tile, lam0):
    b, s, _ = qkv.shape
    hw = 2 * ATT_HEAD_DIM
    nh = N_ATT_HEADS
    kern = functools.partial(_prompt_attn_kernel, tile=tile, qblk=min(tile, 256), lam0=lam0)
    vec = pl.BlockSpec((1, ATT_HEAD_DIM), lambda bi, h, qi: (0, 0))
    return pl.pallas_call(
        kern,
        grid=(b, nh, s // tile),
        in_specs=[
            pl.BlockSpec((1, tile, hw), lambda bi, h, qi: (bi, qi, h)),
            pl.BlockSpec((1, s, hw), lambda bi, h, qi: (bi, 0, nh + h)),
            pl.BlockSpec((1, hw, s), lambda bi, h, qi: (bi, h, 0)),
            pl.BlockSpec((1, tile, hw), lambda bi, h, qi: (bi, qi, COL_G // hw + h)),
            pl.BlockSpec((2, 1, tile, tile), lambda bi, h, qi: (0, h, 0, 0)),
            pl.BlockSpec((1, hw), lambda bi, h, qi: (0, 0)),
            vec, vec, vec, vec,
        ],
        out_specs=pl.BlockSpec((1, tile, hw), lambda bi, h, qi: (bi, qi, h)),
        out_shape=jax.ShapeDtypeStruct((b, s, D_ATT), BF16),
        scratch_shapes=[
            pltpu.VMEM((2, 1, tile), F32),
            pltpu.VMEM((2, 1, tile), F32),
            pltpu.VMEM((2, hw, tile), F32),
            pltpu.VMEM((hw, tile), BF16),
        ],
        compiler_params=pltpu.CompilerParams(
            dimension_semantics=("parallel", "parallel", "arbitrary"), vmem_limit_bytes=VMEM_LIMIT),
        name="prompt_attention",
    )(qkv, qkv, vt, p32, bias_tiles, subln_w, lq1, lk1, lq2, lk2)


def _sample_attn_kernel(q_ref, kn_ref, vn_ref, kp_ref, vp_ref, g_ref, bp_ref, bn_ref, sw_ref,
                        lq1, lk1, lq2, lk2, o_ref, *, lam0):
    d = ATT_HEAD_DIM
    q = q_ref[0]
    kn = kn_ref[0]
    vn = vn_ref[0]
    kp = kp_ref[0].astype(BF16)
    vp = vp_ref[0].astype(BF16)
    nt = (((1,), (1,)), ((), ()))
    outs = []
    for mi in range(2):
        qm = q[:, mi * d:(mi + 1) * d]
        sp = lax.dot_general(qm, kp[:, mi * d:(mi + 1) * d], nt, preferred_element_type=F32)
        sn = lax.dot_general(qm, kn[:, mi * d:(mi + 1) * d], nt, preferred_element_type=F32)
        sp = sp + bp_ref[0]
        sn = sn + bn_ref[0]
        m = jnp.maximum(jnp.max(sp, axis=-1, keepdims=True), jnp.max(sn, axis=-1, keepdims=True))
        pp = jnp.exp2(sp - m)
        pn = jnp.exp2(sn - m)
        l = jnp.sum(pp, axis=-1, keepdims=True) + jnp.sum(pn, axis=-1, keepdims=True)
        acc = (jnp.dot(pp.astype(BF16), vp, preferred_element_type=F32)
               + jnp.dot(pn.astype(BF16), vn, preferred_element_type=F32))
        outs.append(acc / l)
    lam = _lam_from_refs(lq1, lk1, lq2, lk2, lam0)
    o_ref[0] = _attn_epilogue(outs[0], outs[1], lam, g_ref[0], sw_ref[...], lam0).astype(BF16)


def _sample_bias(rel_table, past_len, s):
    qpos = past_len + jnp.arange(s, dtype=jnp.int32)[:, None]
    kpos = jnp.arange(past_len + s, dtype=jnp.int32)[None, :]
    table = rel_table.astype(F32) * LOG2E
    bias = _toeplitz(lambda dd: table[_rel_bucket(dd - past_len)], s, past_len + s)
    visible = (kpos // CHUNK) <= (qpos // CHUNK)
    return jnp.where(visible[None], bias, MASK_VALUE)


def _sample_attention(qkv, p32, k_past, v_past, bias, subln_w, lq1, lk1, lq2, lk2, *, lam0):
    b, s, _ = qkv.shape
    past = k_past.shape[1]
    hw = 2 * ATT_HEAD_DIM
    nh = N_ATT_HEADS
    bias_p = bias[:, :, :past]
    bias_n = bias[:, :, past:]
    kern = functools.partial(_sample_attn_kernel, lam0=lam0)
    vec = pl.BlockSpec((1, ATT_HEAD_DIM), lambda bi, h: (0, 0))
    return pl.pallas_call(
        kern,
        grid=(b, nh),
        in_specs=[
            pl.BlockSpec((1, s, hw), lambda bi, h: (bi, 0, h)),
            pl.BlockSpec((1, s, hw), lambda bi, h: (bi, 0, nh + h)),
            pl.BlockSpec((1, s, hw), lambda bi, h: (bi, 0, 2 * nh + h)),
            pl.BlockSpec((1, past, hw), lambda bi, h: (bi, 0, h)),
            pl.BlockSpec((1, past, hw), lambda bi, h: (bi, 0, h)),
            pl.BlockSpec((1, s, hw), lambda bi, h: (bi, 0, COL_G // hw + h)),
            pl.BlockSpec((1, s, past), lambda bi, h: (h, 0, 0)),
            pl.BlockSpec((1, s, s), lambda bi, h: (h, 0, 0)),
            pl.BlockSpec((1, hw), lambda bi, h: (0, 0)),
            vec, vec, vec, vec,
        ],
        out_specs=pl.BlockSpec((1, s, hw), lambda bi, h: (bi, 0, h)),
        out_shape=jax.ShapeDtypeStruct((b, s, D_ATT), BF16),
        compiler_params=pltpu.CompilerParams(
            dimension_semantics=("parallel", "parallel"), vmem_limit_bytes=VMEM_LIMIT),
        name="sample_attention",
    )(qkv, qkv, qkv, k_past, v_past, p32, bias_p, bias_n, subln_w, lq1, lk1, lq2, lk2)


def _ssd_kernel(z_ref, xs_ref, b_ref, c_ref, dtT_ref, hist_ref, h0_ref, cw_ref, cb_ref, dtb_ref,
                alog_ref, dskip_ref, nw_ref, y_ref, hout_ref, h_s, carry_s, xT_s, yT_s, *, valid):
    c = pl.program_id(1)
    L = xs_ref.shape[1]
    P, N, R = SSM_HEAD_DIM, SSM_STATE, HEADS_PER_GROUP

    @pl.when(c == 0)
    def _():
        h_s[...] = h0_ref[0]
        carry_s[...] = hist_ref[0]

    row8 = lax.broadcasted_iota(jnp.int32, (SUBLANES, 1), 0)

    def conv_silu(x, lo, hi):
        carry = carry_s[:, lo:hi]
        acc = x * cw_ref[CONV_WIDTH - 1:CONV_WIDTH, lo:hi] + cb_ref[:, lo:hi]
        for k in range(1, CONV_WIDTH):
            xk = pltpu.roll(x, k, axis=0)
            ck = pltpu.roll(carry, k, axis=0)
            head = jnp.where(row8 < k, ck, xk[:SUBLANES])
            xk = jnp.concatenate([head, xk[SUBLANES:]], axis=0)
            acc = acc + xk * cw_ref[CONV_WIDTH - 1 - k:CONV_WIDTH - k, lo:hi]
        carry_s[:, lo:hi] = x[L - SUBLANES:]
        return _silu(acc)

    xs = conv_silu(xs_ref[0], 0, D_SSM)
    bm = conv_silu(b_ref[0], D_SSM, D_SSM + D_BC)
    cm = conv_silu(c_ref[0], D_SSM + D_BC, D_CONV)

    dt = _softplus(dtT_ref[0] + dtb_ref[...])
    if valid < L:
        dt = jnp.where(lax.broadcasted_iota(jnp.int32, dt.shape, 1) < valid, dt, 0.0)
    a = dt * (-jnp.exp(alog_ref[...]))
    s_idx = lax.broadcasted_iota(jnp.int32, (L, L), 0)
    t_idx = lax.broadcasted_iota(jnp.int32, (L, L), 1)
    causal = s_idx <= t_idx
    upper = jnp.where(causal, 1.0, 0.0).astype(BF16)
    a1 = a.astype(BF16)
    r1 = a - a1.astype(F32)
    a2 = r1.astype(BF16)
    a3 = (r1 - a2.astype(F32)).astype(BF16)
    acs = (jnp.dot(a1, upper, preferred_element_type=F32)
           + jnp.dot(a2, upper, preferred_element_type=F32)
           + jnp.dot(a3, upper, preferred_element_type=F32))
    tot = acs[:, L - 1:L]
    e_row = jnp.exp(acs)
    w_row = dt * jnp.exp(tot - acs)
    d_row = jnp.broadcast_to(jnp.exp(tot), (N_SSM_HEADS, N))
    acs_col = jnp.concatenate([acs, jnp.zeros((L - N_SSM_HEADS, L), F32)], axis=0).T

    xT_s[...] = xs.T

    for g in range(N_SSM_GROUPS):
        bg = bm[:, g * N:(g + 1) * N].astype(BF16)
        cgT = cm[:, g * N:(g + 1) * N].T.astype(BF16)
        cbT = jnp.dot(bg, cgT, preferred_element_type=F32)
        for r8 in range(R):
            r = g * R + r8
            rows = slice(r * P, (r + 1) * P)
            seg = acs[r:r + 1, :] - acs_col[:, r:r + 1]
            decay = jnp.exp(jnp.where(causal, seg, MASK_VALUE))
            mT = (cbT * decay).astype(BF16)
            xr = xT_s[rows, :]
            hr = h_s[rows, :]
            y_intra = jnp.dot((xr * dt[r:r + 1, :]).astype(BF16), mT, preferred_element_type=F32)
            y_inter = jnp.dot(hr.astype(BF16), cgT, preferred_element_type=F32)
            yT_s[rows, :] = y_intra + y_inter * e_row[r:r + 1, :]
            upd = jnp.dot((xr * w_row[r:r + 1, :]).astype(BF16), bg, preferred_element_type=F32)
            h_s[rows, :] = hr * d_row[r:r + 1, :] + upd

    y = yT_s[...].T
    y = y + dskip_ref[...] * xs
    y = y * _silu(z_ref[0])
    gs = D_SSM // N_SSM_GROUPS
    outs = []
    for g in range(N_SSM_GROUPS):
        yg = y[:, g * gs:(g + 1) * gs]
        ms = jnp.mean(yg * yg, axis=-1, keepdims=True)
        outs.append(yg * lax.rsqrt(ms + NORM_EPS) * nw_ref[:, g * gs:(g + 1) * gs])
    y_ref[0] = jnp.concatenate(outs, axis=-1).astype(BF16)

    @pl.when(c == pl.num_programs(1) - 1)
    def _():
        hout_ref[0] = h_s[...]


def _ssd(src, cols, dtT, hist8, h0, conv_w, conv_b, dt_bias, a_log, d_full, norm_w, *, valid):
    b, s, _ = src.shape
    L = SSD_L
    col_z, col_xs, col_b, col_c = cols
    kern = functools.partial(_ssd_kernel, valid=valid)
    const2 = lambda shape: pl.BlockSpec(shape, lambda bi, c: (0, 0))
    hp = N_SSM_HEADS * SSM_HEAD_DIM
    return pl.pallas_call(
        kern,
        grid=(b, s // L),
        in_specs=[
            pl.BlockSpec((1, L, D_SSM), lambda bi, c: (bi, c, col_z // D_SSM)),
            pl.BlockSpec((1, L, D_SSM), lambda bi, c: (bi, c, col_xs // D_SSM)),
            pl.BlockSpec((1, L, D_BC), lambda bi, c: (bi, c, col_b // D_BC)),
            pl.BlockSpec((1, L, D_BC), lambda bi, c: (bi, c, col_c // D_BC)),
            pl.BlockSpec((1, N_SSM_HEADS, L), lambda bi, c: (bi, 0, c)),
            pl.BlockSpec((1, SUBLANES, D_CONV), lambda bi, c: (bi, 0, 0)),
            pl.BlockSpec((1, hp, SSM_STATE), lambda bi, c: (bi, 0, 0)),
            const2((CONV_WIDTH, D_CONV)),
            const2((1, D_CONV)),
            const2((N_SSM_HEADS, 1)),
            const2((N_SSM_HEADS, 1)),
            const2((1, D_SSM)),
            const2((1, D_SSM)),
        ],
        out_specs=[
            pl.BlockSpec((1, L, D_SSM), lambda bi, c: (bi, c, 0)),
            pl.BlockSpec((1, hp, SSM_STATE), lambda bi, c: (bi, 0, 0)),
        ],
        out_shape=[
            jax.ShapeDtypeStruct((b, s, D_SSM), BF16),
            jax.ShapeDtypeStruct((b, hp, SSM_STATE), F32),
        ],
        scratch_shapes=[
            pltpu.VMEM((hp, SSM_STATE), F32),
            pltpu.VMEM((SUBLANES, D_CONV), F32),
            pltpu.VMEM((D_SSM, L), F32),
            pltpu.VMEM((D_SSM, L), F32),
        ],
        compiler_params=pltpu.CompilerParams(
            dimension_semantics=("parallel", "arbitrary"), vmem_limit_bytes=VMEM_LIMIT),
        name="ssd",
    )(src, src, src, src, dtT, hist8, h0, conv_w, conv_b, dt_bias, a_log, d_full, norm_w)


def _outproj_kernel(a_ref, y_ref, x_ref, w1_ref, w2_ref, fw_ref, o_ref):
    acc = (jnp.dot(a_ref[...], w1_ref[...], preferred_element_type=F32)
           + jnp.dot(y_ref[...], w2_ref[...], preferred_element_type=F32))
    h = x_ref[...] + acc
    ms = jnp.mean(h * h, axis=-1, keepdims=True)
    o_ref[...] = h * lax.rsqrt(ms + NORM_EPS) * fw_ref[...]


def _out_proj(att, y, x2d, w1, w2, final_w, *, tm):
    m = x2d.shape[0]
    const = pl.BlockSpec((D_ATT, D_MODEL), lambda i: (0, 0), pipeline_mode=pl.Buffered(1))
    return pl.pallas_call(
        _outproj_kernel,
        grid=(m // tm,),
        in_specs=[
            pl.BlockSpec((tm, D_ATT), lambda i: (i, 0)),
            pl.BlockSpec((tm, D_SSM), lambda i: (i, 0)),
            pl.BlockSpec((tm, D_MODEL), lambda i: (i, 0)),
            const, const,
            pl.BlockSpec((1, D_MODEL), lambda i: (0, 0)),
        ],
        out_specs=pl.BlockSpec((tm, D_MODEL), lambda i: (i, 0)),
        out_shape=jax.ShapeDtypeStruct((m, D_MODEL), F32),
        compiler_params=pltpu.CompilerParams(
            dimension_semantics=("parallel",), vmem_limit_bytes=VMEM_LIMIT),
        name="out_proj",
    )(att, y, x2d, w1, w2, final_w)


def _tiles(m):
    tm = min(m, 1024)
    assert m % tm == 0
    return tm


def _layer(h, k_past, v_past, conv_past, ssm_past, layer, rel_bias, norm_w, w_main, w_dt, lq1, lk1, lq2, lk2,
           subln_w, conv_w, conv_b, dt_bias, a_log, d_full, ssm_norm_w, w_out1, w_out2, out_norm_w):
    b, s, _ = h.shape
    m = b * s
    lam0 = _lambda_init(layer)
    x2d = h.reshape(m, D_MODEL)
    p32, k_new, v_new, qkv, dt_raw = _in_proj(x2d, norm_w, w_main, w_dt, tm=_tiles(m), tn=512)
    k_new = k_new.reshape(b, s, N_ATT_HEADS, 2, ATT_HEAD_DIM)
    v_new = v_new.reshape(b, s, N_ATT_HEADS, 2 * ATT_HEAD_DIM)
    p32 = p32.reshape(b, s, D_P32)
    qkv = qkv.reshape(b, s, D_QKV)

    if k_past is None:
        tile = min(s, 512)
        vt = jnp.transpose(qkv[:, :, 2 * D_ATT:], (0, 2, 1))
        att = _prompt_attention(qkv, vt, p32, _prompt_bias_tiles(rel_bias, tile), subln_w, lq1, lk1, lq2, lk2,
                                tile=tile, lam0=lam0)
    else:
        past = k_past.shape[1]
        att = _sample_attention(qkv, p32, k_past.reshape(b, past, D_ATT), v_past.reshape(b, past, D_ATT),
                                _sample_bias(rel_bias, past, s), subln_w, lq1, lk1, lq2, lk2, lam0=lam0)

    assert s >= CONV_WIDTH - 1
    conv_new = p32[:, s - (CONV_WIDTH - 1):, COL_XS:COL_XS + D_CONV]
    hist8 = jnp.pad(conv_past, ((0, 0), (SUBLANES - (CONV_WIDTH - 1), 0), (0, 0)))
    dtT = jnp.transpose(dt_raw[:, :N_SSM_HEADS].reshape(b, s, N_SSM_HEADS), (0, 2, 1))
    h0 = ssm_past.reshape(b, N_SSM_HEADS * SSM_HEAD_DIM, SSM_STATE)
    if s % SSD_L == 0:
        src, cols = p32, (COL_Z, COL_XS, COL_B, COL_C)
    else:
        assert s < SSD_L
        pad = SSD_L - s
        src = jnp.pad(p32[:, :, COL_Z:], ((0, 0), (0, pad), (0, 0)))
        cols = (0, D_SSM, 2 * D_SSM, 2 * D_SSM + D_BC)
        dtT = jnp.pad(dtT, ((0, 0), (0, 0), (0, pad)))
    y, ssm_new = _ssd(src, cols, dtT, hist8, h0, conv_w, conv_b, dt_bias, a_log, d_full, ssm_norm_w,
                      valid=min(s, SSD_L))
    y = y[:, :s].reshape(m, D_SSM)
    ssm_new = ssm_new.reshape(b, N_SSM_HEADS, SSM_HEAD_DIM, SSM_STATE)

    out = _out_proj(att.reshape(m, D_ATT), y, x2d, w_out1, w_out2, out_norm_w, tm=min(m, 512))
    return out.reshape(b, s, D_MODEL), k_new, v_new, conv_new, ssm_new


def kernel(x_prompt, x_sample, cache_k, cache_v, cache_conv, state_ssm, rel_bias, norm_w, w_in, lambda_q1,
           lambda_k1, lambda_q2, lambda_k2, subln_w, conv_w, conv_b, dt_bias, A_log, D_skip, ssm_norm_w, w_out,
           final_norm_w):
    depth = w_in.shape[0]
    assert depth == 1, "the final norm is fused into the (single) layer's output projection"
    bp = x_prompt.shape[0]
    l = 0
    col_q = 0
    col_dt = 4 * D_ATT + D_SSM + D_CONV
    w = w_in[l]
    w_main = w[:, col_q:col_dt].astype(BF16)
    w_dt = jnp.pad(w[:, col_dt:], ((0, 0), (0, LANES - N_SSM_HEADS))).astype(BF16)
    row = lambda t: t.reshape(1, -1).astype(F32)
    col = lambda t: t.reshape(-1, 1).astype(F32)
    params = (rel_bias, row(norm_w[l]), w_main, w_dt, row(lambda_q1[l]), row(lambda_k1[l]), row(lambda_q2[l]),
              row(lambda_k2[l]), row(subln_w[l]), conv_w[l].astype(F32), row(conv_b[l]), col(dt_bias[l]),
              col(A_log[l]), row(jnp.repeat(D_skip[l], SSM_HEAD_DIM)), row(ssm_norm_w[l]),
              w_out[l, :D_ATT].astype(BF16), w_out[l, D_ATT:].astype(BF16), row(final_norm_w))
    conv0 = jnp.zeros((bp, CONV_WIDTH - 1, D_CONV), x_prompt.dtype)
    ssm0 = jnp.zeros((bp, N_SSM_HEADS, SSM_HEAD_DIM, SSM_STATE), state_ssm.dtype)
    yp, k1, v1, c1, s1 = _layer(x_prompt, None, None, conv0, ssm0, l, *params)
    ys, k2, v2, c2, s2 = _layer(x_sample, cache_k[l], cache_v[l], cache_conv[l], state_ssm[l], l, *params)
    return (yp, ys, k1[None], v1[None], c1[None], s1[None], k2[None], v2[None], c2[None], s2[None])
```

```python
import functools
import math

import jax
import jax.numpy as jnp
from jax import lax
from jax.experimental import pallas as pl
from jax.experimental.pallas import tpu as pltpu

F32 = jnp.float32
BF16 = jnp.bfloat16

D_MODEL = 2048
CHUNK = 64
NORM_EPS = 1e-5
N_ATT_HEADS = 8
ATT_HEAD_DIM = 128
D_ATT = N_ATT_HEADS * 2 * ATT_HEAD_DIM
N_REL_BUCKETS = 32
REL_MAX_DIST = 128
D_SSM = 2048
SSM_HEAD_DIM = 64
N_SSM_HEADS = D_SSM // SSM_HEAD_DIM
N_SSM_GROUPS = 4
HEADS_PER_GROUP = N_SSM_HEADS // N_SSM_GROUPS
SSM_STATE = 128
CONV_WIDTH = 4
D_BC = N_SSM_GROUPS * SSM_STATE
D_CONV = D_SSM + 2 * D_BC
D_MIX = D_ATT + D_SSM
D_QKV = 3 * D_ATT
D_P32 = D_ATT + D_SSM + D_CONV
COL_G, COL_Z, COL_XS = 0, D_ATT, D_ATT + D_SSM
COL_B, COL_C = COL_XS + D_SSM, COL_XS + D_SSM + D_BC

LANES = 128
SUBLANES = 8
VMEM_LIMIT = 56 * 1024 * 1024
MASK_VALUE = -1e30
LOG2E = math.log2(math.e)
Q_SCALE = ATT_HEAD_DIM ** -0.5 * LOG2E
SSD_L = 128


def _silu(x):
    return x * (1.0 / (1.0 + jnp.exp(-x)))


def _softplus(x):
    return jnp.maximum(x, 0.0) + jnp.log1p(jnp.exp(-jnp.abs(x)))


def _inproj_kernel(x_ref, nw_ref, w_ref, wdt_ref, p_ref, k_ref, v_ref, qkv_ref, dt_ref, u_ref, *, n_q_tiles,
                   slab):
    j = pl.program_id(1)
    tm = x_ref.shape[0]

    @pl.when(j == 0)
    def _():
        def body(r, carry):
            rows = pl.ds(pl.multiple_of(r * slab, slab), slab)
            x = x_ref[rows, :]
            ms = jnp.mean(x * x, axis=-1, keepdims=True)
            u = x * lax.rsqrt(ms + NORM_EPS) * nw_ref[...]
            u_ref[rows, :] = u.astype(BF16)
            return carry
        lax.fori_loop(0, tm // slab, body, 0)
        dt_ref[...] = jnp.dot(u_ref[...], wdt_ref[...], preferred_element_type=F32)

    def project():
        return jnp.dot(u_ref[...], w_ref[...], preferred_element_type=F32)

    @pl.when(j < n_q_tiles)
    def _():
        qkv_ref[...] = (project() * Q_SCALE).astype(BF16)

    @pl.when(jnp.logical_and(j >= n_q_tiles, j < 2 * n_q_tiles))
    def _():
        res = project()
        k_ref[...] = res
        qkv_ref[...] = res.astype(BF16)

    @pl.when(jnp.logical_and(j >= 2 * n_q_tiles, j < 3 * n_q_tiles))
    def _():
        res = project()
        v_ref[...] = res
        qkv_ref[...] = res.astype(BF16)

    @pl.when(j >= 3 * n_q_tiles)
    def _():
        p_ref[...] = project()


def _in_proj(x2d, norm_w, w_main, w_dt, *, tm, tn):
    m = x2d.shape[0]
    n_tiles = w_main.shape[1] // tn
    n_q_tiles = D_ATT // tn
    slab = min(tm, 256)
    kern = functools.partial(_inproj_kernel, n_q_tiles=n_q_tiles, slab=slab)
    clamp = lambda j, lo: jnp.clip(j - lo * n_q_tiles, 0, n_q_tiles - 1)
    return pl.pallas_call(
        kern,
        grid=(m // tm, n_tiles),
        in_specs=[
            pl.BlockSpec((tm, D_MODEL), lambda i, j: (i, 0)),
            pl.BlockSpec((1, D_MODEL), lambda i, j: (0, 0)),
            pl.BlockSpec((D_MODEL, tn), lambda i, j: (0, j)),
            pl.BlockSpec((D_MODEL, LANES), lambda i, j: (0, 0)),
        ],
        out_specs=[
            pl.BlockSpec((tm, tn), lambda i, j: (i, jnp.maximum(j - 3 * n_q_tiles, 0))),
            pl.BlockSpec((tm, tn), lambda i, j: (i, clamp(j, 1))),
            pl.BlockSpec((tm, tn), lambda i, j: (i, clamp(j, 2))),
            pl.BlockSpec((tm, tn), lambda i, j: (i, jnp.minimum(j, 3 * n_q_tiles - 1))),
            pl.BlockSpec((tm, LANES), lambda i, j: (i, 0)),
        ],
        out_shape=[
            jax.ShapeDtypeStruct((m, D_P32), F32),
            jax.ShapeDtypeStruct((m, D_ATT), F32),
            jax.ShapeDtypeStruct((m, D_ATT), F32),
            jax.ShapeDtypeStruct((m, D_QKV), BF16),
            jax.ShapeDtypeStruct((m, LANES), F32),
        ],
        scratch_shapes=[pltpu.VMEM((tm, D_MODEL), BF16)],
        compiler_params=pltpu.CompilerParams(
            dimension_semantics=("parallel", "arbitrary"), vmem_limit_bytes=VMEM_LIMIT),
        name="in_proj",
    )(x2d, norm_w, w_main, w_dt)


def _rel_bucket(rel):
    half = N_REL_BUCKETS // 2
    max_exact = half // 2
    ret = jnp.where(rel > 0, half, 0)
    n = jnp.abs(rel)
    nf = jnp.maximum(n, 1).astype(F32)
    large = max_exact + (jnp.log(nf / max_exact) / math.log(REL_MAX_DIST / max_exact)
                         * (half - max_exact)).astype(jnp.int32)
    large = jnp.minimum(large, half - 1)
    return ret + jnp.where(n < max_exact, n, large)


def _lambda_init(layer):
    return 0.8 - 0.6 * math.exp(-0.3 * layer)


def _lam_from_refs(lq1, lk1, lq2, lk2, lam0):
    return (jnp.exp(jnp.sum(lq1[...] * lk1[...], axis=-1, keepdims=True))
            - jnp.exp(jnp.sum(lq2[...] * lk2[...], axis=-1, keepdims=True)) + lam0)


def _attn_epilogue(a1, a2, lam, g, subln_w, lam0):
    o = a1 - lam * a2
    ms = jnp.mean(o * o, axis=-1, keepdims=True)
    o = o * lax.rsqrt(ms + NORM_EPS) * subln_w
    o = o * (1.0 - lam0)
    return o * _silu(g)


def _prompt_attn_kernel(q_ref, k_ref, vt_ref, g_ref, bias_ref, sw_ref, lq1, lk1, lq2, lk2,
                        o_ref, m_ref, l_ref, acc_ref, qt_ref, *, tile, qblk, lam0):
    qi = pl.program_id(2)
    d = ATT_HEAD_DIM
    m_ref[...] = jnp.full(m_ref.shape, MASK_VALUE, F32)
    l_ref[...] = jnp.zeros(l_ref.shape, F32)
    acc_ref[...] = jnp.zeros(acc_ref.shape, F32)
    units = [(c, mi) for c in range(tile // qblk) for mi in range(2)]
    qt_ref[...] = q_ref[0].astype(F32).T.astype(BF16)

    def kv_tiles(js, bias_idx):
        keys = [pl.ds(pl.multiple_of(j * tile, tile), tile) for j in js]
        ss = []
        for kk in keys:
            k = k_ref[0, kk, :]
            ss.append([jnp.dot(k[:, mi * d:(mi + 1) * d], qt_ref[mi * d:(mi + 1) * d, c * qblk:(c + 1) * qblk],
                               preferred_element_type=F32) for c, mi in units])
        for kk, s_units in zip(keys, ss):
            vt = vt_ref[0, :, kk]
            for (c, mi), s in zip(units, s_units):
                cols = slice(c * qblk, (c + 1) * qblk)
                if bias_idx is not None:
                    s = s + bias_ref[bias_idx, 0, :, cols]
                m_old = m_ref[mi, :, cols]
                m_new = jnp.maximum(m_old, jnp.max(s, axis=0, keepdims=True))
                alpha = jnp.exp2(m_old - m_new)
                p = jnp.exp2(s - m_new)
                l_ref[mi, :, cols] = alpha * l_ref[mi, :, cols] + jnp.sum(p, axis=0, keepdims=True)
                pv = jnp.dot(vt, p.astype(BF16), preferred_element_type=F32)
                acc_ref[mi, :, cols] = alpha * acc_ref[mi, :, cols] + pv
                m_ref[mi, :, cols] = m_new

    def far_quad_body(i, carry):
        kv_tiles((4 * i, 4 * i + 1, 4 * i + 2, 4 * i + 3), None)
        return carry

    def far_pair_body(i, carry):
        kv_tiles((2 * i, 2 * i + 1), None)
        return carry

    def far_body(j, carry):
        kv_tiles((j,), None)
        return carry

    def near_body(j, carry):
        kv_tiles((j,), j - qi + 1)
        return carry

    n_far = jnp.maximum(qi - 1, 0)
    n_quads = n_far // 4
    n_pairs = n_far // 2
    lax.fori_loop(0, n_quads, far_quad_body, 0)
    lax.fori_loop(2 * n_quads, n_pairs, far_pair_body, 0)
    lax.fori_loop(2 * n_pairs, n_far, far_body, 0)
    lax.fori_loop(n_far, qi + 1, near_body, 0)

    lam = _lam_from_refs(lq1, lk1, lq2, lk2, lam0)
    a1 = (acc_ref[0] / l_ref[0]).T
    a2 = (acc_ref[1] / l_ref[1]).T
    o_ref[0] = _attn_epilogue(a1, a2, lam, g_ref[0], sw_ref[...], lam0).astype(BF16)


def _toeplitz(fn, rows, cols):
    period = rows + cols
    d = jnp.arange(period, dtype=jnp.int32)
    g = jnp.moveaxis(fn(jnp.where(d < cols, d, d - period)), 0, -1)
    x = jnp.tile(g, rows)[..., :rows * (period - 1)].reshape(g.shape[:-1] + (rows, period - 1))
    return x[..., :cols]


def _prompt_bias_tiles(rel_table, tile):
    assert tile >= REL_MAX_DIST and tile % CHUNK == 0
    table = rel_table.astype(F32) * LOG2E
    far = table[_rel_bucket(jnp.full((1,), -(tile + 1), jnp.int32))]
    diag = _toeplitz(lambda dd: table[_rel_bucket(-dd)] - far, tile, tile)
    off = _toeplitz(lambda dd: table[_rel_bucket(-dd - tile)] - far, tile, tile)
    kpos = jnp.arange(tile, dtype=jnp.int32)[:, None]
    qpos = jnp.arange(tile, dtype=jnp.int32)[None, :]
    visible = (kpos // CHUNK) <= (qpos // CHUNK)
    diag = jnp.where(visible[None], diag, MASK_VALUE)
    return jnp.stack([off, diag])


def _prompt_attention(qkv, vt, p32, bias_tiles, subln_w, lq1, lk1, lq2, lk2, *, tile, lam0):
    b, s, _ = qkv.shape
    hw = 2 * ATT_HEAD_DIM
    nh = N_ATT_HEADS
    kern = functools.partial(_prompt_attn_kernel, tile=tile, qblk=min(tile, 256), lam0=lam0)
    vec = pl.BlockSpec((1, ATT_HEAD_DIM), lambda bi, h, qi: (0, 0))
    return pl.pallas_call(
        kern,
        grid=(b, nh, s // tile),
        in_specs=[
            pl.BlockSpec((1, tile, hw), lambda bi, h, qi: (bi, qi, h)),
            pl.BlockSpec((1, s, hw), lambda bi, h, qi: (bi, 0, nh + h)),
            pl.BlockSpec((1, hw, s), lambda bi, h, qi: (bi, h, 0)),
            pl.BlockSpec((1, tile, hw), lambda bi, h, qi: (bi, qi, COL_G // hw + h)),
            pl.BlockSpec((2, 1, tile, tile), lambda bi, h, qi: (0, h, 0, 0)),
            pl.BlockSpec((1, hw), lambda bi, h, qi: (0, 0)),
            vec, vec, vec, vec,
        ],
        out_specs=pl.BlockSpec((1, tile, hw), lambda bi, h, qi: (bi, qi, h)),
        out_shape=jax.ShapeDtypeStruct((b, s, D_ATT), BF16),
        scratch_shapes=[
            pltpu.VMEM((2, 1, tile), F32),
            pltpu.VMEM((2, 1, tile), F32),
            pltpu.VMEM((2, hw, tile), F32),
            pltpu.VMEM((hw, tile), BF16),
        ],
        compiler_params=pltpu.CompilerParams(
            dimension_semantics=("parallel", "parallel", "arbitrary"), vmem_limit_bytes=VMEM_LIMIT),
        name="prompt_attention",
    )(qkv, qkv, vt, p32, bias_tiles, subln_w, lq1, lk1, lq2, lk2)


def _sample_attn_kernel(q_ref, kn_ref, vn_ref, kp_ref, vp_ref, g_ref, bp_ref, bn_ref, sw_ref,
                        lq1, lk1, lq2, lk2, o_ref, *, lam0):
    d = ATT_HEAD_DIM
    q = q_ref[0]
    kn = kn_ref[0]
    vn = vn_ref[0]
    kp = kp_ref[0].astype(BF16)
    vp = vp_ref[0].astype(BF16)
    nt = (((1,), (1,)), ((), ()))
    outs = []
    for mi in range(2):
        qm = q[:, mi * d:(mi + 1) * d]
        sp = lax.dot_general(qm, kp[:, mi * d:(mi + 1) * d], nt, preferred_element_type=F32)
        sn = lax.dot_general(qm, kn[:, mi * d:(mi + 1) * d], nt, preferred_element_type=F32)
        sp = sp + bp_ref[0]
        sn = sn + bn_ref[0]
        m = jnp.maximum(jnp.max(sp, axis=-1, keepdims=True), jnp.max(sn, axis=-1, keepdims=True))
        pp = jnp.exp2(sp - m)
        pn = jnp.exp2(sn - m)
        l = jnp.sum(pp, axis=-1, keepdims=True) + jnp.sum(pn, axis=-1, keepdims=True)
        acc = (jnp.dot(pp.astype(BF16), vp, preferred_element_type=F32)
               + jnp.dot(pn.astype(BF16), vn, preferred_element_type=F32))
        outs.append(acc / l)
    lam = _lam_from_refs(lq1, lk1, lq2, lk2, lam0)
    o_ref[0] = _attn_epilogue(outs[0], outs[1], lam, g_ref[0], sw_ref[...], lam0).astype(BF16)


def _sample_bias(rel_table, past_len, s):
    qpos = past_len + jnp.arange(s, dtype=jnp.int32)[:, None]
    kpos = jnp.arange(past_len + s, dtype=jnp.int32)[None, :]
    table = rel_table.astype(F32) * LOG2E
    bias = _toeplitz(lambda dd: table[_rel_bucket(dd - past_len)], s, past_len + s)
    visible = (kpos // CHUNK) <= (qpos // CHUNK)
    return jnp.where(visible[None], bias, MASK_VALUE)


def _sample_attention(qkv, p32, k_past, v_past, bias, subln_w, lq1, lk1, lq2, lk2, *, lam0):
    b, s, _ = qkv.shape
    past = k_past.shape[1]
    hw = 2 * ATT_HEAD_DIM
    nh = N_ATT_HEADS
    bias_p = bias[:, :, :past]
    bias_n = bias[:, :, past:]
    kern = functools.partial(_sample_attn_kernel, lam0=lam0)
    vec = pl.BlockSpec((1, ATT_HEAD_DIM), lambda bi, h: (0, 0))
    return pl.pallas_call(
        kern,
        grid=(b, nh),
        in_specs=[
            pl.BlockSpec((1, s, hw), lambda bi, h: (bi, 0, h)),
            pl.BlockSpec((1, s, hw), lambda bi, h: (bi, 0, nh + h)),
            pl.BlockSpec((1, s, hw), lambda bi, h: (bi, 0, 2 * nh + h)),
            pl.BlockSpec((1, past, hw), lambda bi, h: (bi, 0, h)),
            pl.BlockSpec((1, past, hw), lambda bi, h: (bi, 0, h)),
            pl.BlockSpec((1, s, hw), lambda bi, h: (bi, 0, COL_G // hw + h)),
            pl.BlockSpec((1, s, past), lambda bi, h: (h, 0, 0)),
            pl.BlockSpec((1, s, s), lambda bi, h: (h, 0, 0)),
            pl.BlockSpec((1, hw), lambda bi, h: (0, 0)),
            vec, vec, vec, vec,
        ],
        out_specs=pl.BlockSpec((1, s, hw), lambda bi, h: (bi, 0, h)),
        out_shape=jax.ShapeDtypeStruct((b, s, D_ATT), BF16),
        compiler_params=pltpu.CompilerParams(
            dimension_semantics=("parallel", "parallel"), vmem_limit_bytes=VMEM_LIMIT),
        name="sample_attention",
    )(qkv, qkv, qkv, k_past, v_past, p32, bias_p, bias_n, subln_w, lq1, lk1, lq2, lk2)


def _ssd_kernel(z_ref, xs_ref, b_ref, c_ref, dtT_ref, hist_ref, h0_ref, cw_ref, cb_ref, dtb_ref,
                alog_ref, dskip_ref, nw_ref, y_ref, hout_ref, h_s, xpad_s, xT_s, yT_s, *, valid):
    c = pl.program_id(1)
    L = xs_ref.shape[1]
    P, N, R = SSM_HEAD_DIM, SSM_STATE, HEADS_PER_GROUP

    @pl.when(c == 0)
    def _():
        h_s[...] = h0_ref[0]
        xpad_s[:SUBLANES, :] = hist_ref[0]

    def conv_silu(x_ref, lo, hi):
        xpad_s[SUBLANES:, lo:hi] = x_ref[0]
        acc = cb_ref[:, lo:hi]
        for k in range(CONV_WIDTH):
            acc = acc + (xpad_s[SUBLANES - k:SUBLANES - k + L, lo:hi]
                         * cw_ref[CONV_WIDTH - 1 - k:CONV_WIDTH - k, lo:hi])
        return _silu(acc)

    xs = conv_silu(xs_ref, 0, D_SSM)
    bm = conv_silu(b_ref, D_SSM, D_SSM + D_BC)
    cm = conv_silu(c_ref, D_SSM + D_BC, D_CONV)
    xpad_s[:SUBLANES, :] = xpad_s[L:, :]

    dt = _softplus(dtT_ref[0] + dtb_ref[...])
    if valid < L:
        dt = jnp.where(lax.broadcasted_iota(jnp.int32, dt.shape, 1) < valid, dt, 0.0)
    a = dt * (-jnp.exp(alog_ref[...]))
    s_idx = lax.broadcasted_iota(jnp.int32, (L, L), 0)
    t_idx = lax.broadcasted_iota(jnp.int32, (L, L), 1)
    causal = s_idx <= t_idx
    upper = jnp.where(causal, 1.0, 0.0).astype(BF16)
    a1 = a.astype(BF16)
    r1 = a - a1.astype(F32)
    a2 = r1.astype(BF16)
    a3 = (r1 - a2.astype(F32)).astype(BF16)
    acs = (jnp.dot(a1, upper, preferred_element_type=F32)
           + jnp.dot(a2, upper, preferred_element_type=F32)
           + jnp.dot(a3, upper, preferred_element_type=F32))
    tot = acs[:, L - 1:L]
    e_row = jnp.exp(acs)
    w_row = dt * jnp.exp(tot - acs)
    d_row = jnp.broadcast_to(jnp.exp(tot), (N_SSM_HEADS, N))
    acs_col = jnp.concatenate([acs, jnp.zeros((L - N_SSM_HEADS, L), F32)], axis=0).T

    xT_s[...] = xs.T

    for g in range(N_SSM_GROUPS):
        bg = bm[:, g * N:(g + 1) * N].astype(BF16)
        cgT = cm[:, g * N:(g + 1) * N].T.astype(BF16)
        cbT = jnp.dot(bg, cgT, preferred_element_type=F32)
        for r8 in range(R):
            r = g * R + r8
            rows = slice(r * P, (r + 1) * P)
            seg = acs[r:r + 1, :] - acs_col[:, r:r + 1]
            decay = jnp.exp(jnp.where(causal, seg, MASK_VALUE))
            mT = (cbT * decay).astype(BF16)
            xr = xT_s[rows, :]
            hr = h_s[rows, :]
            y_intra = jnp.dot((xr * dt[r:r + 1, :]).astype(BF16), mT, preferred_element_type=F32)
            y_inter = jnp.dot(hr.astype(BF16), cgT, preferred_element_type=F32)
            yT_s[rows, :] = y_intra + y_inter * e_row[r:r + 1, :]
            upd = jnp.dot((xr * w_row[r:r + 1, :]).astype(BF16), bg, preferred_element_type=F32)
            h_s[rows, :] = hr * d_row[r:r + 1, :] + upd

    y = yT_s[...].T
    y = y + dskip_ref[...] * xs
    y = y * _silu(z_ref[0])
    gs = D_SSM // N_SSM_GROUPS
    outs = []
    for g in range(N_SSM_GROUPS):
        yg = y[:, g * gs:(g + 1) * gs]
        ms = jnp.mean(yg * yg, axis=-1, keepdims=True)
        outs.append(yg * lax.rsqrt(ms + NORM_EPS) * nw_ref[:, g * gs:(g + 1) * gs])
    y_ref[0] = jnp.concatenate(outs, axis=-1).astype(BF16)

    @pl.when(c == pl.num_programs(1) - 1)
    def _():
        hout_ref[0] = h_s[...]


def _ssd(src, cols, dtT, hist8, h0, conv_w, conv_b, dt_bias, a_log, d_full, norm_w, *, valid):
    b, s, _ = src.shape
    L = SSD_L
    col_z, col_xs, col_b, col_c = cols
    kern = functools.partial(_ssd_kernel, valid=valid)
    const2 = lambda shape: pl.BlockSpec(shape, lambda bi, c: (0, 0))
    hp = N_SSM_HEADS * SSM_HEAD_DIM
    return pl.pallas_call(
        kern,
        grid=(b, s // L),
        in_specs=[
            pl.BlockSpec((1, L, D_SSM), lambda bi, c: (bi, c, col_z // D_SSM)),
            pl.BlockSpec((1, L, D_SSM), lambda bi, c: (bi, c, col_xs // D_SSM)),
            pl.BlockSpec((1, L, D_BC), lambda bi, c: (bi, c, col_b // D_BC)),
            pl.BlockSpec((1, L, D_BC), lambda bi, c: (bi, c, col_c // D_BC)),
            pl.BlockSpec((1, N_SSM_HEADS, L), lambda bi, c: (bi, 0, c)),
            pl.BlockSpec((1, SUBLANES, D_CONV), lambda bi, c: (bi, 0, 0)),
            pl.BlockSpec((1, hp, SSM_STATE), lambda bi, c: (bi, 0, 0)),
            const2((CONV_WIDTH, D_CONV)),
            const2((1, D_CONV)),
            const2((N_SSM_HEADS, 1)),
            const2((N_SSM_HEADS, 1)),
            const2((1, D_SSM)),
            const2((1, D_SSM)),
        ],
        out_specs=[
            pl.BlockSpec((1, L, D_SSM), lambda bi, c: (bi, c, 0)),
            pl.BlockSpec((1, hp, SSM_STATE), lambda bi, c: (bi, 0, 0)),
        ],
        out_shape=[
            jax.ShapeDtypeStruct((b, s, D_SSM), BF16),
            jax.ShapeDtypeStruct((b, hp, SSM_STATE), F32),
        ],
        scratch_shapes=[
            pltpu.VMEM((hp, SSM_STATE), F32),
            pltpu.VMEM((SUBLANES + L, D_CONV), F32),
            pltpu.VMEM((D_SSM, L), F32),
            pltpu.VMEM((D_SSM, L), F32),
        ],
        compiler_params=pltpu.CompilerParams(
            dimension_semantics=("parallel", "arbitrary"), vmem_limit_bytes=VMEM_LIMIT),
        name="ssd",
    )(src, src, src, src, dtT, hist8, h0, conv_w, conv_b, dt_bias, a_log, d_full, norm_w)


def _outproj_kernel(a_ref, y_ref, x_ref, w1_ref, w2_ref, fw_ref, o_ref):
    acc = (jnp.dot(a_ref[...], w1_ref[...], preferred_element_type=F32)
           + jnp.dot(y_ref[...], w2_ref[...], preferred_element_type=F32))
    h = x_ref[...] + acc
    ms = jnp.mean(h * h, axis=-1, keepdims=True)
    o_ref[...] = h * lax.rsqrt(ms + NORM_EPS) * fw_ref[...]


def _out_proj(att, y, x2d, w1, w2, final_w, *, tm):
    m = x2d.shape[0]
    const = pl.BlockSpec((D_ATT, D_MODEL), lambda i: (0, 0), pipeline_mode=pl.Buffered(1))
    return pl.pallas_call(
        _outproj_kernel,
        grid=(m // tm,),
        in_specs=[
            pl.BlockSpec((tm, D_ATT), lambda i: (i, 0)),
            pl.BlockSpec((tm, D_SSM), lambda i: (i, 0)),
            pl.BlockSpec((tm, D_MODEL), lambda i: (i, 0)),
            const, const,
            pl.BlockSpec((1, D_MODEL), lambda i: (0, 0)),
        ],
        out_specs=pl.BlockSpec((tm, D_MODEL), lambda i: (i, 0)),
        out_shape=jax.ShapeDtypeStruct((m, D_MODEL), F32),
        compiler_params=pltpu.CompilerParams(
            dimension_semantics=("parallel",), vmem_limit_bytes=VMEM_LIMIT),
        name="out_proj",
    )(att, y, x2d, w1, w2, final_w)


def _tiles(m):
    tm = min(m, 1024)
    assert m % tm == 0
    return tm


def _layer(h, k_past, v_past, conv_past, ssm_past, layer, rel_bias, norm_w, w_main, w_dt, lq1, lk1, lq2, lk2,
           subln_w, conv_w, conv_b, dt_bias, a_log, d_full, ssm_norm_w, w_out1, w_out2, out_norm_w):
    b, s, _ = h.shape
    m = b * s
    lam0 = _lambda_init(layer)
    x2d = h.reshape(m, D_MODEL)
    p32, k_new, v_new, qkv, dt_raw = _in_proj(x2d, norm_w, w_main, w_dt, tm=_tiles(m), tn=512)
    k_new = k_new.reshape(b, s, N_ATT_HEADS, 2, ATT_HEAD_DIM)
    v_new = v_new.reshape(b, s, N_ATT_HEADS, 2 * ATT_HEAD_DIM)
    p32 = p32.reshape(b, s, D_P32)
    qkv = qkv.reshape(b, s, D_QKV)

    if k_past is None:
        tile = min(s, 512)
        vt = jnp.transpose(qkv[:, :, 2 * D_ATT:], (0, 2, 1))
        att = _prompt_attention(qkv, vt, p32, _prompt_bias_tiles(rel_bias, tile), subln_w, lq1, lk1, lq2, lk2,
                                tile=tile, lam0=lam0)
    else:
        past = k_past.shape[1]
        att = _sample_attention(qkv, p32, k_past.reshape(b, past, D_ATT), v_past.reshape(b, past, D_ATT),
                                _sample_bias(rel_bias, past, s), subln_w, lq1, lk1, lq2, lk2, lam0=lam0)

    assert s >= CONV_WIDTH - 1
    conv_new = p32[:, s - (CONV_WIDTH - 1):, COL_XS:COL_XS + D_CONV]
    hist8 = jnp.pad(conv_past, ((0, 0), (SUBLANES - (CONV_WIDTH - 1), 0), (0, 0)))
    dtT = jnp.transpose(dt_raw[:, :N_SSM_HEADS].reshape(b, s, N_SSM_HEADS), (0, 2, 1))
    h0 = ssm_past.reshape(b, N_SSM_HEADS * SSM_HEAD_DIM, SSM_STATE)
    if s % SSD_L == 0:
        src, cols = p32, (COL_Z, COL_XS, COL_B, COL_C)
    else:
        assert s < SSD_L
        pad = SSD_L - s
        src = jnp.pad(p32[:, :, COL_Z:], ((0, 0), (0, pad), (0, 0)))
        cols = (0, D_SSM, 2 * D_SSM, 2 * D_SSM + D_BC)
        dtT = jnp.pad(dtT, ((0, 0), (0, 0), (0, pad)))
    y, ssm_new = _ssd(src, cols, dtT, hist8, h0, conv_w, conv_b, dt_bias, a_log, d_full, ssm_norm_w,
                      valid=min(s, SSD_L))
    y = y[:, :s].reshape(m, D_SSM)
    ssm_new = ssm_new.reshape(b, N_SSM_HEADS, SSM_HEAD_DIM, SSM_STATE)

    out = _out_proj(att.reshape(m, D_ATT), y, x2d, w_out1, w_out2, out_norm_w, tm=min(m, 512))
    return out.reshape(b, s, D_MODEL), k_new, v_new, conv_new, ssm_new


def kernel(x_prompt, x_sample, cache_k, cache_v, cache_conv, state_ssm, rel_bias, norm_w, w_in, lambda_q1,
           lambda_k1, lambda_q2, lambda_k2, subln_w, conv_w, conv_b, dt_bias, A_log, D_skip, ssm_norm_w, w_out,
           final_norm_w):
    depth = w_in.shape[0]
    assert depth == 1, "the final norm is fused into the (single) layer's output projection"
    bp = x_prompt.shape[0]
    l = 0
    col_q = 0
    col_dt = 4 * D_ATT + D_SSM + D_CONV
    w = w_in[l]
    w_main = w[:, col_q:col_dt].astype(BF16)
    w_dt = jnp.pad(w[:, col_dt:], ((0, 0), (0, LANES - N_SSM_HEADS))).astype(BF16)
    row = lambda t: t.reshape(1, -1).astype(F32)
    col = lambda t: t.reshape(-1, 1).astype(F32)
    params = (rel_bias, row(norm_w[l]), w_main, w_dt, row(lambda_q1[l]), row(lambda_k1[l]), row(lambda_q2[l]),
              row(lambda_k2[l]), row(subln_w[l]), conv_w[l].astype(F32), row(conv_b[l]), col(dt_bias[l]),
              col(A_log[l]), row(jnp.repeat(D_skip[l], SSM_HEAD_DIM)), row(ssm_norm_w[l]),
              w_out[l, :D_ATT].astype(BF16), w_out[l, D_ATT:].astype(BF16), row(final_norm_w))
    conv0 = jnp.zeros((bp, CONV_WIDTH - 1, D_CONV), x_prompt.dtype)
    ssm0 = jnp.zeros((bp, N_SSM_HEADS, SSM_HEAD_DIM, SSM_STATE), state_ssm.dtype)
    yp, k1, v1, c1, s1 = _layer(x_prompt, None, None, conv0, ssm0, l, *params)
    ys, k2, v2, c2, s2 = _layer(x_sample, cache_k[l], cache_v[l], cache_conv[l], state_ssm[l], l, *params)
    return (yp, ys, k1[None], v1[None], c1[None], s1[None], k2[None], v2[None], c2[None], s2[None])
```

```python
import functools
import math

import jax
import jax.numpy as jnp
from jax import lax
from jax.experimental import pallas as pl
from jax.experimental.pallas import tpu as pltpu

F32 = jnp.float32
BF16 = jnp.bfloat16

D_MODEL = 2048
CHUNK = 64
NORM_EPS = 1e-5
N_ATT_HEADS = 8
ATT_HEAD_DIM = 128
D_ATT = N_ATT_HEADS * 2 * ATT_HEAD_DIM
N_REL_BUCKETS = 32
REL_MAX_DIST = 128
D_SSM = 2048
SSM_HEAD_DIM = 64
N_SSM_HEADS = D_SSM // SSM_HEAD_DIM
N_SSM_GROUPS = 4
HEADS_PER_GROUP = N_SSM_HEADS // N_SSM_GROUPS
SSM_STATE = 128
CONV_WIDTH = 4
D_BC = N_SSM_GROUPS * SSM_STATE
D_CONV = D_SSM + 2 * D_BC
D_MIX = D_ATT + D_SSM
D_QKV = 3 * D_ATT
D_P32 = D_ATT + D_SSM + D_CONV
COL_G, COL_Z, COL_XS = 0, D_ATT, D_ATT + D_SSM
COL_B, COL_C = COL_XS + D_SSM, COL_XS + D_SSM + D_BC

LANES = 128
SUBLANES = 8
VMEM_LIMIT = 56 * 1024 * 1024
MASK_VALUE = -1e30
LOG2E = math.log2(math.e)
Q_SCALE = ATT_HEAD_DIM ** -0.5 * LOG2E
SSD_L = 128


def _silu(x):
    return x * (1.0 / (1.0 + jnp.exp(-x)))


def _softplus(x):
    return jnp.maximum(x, 0.0) + jnp.log1p(jnp.exp(-jnp.abs(x)))


def _inproj_kernel(x_ref, nw_ref, w_ref, wdt_ref, p_ref, k_ref, v_ref, qkv_ref, dt_ref, u_ref, *, n_q_tiles,
                   slab):
    j = pl.program_id(1)
    tm = x_ref.shape[0]

    @pl.when(j == 0)
    def _():
        def body(r, carry):
            rows = pl.ds(pl.multiple_of(r * slab, slab), slab)
            x = x_ref[rows, :]
            ms = jnp.mean(x * x, axis=-1, keepdims=True)
            u = x * lax.rsqrt(ms + NORM_EPS) * nw_ref[...]
            u_ref[rows, :] = u.astype(BF16)
            return carry
        lax.fori_loop(0, tm // slab, body, 0)
        dt_ref[...] = jnp.dot(u_ref[...], wdt_ref[...], preferred_element_type=F32)

    def project():
        return jnp.dot(u_ref[...], w_ref[...], preferred_element_type=F32)

    @pl.when(j < n_q_tiles)
    def _():
        qkv_ref[...] = (project() * Q_SCALE).astype(BF16)

    @pl.when(jnp.logical_and(j >= n_q_tiles, j < 2 * n_q_tiles))
    def _():
        res = project()
        k_ref[...] = res
        qkv_ref[...] = res.astype(BF16)

    @pl.when(jnp.logical_and(j >= 2 * n_q_tiles, j < 3 * n_q_tiles))
    def _():
        res = project()
        v_ref[...] = res
        qkv_ref[...] = res.astype(BF16)

    @pl.when(j >= 3 * n_q_tiles)
    def _():
        p_ref[...] = project()


def _in_proj(x2d, norm_w, w_main, w_dt, *, tm, tn):
    m = x2d.shape[0]
    n_tiles = w_main.shape[1] // tn
    n_q_tiles = D_ATT // tn
    slab = min(tm, 256)
    kern = functools.partial(_inproj_kernel, n_q_tiles=n_q_tiles, slab=slab)
    clamp = lambda j, lo: jnp.clip(j - lo * n_q_tiles, 0, n_q_tiles - 1)
    return pl.pallas_call(
        kern,
        grid=(m // tm, n_tiles),
        in_specs=[
            pl.BlockSpec((tm, D_MODEL), lambda i, j: (i, 0)),
            pl.BlockSpec((1, D_MODEL), lambda i, j: (0, 0)),
            pl.BlockSpec((D_MODEL, tn), lambda i, j: (0, j)),
            pl.BlockSpec((D_MODEL, LANES), lambda i, j: (0, 0)),
        ],
        out_specs=[
            pl.BlockSpec((tm, tn), lambda i, j: (i, jnp.maximum(j - 3 * n_q_tiles, 0))),
            pl.BlockSpec((tm, tn), lambda i, j: (i, clamp(j, 1))),
            pl.BlockSpec((tm, tn), lambda i, j: (i, clamp(j, 2))),
            pl.BlockSpec((tm, tn), lambda i, j: (i, jnp.minimum(j, 3 * n_q_tiles - 1))),
            pl.BlockSpec((tm, LANES), lambda i, j: (i, 0)),
        ],
        out_shape=[
            jax.ShapeDtypeStruct((m, D_P32), F32),
            jax.ShapeDtypeStruct((m, D_ATT), F32),
            jax.ShapeDtypeStruct((m, D_ATT), F32),
            jax.ShapeDtypeStruct((m, D_QKV), BF16),
            jax.ShapeDtypeStruct((m, LANES), F32),
        ],
        scratch_shapes=[pltpu.VMEM((tm, D_MODEL), BF16)],
        compiler_params=pltpu.CompilerParams(
            dimension_semantics=("parallel", "arbitrary"), vmem_limit_bytes=VMEM_LIMIT),
        name="in_proj",
    )(x2d, norm_w, w_main, w_dt)


def _rel_bucket(rel):
    half = N_REL_BUCKETS // 2
    max_exact = half // 2
    ret = jnp.where(rel > 0, half, 0)
    n = jnp.abs(rel)
    nf = jnp.maximum(n, 1).astype(F32)
    large = max_exact + (jnp.log(nf / max_exact) / math.log(REL_MAX_DIST / max_exact)
                         * (half - max_exact)).astype(jnp.int32)
    large = jnp.minimum(large, half - 1)
    return ret + jnp.where(n < max_exact, n, large)


def _lambda_init(layer):
    return 0.8 - 0.6 * math.exp(-0.3 * layer)


def _lam_from_refs(lq1, lk1, lq2, lk2, lam0):
    return (jnp.exp(jnp.sum(lq1[...] * lk1[...], axis=-1, keepdims=True))
            - jnp.exp(jnp.sum(lq2[...] * lk2[...], axis=-1, keepdims=True)) + lam0)


def _attn_epilogue(a1, a2, lam, g, subln_w, lam0):
    o = a1 - lam * a2
    ms = jnp.mean(o * o, axis=-1, keepdims=True)
    o = o * lax.rsqrt(ms + NORM_EPS) * subln_w
    o = o * (1.0 - lam0)
    return o * _silu(g)


def _prompt_attn_kernel(q_ref, k_ref, vt_ref, g_ref, bias_ref, sw_ref, lq1, lk1, lq2, lk2,
                        o_ref, m_ref, l_ref, acc_ref, qt_ref, *, tile, qblk, lam0):
    qi = pl.program_id(2)
    d = ATT_HEAD_DIM
    m_ref[...] = jnp.full(m_ref.shape, MASK_VALUE, F32)
    l_ref[...] = jnp.zeros(l_ref.shape, F32)
    acc_ref[...] = jnp.zeros(acc_ref.shape, F32)
    units = [(c, mi) for c in range(tile // qblk) for mi in range(2)]
    qt_ref[...] = q_ref[0].astype(F32).T.astype(BF16)

    def kv_span(j, n, bias_idx):
        keys = pl.ds(pl.multiple_of(j * tile, tile), n * tile)
        k = k_ref[0, keys, :]
        vt = vt_ref[0, :, keys]
        ss = [jnp.dot(k[:, mi * d:(mi + 1) * d], qt_ref[mi * d:(mi + 1) * d, c * qblk:(c + 1) * qblk],
                      preferred_element_type=F32) for c, mi in units]
        for (c, mi), s in zip(units, ss):
            cols = slice(c * qblk, (c + 1) * qblk)
            if bias_idx is not None:
                s = s + bias_ref[bias_idx, 0, :, cols]
            m_old = m_ref[mi, :, cols]
            m_new = jnp.maximum(m_old, jnp.max(s, axis=0, keepdims=True))
            alpha = jnp.exp2(m_old - m_new)
            p = jnp.exp2(s - m_new)
            l_ref[mi, :, cols] = alpha * l_ref[mi, :, cols] + jnp.sum(p, axis=0, keepdims=True)
            pv = jnp.dot(vt, p.astype(BF16), preferred_element_type=F32)
            acc_ref[mi, :, cols] = alpha * acc_ref[mi, :, cols] + pv
            m_ref[mi, :, cols] = m_new

    def far_quad_body(i, carry):
        kv_span(4 * i, 4, None)
        return carry

    def far_pair_body(i, carry):
        kv_span(2 * i, 2, None)
        return carry

    def far_body(j, carry):
        kv_span(j, 1, None)
        return carry

    def near_body(j, carry):
        kv_span(j, 1, j - qi + 1)
        return carry

    n_far = jnp.maximum(qi - 1, 0)
    n_quads = n_far // 4
    n_pairs = n_far // 2
    lax.fori_loop(0, n_quads, far_quad_body, 0)
    lax.fori_loop(2 * n_quads, n_pairs, far_pair_body, 0)
    lax.fori_loop(2 * n_pairs, n_far, far_body, 0)
    lax.fori_loop(n_far, qi + 1, near_body, 0)

    lam = _lam_from_refs(lq1, lk1, lq2, lk2, lam0)
    a1 = (acc_ref[0] / l_ref[0]).T
    a2 = (acc_ref[1] / l_ref[1]).T
    o_ref[0] = _attn_epilogue(a1, a2, lam, g_ref[0], sw_ref[...], lam0).astype(BF16)


def _toeplitz(fn, rows, cols):
    period = rows + cols
    d = jnp.arange(period, dtype=jnp.int32)
    g = jnp.moveaxis(fn(jnp.where(d < cols, d, d - period)), 0, -1)
    x = jnp.tile(g, rows)[..., :rows * (period - 1)].reshape(g.shape[:-1] + (rows, period - 1))
    return x[..., :cols]


def _prompt_bias_tiles(rel_table, tile):
    assert tile >= REL_MAX_DIST and tile % CHUNK == 0
    table = rel_table.astype(F32) * LOG2E
    far = table[_rel_bucket(jnp.full((1,), -(tile + 1), jnp.int32))]
    diag = _toeplitz(lambda dd: table[_rel_bucket(-dd)] - far, tile, tile)
    off = _toeplitz(lambda dd: table[_rel_bucket(-dd - tile)] - far, tile, tile)
    kpos = jnp.arange(tile, dtype=jnp.int32)[:, None]
    qpos = jnp.arange(tile, dtype=jnp.int32)[None, :]
    visible = (kpos // CHUNK) <= (qpos // CHUNK)
    diag = jnp.where(visible[None], diag, MASK_VALUE)
    return jnp.stack([off, diag])


def _prompt_attention(qkv, vt, p32, bias_tiles, subln_w, lq1, lk1, lq2, lk2, *, tile, lam0):
    b, s, _ = qkv.shape
    hw = 2 * ATT_HEAD_DIM
    nh = N_ATT_HEADS
    kern = functools.partial(_prompt_attn_kernel, tile=tile, qblk=min(tile, 256), lam0=lam0)
    vec = pl.BlockSpec((1, ATT_HEAD_DIM), lambda bi, h, qi: (0, 0))
    return pl.pallas_call(
        kern,
        grid=(b, nh, s // tile),
        in_specs=[
            pl.BlockSpec((1, tile, hw), lambda bi, h, qi: (bi, qi, h)),
            pl.BlockSpec((1, s, hw), lambda bi, h, qi: (bi, 0, nh + h)),
            pl.BlockSpec((1, hw, s), lambda bi, h, qi: (bi, h, 0)),
            pl.BlockSpec((1, tile, hw), lambda bi, h, qi: (bi, qi, COL_G // hw + h)),
            pl.BlockSpec((2, 1, tile, tile), lambda bi, h, qi: (0, h, 0, 0)),
            pl.BlockSpec((1, hw), lambda bi, h, qi: (0, 0)),
            vec, vec, vec, vec,
        ],
        out_specs=pl.BlockSpec((1, tile, hw), lambda bi, h, qi: (bi, qi, h)),
        out_shape=jax.ShapeDtypeStruct((b, s, D_ATT), BF16),
        scratch_shapes=[
            pltpu.VMEM((2, 1, tile), F32),
            pltpu.VMEM((2, 1, tile), F32),
            pltpu.VMEM((2, hw, tile), F32),
            pltpu.VMEM((hw, tile), BF16),
        ],
        compiler_params=pltpu.CompilerParams(
            dimension_semantics=("parallel", "parallel", "arbitrary"), vmem_limit_bytes=VMEM_LIMIT),
        name="prompt_attention",
    )(qkv, qkv, vt, p32, bias_tiles, subln_w, lq1, lk1, lq2, lk2)


def _sample_attn_kernel(q_ref, kn_ref, vn_ref, kp_ref, vp_ref, g_ref, bp_ref, bn_ref, sw_ref,
                        lq1, lk1, lq2, lk2, o_ref, *, lam0):
    d = ATT_HEAD_DIM
    q = q_ref[0]
    kn = kn_ref[0]
    vn = vn_ref[0]
    kp = kp_ref[0].astype(BF16)
    vp = vp_ref[0].astype(BF16)
    nt = (((1,), (1,)), ((), ()))
    outs = []
    for mi in range(2):
        qm = q[:, mi * d:(mi + 1) * d]
        sp = lax.dot_general(qm, kp[:, mi * d:(mi + 1) * d], nt, preferred_element_type=F32)
        sn = lax.dot_general(qm, kn[:, mi * d:(mi + 1) * d], nt, preferred_element_type=F32)
        sp = sp + bp_ref[0]
        sn = sn + bn_ref[0]
        m = jnp.maximum(jnp.max(sp, axis=-1, keepdims=True), jnp.max(sn, axis=-1, keepdims=True))
        pp = jnp.exp2(sp - m)
        pn = jnp.exp2(sn - m)
        l = jnp.sum(pp, axis=-1, keepdims=True) + jnp.sum(pn, axis=-1, keepdims=True)
        acc = (jnp.dot(pp.astype(BF16), vp, preferred_element_type=F32)
               + jnp.dot(pn.astype(BF16), vn, preferred_element_type=F32))
        outs.append(acc / l)
    lam = _lam_from_refs(lq1, lk1, lq2, lk2, lam0)
    o_ref[0] = _attn_epilogue(outs[0], outs[1], lam, g_ref[0], sw_ref[...], lam0).astype(BF16)


def _sample_bias(rel_table, past_len, s):
    qpos = past_len + jnp.arange(s, dtype=jnp.int32)[:, None]
    kpos = jnp.arange(past_len + s, dtype=jnp.int32)[None, :]
    table = rel_table.astype(F32) * LOG2E
    bias = _toeplitz(lambda dd: table[_rel_bucket(dd - past_len)], s, past_len + s)
    visible = (kpos // CHUNK) <= (qpos // CHUNK)
    return jnp.where(visible[None], bias, MASK_VALUE)


def _sample_attention(qkv, p32, k_past, v_past, bias, subln_w, lq1, lk1, lq2, lk2, *, lam0):
    b, s, _ = qkv.shape
    past = k_past.shape[1]
    hw = 2 * ATT_HEAD_DIM
    nh = N_ATT_HEADS
    bias_p = bias[:, :, :past]
    bias_n = bias[:, :, past:]
    kern = functools.partial(_sample_attn_kernel, lam0=lam0)
    vec = pl.BlockSpec((1, ATT_HEAD_DIM), lambda bi, h: (0, 0))
    return pl.pallas_call(
        kern,
        grid=(b, nh),
        in_specs=[
            pl.BlockSpec((1, s, hw), lambda bi, h: (bi, 0, h)),
            pl.BlockSpec((1, s, hw), lambda bi, h: (bi, 0, nh + h)),
            pl.BlockSpec((1, s, hw), lambda bi, h: (bi, 0, 2 * nh + h)),
            pl.BlockSpec((1, past, hw), lambda bi, h: (bi, 0, h)),
            pl.BlockSpec((1, past, hw), lambda bi, h: (bi, 0, h)),
            pl.BlockSpec((1, s, hw), lambda bi, h: (bi, 0, COL_G // hw + h)),
            pl.BlockSpec((1, s, past), lambda bi, h: (h, 0, 0)),
            pl.BlockSpec((1, s, s), lambda bi, h: (h, 0, 0)),
            pl.BlockSpec((1, hw), lambda bi, h: (0, 0)),
            vec, vec, vec, vec,
        ],
        out_specs=pl.BlockSpec((1, s, hw), lambda bi, h: (bi, 0, h)),
        out_shape=jax.ShapeDtypeStruct((b, s, D_ATT), BF16),
        compiler_params=pltpu.CompilerParams(
            dimension_semantics=("parallel", "parallel"), vmem_limit_bytes=VMEM_LIMIT),
        name="sample_attention",
    )(qkv, qkv, qkv, k_past, v_past, p32, bias_p, bias_n, subln_w, lq1, lk1, lq2, lk2)


def _ssd_kernel(z_ref, xs_ref, b_ref, c_ref, dtT_ref, hist_ref, h0_ref, cw_ref, cb_ref, dtb_ref,
                alog_ref, dskip_ref, nw_ref, y_ref, hout_ref, h_s, xpad_s, xT_s, yT_s, *, valid):
    c = pl.program_id(1)
    L = xs_ref.shape[1]
    P, N, R = SSM_HEAD_DIM, SSM_STATE, HEADS_PER_GROUP

    @pl.when(c == 0)
    def _():
        h_s[...] = h0_ref[0]
        xpad_s[:SUBLANES, :] = hist_ref[0]

    def conv_silu(x_ref, lo, hi):
        xpad_s[SUBLANES:, lo:hi] = x_ref[0]
        acc = cb_ref[:, lo:hi]
        for k in range(CONV_WIDTH):
            acc = acc + (xpad_s[SUBLANES - k:SUBLANES - k + L, lo:hi]
                         * cw_ref[CONV_WIDTH - 1 - k:CONV_WIDTH - k, lo:hi])
        return _silu(acc)

    xs = conv_silu(xs_ref, 0, D_SSM)
    bm = conv_silu(b_ref, D_SSM, D_SSM + D_BC)
    cm = conv_silu(c_ref, D_SSM + D_BC, D_CONV)
    xpad_s[:SUBLANES, :] = xpad_s[L:, :]

    dt = _softplus(dtT_ref[0] + dtb_ref[...])
    if valid < L:
        dt = jnp.where(lax.broadcasted_iota(jnp.int32, dt.shape, 1) < valid, dt, 0.0)
    a = dt * (-jnp.exp(alog_ref[...]))
    s_idx = lax.broadcasted_iota(jnp.int32, (L, L), 0)
    t_idx = lax.broadcasted_iota(jnp.int32, (L, L), 1)
    causal = s_idx <= t_idx
    upper = jnp.where(causal, 1.0, 0.0).astype(BF16)
    a1 = a.astype(BF16)
    r1 = a - a1.astype(F32)
    a2 = r1.astype(BF16)
    a3 = (r1 - a2.astype(F32)).astype(BF16)
    acs = (jnp.dot(a1, upper, preferred_element_type=F32)
           + jnp.dot(a2, upper, preferred_element_type=F32)
           + jnp.dot(a3, upper, preferred_element_type=F32))
    tot = acs[:, L - 1:L]
    e_row = jnp.exp(acs)
    w_row = dt * jnp.exp(tot - acs)
    d_row = jnp.broadcast_to(jnp.exp(tot), (N_SSM_HEADS, N))
    acs_col = jnp.concatenate([acs, jnp.zeros((L - N_SSM_HEADS, L), F32)], axis=0).T

    xT_s[...] = xs.T

    for g in range(N_SSM_GROUPS):
        bg = bm[:, g * N:(g + 1) * N].astype(BF16)
        cgT = cm[:, g * N:(g + 1) * N].T.astype(BF16)
        cbT = jnp.dot(bg, cgT, preferred_element_type=F32)
        for r8 in range(R):
            r = g * R + r8
            rows = slice(r * P, (r + 1) * P)
            seg = acs[r:r + 1, :] - acs_col[:, r:r + 1]
            decay = jnp.exp(jnp.where(causal, seg, MASK_VALUE))
            mT = (cbT * decay).astype(BF16)
            xr = xT_s[rows, :]
            hr = h_s[rows, :]
            y_intra = jnp.dot((xr * dt[r:r + 1, :]).astype(BF16), mT, preferred_element_type=F32)
            y_inter = jnp.dot(hr.astype(BF16), cgT, preferred_element_type=F32)
            yT_s[rows, :] = y_intra + y_inter * e_row[r:r + 1, :]
            upd = jnp.dot((xr * w_row[r:r + 1, :]).astype(BF16), bg, preferred_element_type=F32)
            h_s[rows, :] = hr * d_row[r:r + 1, :] + upd

    y = yT_s[...].T
    y = y + dskip_ref[...] * xs
    y = y * _silu(z_ref[0])
    gs = D_SSM // N_SSM_GROUPS
    outs = []
    for g in range(N_SSM_GROUPS):
        yg = y[:, g * gs:(g + 1) * gs]
        ms = jnp.mean(yg * yg, axis=-1, keepdims=True)
        outs.append(yg * lax.rsqrt(ms + NORM_EPS) * nw_ref[:, g * gs:(g + 1) * gs])
    y_ref[0] = jnp.concatenate(outs, axis=-1).astype(BF16)

    @pl.when(c == pl.num_programs(1) - 1)
    def _():
        hout_ref[0] = h_s[...]


def _ssd(src, cols, dtT, hist8, h0, conv_w, conv_b, dt_bias, a_log, d_full, norm_w, *, valid):
    b, s, _ = src.shape
    L = SSD_L
    col_z, col_xs, col_b, col_c = cols
    kern = functools.partial(_ssd_kernel, valid=valid)
    const2 = lambda shape: pl.BlockSpec(shape, lambda bi, c: (0, 0))
    hp = N_SSM_HEADS * SSM_HEAD_DIM
    return pl.pallas_call(
        kern,
        grid=(b, s // L),
        in_specs=[
            pl.BlockSpec((1, L, D_SSM), lambda bi, c: (bi, c, col_z // D_SSM)),
            pl.BlockSpec((1, L, D_SSM), lambda bi, c: (bi, c, col_xs // D_SSM)),
            pl.BlockSpec((1, L, D_BC), lambda bi, c: (bi, c, col_b // D_BC)),
            pl.BlockSpec((1, L, D_BC), lambda bi, c: (bi, c, col_c // D_BC)),
            pl.BlockSpec((1, N_SSM_HEADS, L), lambda bi, c: (bi, 0, c)),
            pl.BlockSpec((1, SUBLANES, D_CONV), lambda bi, c: (bi, 0, 0)),
            pl.BlockSpec((1, hp, SSM_STATE), lambda bi, c: (bi, 0, 0)),
            const2((CONV_WIDTH, D_CONV)),
            const2((1, D_CONV)),
            const2((N_SSM_HEADS, 1)),
            const2((N_SSM_HEADS, 1)),
            const2((1, D_SSM)),
            const2((1, D_SSM)),
        ],
        out_specs=[
            pl.BlockSpec((1, L, D_SSM), lambda bi, c: (bi, c, 0)),
            pl.BlockSpec((1, hp, SSM_STATE), lambda bi, c: (bi, 0, 0)),
        ],
        out_shape=[
            jax.ShapeDtypeStruct((b, s, D_SSM), BF16),
            jax.ShapeDtypeStruct((b, hp, SSM_STATE), F32),
        ],
        scratch_shapes=[
            pltpu.VMEM((hp, SSM_STATE), F32),
            pltpu.VMEM((SUBLANES + L, D_CONV), F32),
            pltpu.VMEM((D_SSM, L), F32),
            pltpu.VMEM((D_SSM, L), F32),
        ],
        compiler_params=pltpu.CompilerParams(
            dimension_semantics=("parallel", "arbitrary"), vmem_limit_bytes=VMEM_LIMIT),
        name="ssd",
    )(src, src, src, src, dtT, hist8, h0, conv_w, conv_b, dt_bias, a_log, d_full, norm_w)


def _outproj_kernel(a_ref, y_ref, x_ref, w1_ref, w2_ref, fw_ref, o_ref):
    acc = (jnp.dot(a_ref[...], w1_ref[...], preferred_element_type=F32)
           + jnp.dot(y_ref[...], w2_ref[...], preferred_element_type=F32))
    h = x_ref[...] + acc
    ms = jnp.mean(h * h, axis=-1, keepdims=True)
    o_ref[...] = h * lax.rsqrt(ms + NORM_EPS) * fw_ref[...]


def _out_proj(att, y, x2d, w1, w2, final_w, *, tm):
    m = x2d.shape[0]
    const = pl.BlockSpec((D_ATT, D_MODEL), lambda i: (0, 0), pipeline_mode=pl.Buffered(1))
    return pl.pallas_call(
        _outproj_kernel,
        grid=(m // tm,),
        in_specs=[
            pl.BlockSpec((tm, D_ATT), lambda i: (i, 0)),
            pl.BlockSpec((tm, D_SSM), lambda i: (i, 0)),
            pl.BlockSpec((tm, D_MODEL), lambda i: (i, 0)),
            const, const,
            pl.BlockSpec((1, D_MODEL), lambda i: (0, 0)),
        ],
        out_specs=pl.BlockSpec((tm, D_MODEL), lambda i: (i, 0)),
        out_shape=jax.ShapeDtypeStruct((m, D_MODEL), F32),
        compiler_params=pltpu.CompilerParams(
            dimension_semantics=("parallel",), vmem_limit_bytes=VMEM_LIMIT),
        name="out_proj",
    )(att, y, x2d, w1, w2, final_w)


def _tiles(m):
    tm = min(m, 1024)
    assert m % tm == 0
    return tm


def _layer(h, k_past, v_past, conv_past, ssm_past, layer, rel_bias, norm_w, w_main, w_dt, lq1, lk1, lq2, lk2,
           subln_w, conv_w, conv_b, dt_bias, a_log, d_full, ssm_norm_w, w_out1, w_out2, out_norm_w):
    b, s, _ = h.shape
    m = b * s
    lam0 = _lambda_init(layer)
    x2d = h.reshape(m, D_MODEL)
    p32, k_new, v_new, qkv, dt_raw = _in_proj(x2d, norm_w, w_main, w_dt, tm=_tiles(m), tn=512)
    k_new = k_new.reshape(b, s, N_ATT_HEADS, 2, ATT_HEAD_DIM)
    v_new = v_new.reshape(b, s, N_ATT_HEADS, 2 * ATT_HEAD_DIM)
    p32 = p32.reshape(b, s, D_P32)
    qkv = qkv.reshape(b, s, D_QKV)

    if k_past is None:
        tile = min(s, 512)
        vt = jnp.transpose(qkv[:, :, 2 * D_ATT:], (0, 2, 1))
        att = _prompt_attention(qkv, vt, p32, _prompt_bias_tiles(rel_bias, tile), subln_w, lq1, lk1, lq2, lk2,
                                tile=tile, lam0=lam0)
    else:
        past = k_past.shape[1]
        att = _sample_attention(qkv, p32, k_past.reshape(b, past, D_ATT), v_past.reshape(b, past, D_ATT),
                                _sample_bias(rel_bias, past, s), subln_w, lq1, lk1, lq2, lk2, lam0=lam0)

    assert s >= CONV_WIDTH - 1
    conv_new = p32[:, s - (CONV_WIDTH - 1):, COL_XS:COL_XS + D_CONV]
    hist8 = jnp.pad(conv_past, ((0, 0), (SUBLANES - (CONV_WIDTH - 1), 0), (0, 0)))
    dtT = jnp.transpose(dt_raw[:, :N_SSM_HEADS].reshape(b, s, N_SSM_HEADS), (0, 2, 1))
    h0 = ssm_past.reshape(b, N_SSM_HEADS * SSM_HEAD_DIM, SSM_STATE)
    if s % SSD_L == 0:
        src, cols = p32, (COL_Z, COL_XS, COL_B, COL_C)
    else:
        assert s < SSD_L
        pad = SSD_L - s
        src = jnp.pad(p32[:, :, COL_Z:], ((0, 0), (0, pad), (0, 0)))
        cols = (0, D_SSM, 2 * D_SSM, 2 * D_SSM + D_BC)
        dtT = jnp.pad(dtT, ((0, 0), (0, 0), (0, pad)))
    y, ssm_new = _ssd(src, cols, dtT, hist8, h0, conv_w, conv_b, dt_bias, a_log, d_full, ssm_norm_w,
                      valid=min(s, SSD_L))
    y = y[:, :s].reshape(m, D_SSM)
    ssm_new = ssm_new.reshape(b, N_SSM_HEADS, SSM_HEAD_DIM, SSM_STATE)

    out = _out_proj(att.reshape(m, D_ATT), y, x2d, w_out1, w_out2, out_norm_w, tm=min(m, 512))
    return out.reshape(b, s, D_MODEL), k_new, v_new, conv_new, ssm_new


def kernel(x_prompt, x_sample, cache_k, cache_v, cache_conv, state_ssm, rel_bias, norm_w, w_in, lambda_q1,
           lambda_k1, lambda_q2, lambda_k2, subln_w, conv_w, conv_b, dt_bias, A_log, D_skip, ssm_norm_w, w_out,
           final_norm_w):
    depth = w_in.shape[0]
    assert depth == 1, "the final norm is fused into the (single) layer's output projection"
    bp = x_prompt.shape[0]
    l = 0
    col_q = 0
    col_dt = 4 * D_ATT + D_SSM + D_CONV
    w = w_in[l]
    w_main = w[:, col_q:col_dt].astype(BF16)
    w_dt = jnp.pad(w[:, col_dt:], ((0, 0), (0, LANES - N_SSM_HEADS))).astype(BF16)
    row = lambda t: t.reshape(1, -1).astype(F32)
    col = lambda t: t.reshape(-1, 1).astype(F32)
    params = (rel_bias, row(norm_w[l]), w_main, w_dt, row(lambda_q1[l]), row(lambda_k1[l]), row(lambda_q2[l]),
              row(lambda_k2[l]), row(subln_w[l]), conv_w[l].astype(F32), row(conv_b[l]), col(dt_bias[l]),
              col(A_log[l]), row(jnp.repeat(D_skip[l], SSM_HEAD_DIM)), row(ssm_norm_w[l]),
              w_out[l, :D_ATT].astype(BF16), w_out[l, D_ATT:].astype(BF16), row(final_norm_w))
    conv0 = jnp.zeros((bp, CONV_WIDTH - 1, D_CONV), x_prompt.dtype)
    ssm0 = jnp.zeros((bp, N_SSM_HEADS, SSM_HEAD_DIM, SSM_STATE), state_ssm.dtype)
    yp, k1, v1, c1, s1 = _layer(x_prompt, None, None, conv0, ssm0, l, *params)
    ys, k2, v2, c2, s2 = _layer(x_sample, cache_k[l], cache_v[l], cache_conv[l], state_ssm[l], l, *params)
    return (yp, ys, k1[None], v1[None], c1[None], s1[None], k2[None], v2[None], c2[None], s2[None])
```

```python
import functools
import math

import jax
import jax.numpy as jnp
from jax import lax
from jax.experimental import pallas as pl
from jax.experimental.pallas import tpu as pltpu

F32 = jnp.float32
BF16 = jnp.bfloat16

D_MODEL = 2048
CHUNK = 64
NORM_EPS = 1e-5
N_ATT_HEADS = 8
ATT_HEAD_DIM = 128
D_ATT = N_ATT_HEADS * 2 * ATT_HEAD_DIM
N_REL_BUCKETS = 32
REL_MAX_DIST = 128
D_SSM = 2048
SSM_HEAD_DIM = 64
N_SSM_HEADS = D_SSM // SSM_HEAD_DIM
N_SSM_GROUPS = 4
HEADS_PER_GROUP = N_SSM_HEADS // N_SSM_GROUPS
SSM_STATE = 128
CONV_WIDTH = 4
D_BC = N_SSM_GROUPS * SSM_STATE
D_CONV = D_SSM + 2 * D_BC
D_MIX = D_ATT + D_SSM
D_QKV = 3 * D_ATT
D_P32 = D_ATT + D_SSM + D_CONV
COL_G, COL_Z, COL_XS = 0, D_ATT, D_ATT + D_SSM
COL_B, COL_C = COL_XS + D_SSM, COL_XS + D_SSM + D_BC

LANES = 128
SUBLANES = 8
VMEM_LIMIT = 56 * 1024 * 1024
MASK_VALUE = -1e30
LOG2E = math.log2(math.e)
Q_SCALE = ATT_HEAD_DIM ** -0.5 * LOG2E
SSD_L = 128


def _silu(x):
    return x * (1.0 / (1.0 + jnp.exp(-x)))


def _softplus(x):
    return jnp.maximum(x, 0.0) + jnp.log1p(jnp.exp(-jnp.abs(x)))


def _inproj_kernel(x_ref, nw_ref, w_ref, wdt_ref, p_ref, k_ref, v_ref, qkv_ref, dt_ref, u_ref, *, n_q_tiles,
                   slab):
    j = pl.program_id(1)
    tm = x_ref.shape[0]

    @pl.when(j == 0)
    def _():
        def body(r, carry):
            rows = pl.ds(pl.multiple_of(r * slab, slab), slab)
            x = x_ref[rows, :]
            ms = jnp.mean(x * x, axis=-1, keepdims=True)
            u = x * lax.rsqrt(ms + NORM_EPS) * nw_ref[...]
            u_ref[rows, :] = u.astype(BF16)
            return carry
        lax.fori_loop(0, tm // slab, body, 0)
        dt_ref[...] = jnp.dot(u_ref[...], wdt_ref[...], preferred_element_type=F32)

    def project():
        return jnp.dot(u_ref[...], w_ref[...], preferred_element_type=F32)

    @pl.when(j < n_q_tiles)
    def _():
        qkv_ref[...] = (project() * Q_SCALE).astype(BF16)

    @pl.when(jnp.logical_and(j >= n_q_tiles, j < 2 * n_q_tiles))
    def _():
        res = project()
        k_ref[...] = res
        qkv_ref[...] = res.astype(BF16)

    @pl.when(jnp.logical_and(j >= 2 * n_q_tiles, j < 3 * n_q_tiles))
    def _():
        res = project()
        v_ref[...] = res
        qkv_ref[...] = res.astype(BF16)

    @pl.when(j >= 3 * n_q_tiles)
    def _():
        p_ref[...] = project()


def _in_proj(x2d, norm_w, w_main, w_dt, *, tm, tn):
    m = x2d.shape[0]
    n_tiles = w_main.shape[1] // tn
    n_q_tiles = D_ATT // tn
    slab = min(tm, 256)
    kern = functools.partial(_inproj_kernel, n_q_tiles=n_q_tiles, slab=slab)
    clamp = lambda j, lo: jnp.clip(j - lo * n_q_tiles, 0, n_q_tiles - 1)
    return pl.pallas_call(
        kern,
        grid=(m // tm, n_tiles),
        in_specs=[
            pl.BlockSpec((tm, D_MODEL), lambda i, j: (i, 0)),
            pl.BlockSpec((1, D_MODEL), lambda i, j: (0, 0)),
            pl.BlockSpec((D_MODEL, tn), lambda i, j: (0, j)),
            pl.BlockSpec((D_MODEL, LANES), lambda i, j: (0, 0)),
        ],
        out_specs=[
            pl.BlockSpec((tm, tn), lambda i, j: (i, jnp.maximum(j - 3 * n_q_tiles, 0))),
            pl.BlockSpec((tm, tn), lambda i, j: (i, clamp(j, 1))),
            pl.BlockSpec((tm, tn), lambda i, j: (i, clamp(j, 2))),
            pl.BlockSpec((tm, tn), lambda i, j: (i, jnp.minimum(j, 3 * n_q_tiles - 1))),
            pl.BlockSpec((tm, LANES), lambda i, j: (i, 0)),
        ],
        out_shape=[
            jax.ShapeDtypeStruct((m, D_P32), F32),
            jax.ShapeDtypeStruct((m, D_ATT), F32),
            jax.ShapeDtypeStruct((m, D_ATT), F32),
            jax.ShapeDtypeStruct((m, D_QKV), BF16),
            jax.ShapeDtypeStruct((m, LANES), F32),
        ],
        scratch_shapes=[pltpu.VMEM((tm, D_MODEL), BF16)],
        compiler_params=pltpu.CompilerParams(
            dimension_semantics=("parallel", "arbitrary"), vmem_limit_bytes=VMEM_LIMIT),
        name="in_proj",
    )(x2d, norm_w, w_main, w_dt)


def _rel_bucket(rel):
    half = N_REL_BUCKETS // 2
    max_exact = half // 2
    ret = jnp.where(rel > 0, half, 0)
    n = jnp.abs(rel)
    nf = jnp.maximum(n, 1).astype(F32)
    large = max_exact + (jnp.log(nf / max_exact) / math.log(REL_MAX_DIST / max_exact)
                         * (half - max_exact)).astype(jnp.int32)
    large = jnp.minimum(large, half - 1)
    return ret + jnp.where(n < max_exact, n, large)


def _lambda_init(layer):
    return 0.8 - 0.6 * math.exp(-0.3 * layer)


def _lam_from_refs(lq1, lk1, lq2, lk2, lam0):
    return (jnp.exp(jnp.sum(lq1[...] * lk1[...], axis=-1, keepdims=True))
            - jnp.exp(jnp.sum(lq2[...] * lk2[...], axis=-1, keepdims=True)) + lam0)


def _attn_epilogue(a1, a2, lam, g, subln_w, lam0):
    o = a1 - lam * a2
    ms = jnp.mean(o * o, axis=-1, keepdims=True)
    o = o * lax.rsqrt(ms + NORM_EPS) * subln_w
    o = o * (1.0 - lam0)
    return o * _silu(g)


def _prompt_attn_kernel(q_ref, k_ref, vt_ref, g_ref, bias_ref, sw_ref, lq1, lk1, lq2, lk2,
                        o_ref, m_ref, l_ref, acc_ref, qt_ref, *, tile, qblk, lam0):
    qi = pl.program_id(2)
    d = ATT_HEAD_DIM
    units = [(c, mi) for c in range(tile // qblk) for mi in range(2)]
    qt_ref[...] = q_ref[0].astype(F32).T.astype(BF16)

    def reset():
        m_ref[...] = jnp.full(m_ref.shape, MASK_VALUE, F32)
        l_ref[...] = jnp.zeros(l_ref.shape, F32)
        acc_ref[...] = jnp.zeros(acc_ref.shape, F32)

    def kv_span(j, n, bias_idx, lagged):
        keys = pl.ds(pl.multiple_of(j * tile, tile), n * tile)
        k = k_ref[0, keys, :]
        vt = vt_ref[0, :, keys]
        ss = [jnp.dot(k[:, mi * d:(mi + 1) * d], qt_ref[mi * d:(mi + 1) * d, c * qblk:(c + 1) * qblk],
                      preferred_element_type=F32) for c, mi in units]
        for (c, mi), s in zip(units, ss):
            cols = slice(c * qblk, (c + 1) * qblk)
            if bias_idx is not None:
                s = s + bias_ref[bias_idx, 0, :, cols]
            m_old = m_ref[mi, :, cols]
            if lagged:
                p = jnp.exp2(s - m_old)
                l_new = l_ref[mi, :, cols] + jnp.sum(p, axis=0, keepdims=True)
                acc_new = acc_ref[mi, :, cols] + jnp.dot(vt, p.astype(BF16), preferred_element_type=F32)
                m_new = jnp.maximum(m_old, jnp.max(s, axis=0, keepdims=True))
                alpha = jnp.exp2(m_old - m_new)
                l_ref[mi, :, cols] = l_new * alpha
                acc_ref[mi, :, cols] = acc_new * alpha
            else:
                m_new = jnp.maximum(m_old, jnp.max(s, axis=0, keepdims=True))
                alpha = jnp.exp2(m_old - m_new)
                p = jnp.exp2(s - m_new)
                l_ref[mi, :, cols] = alpha * l_ref[mi, :, cols] + jnp.sum(p, axis=0, keepdims=True)
                pv = jnp.dot(vt, p.astype(BF16), preferred_element_type=F32)
                acc_ref[mi, :, cols] = alpha * acc_ref[mi, :, cols] + pv
            m_ref[mi, :, cols] = m_new

    n_far = jnp.maximum(qi - 1, 0)
    n_quads = n_far // 4
    n_pairs = n_far // 2

    def all_tiles(lagged):
        reset()

        def near_body(j, carry):
            kv_span(j, 1, j - qi + 1, False)
            return carry
        lax.fori_loop(n_far, qi + 1, near_body, 0)

        def far_body(width):
            def body(i, carry):
                kv_span(width * i, width, None, lagged)
                return carry
            return body
        lax.fori_loop(0, n_quads, far_body(4), 0)
        lax.fori_loop(2 * n_quads, n_pairs, far_body(2), 0)
        lax.fori_loop(2 * n_pairs, n_far, far_body(1), 0)

    all_tiles(True)
    finite = jnp.logical_and(jnp.all(jnp.isfinite(l_ref[...])), jnp.all(jnp.isfinite(acc_ref[...])))

    @pl.when(jnp.logical_not(finite))
    def _():
        all_tiles(False)

    lam = _lam_from_refs(lq1, lk1, lq2, lk2, lam0)
    a1 = (acc_ref[0] / l_ref[0]).T
    a2 = (acc_ref[1] / l_ref[1]).T
    o_ref[0] = _attn_epilogue(a1, a2, lam, g_ref[0], sw_ref[...], lam0).astype(BF16)


def _toeplitz(fn, rows, cols):
    period = rows + cols
    d = jnp.arange(period, dtype=jnp.int32)
    g = jnp.moveaxis(fn(jnp.where(d < cols, d, d - period)), 0, -1)
    x = jnp.tile(g, rows)[..., :rows * (period - 1)].reshape(g.shape[:-1] + (rows, period - 1))
    return x[..., :cols]


def _prompt_bias_tiles(rel_table, tile):
    assert tile >= REL_MAX_DIST and tile % CHUNK == 0
    table = rel_table.astype(F32) * LOG2E
    far = table[_rel_bucket(jnp.full((1,), -(tile + 1), jnp.int32))]
    diag = _toeplitz(lambda dd: table[_rel_bucket(-dd)] - far, tile, tile)
    off = _toeplitz(lambda dd: table[_rel_bucket(-dd - tile)] - far, tile, tile)
    kpos = jnp.arange(tile, dtype=jnp.int32)[:, None]
    qpos = jnp.arange(tile, dtype=jnp.int32)[None, :]
    visible = (kpos // CHUNK) <= (qpos // CHUNK)
    diag = jnp.where(visible[None], diag, MASK_VALUE)
    return jnp.stack([off, diag])


def _prompt_attention(qkv, vt, p32, bias_tiles, subln_w, lq1, lk1, lq2, lk2, *, tile, lam0):
    b, s, _ = qkv.shape
    hw = 2 * ATT_HEAD_DIM
    nh = N_ATT_HEADS
    kern = functools.partial(_prompt_attn_kernel, tile=tile, qblk=min(tile, 256), lam0=lam0)
    vec = pl.BlockSpec((1, ATT_HEAD_DIM), lambda bi, h, qi: (0, 0))
    return pl.pallas_call(
        kern,
        grid=(b, nh, s // tile),
        in_specs=[
            pl.BlockSpec((1, tile, hw), lambda bi, h, qi: (bi, qi, h)),
            pl.BlockSpec((1, s, hw), lambda bi, h, qi: (bi, 0, nh + h)),
            pl.BlockSpec((1, hw, s), lambda bi, h, qi: (bi, h, 0)),
            pl.BlockSpec((1, tile, hw), lambda bi, h, qi: (bi, qi, COL_G // hw + h)),
            pl.BlockSpec((2, 1, tile, tile), lambda bi, h, qi: (0, h, 0, 0)),
            pl.BlockSpec((1, hw), lambda bi, h, qi: (0, 0)),
            vec, vec, vec, vec,
        ],
        out_specs=pl.BlockSpec((1, tile, hw), lambda bi, h, qi: (bi, qi, h)),
        out_shape=jax.ShapeDtypeStruct((b, s, D_ATT), BF16),
        scratch_shapes=[
            pltpu.VMEM((2, 1, tile), F32),
            pltpu.VMEM((2, 1, tile), F32),
            pltpu.VMEM((2, hw, tile), F32),
            pltpu.VMEM((hw, tile), BF16),
        ],
        compiler_params=pltpu.CompilerParams(
            dimension_semantics=("parallel", "parallel", "arbitrary"), vmem_limit_bytes=VMEM_LIMIT),
        name="prompt_attention",
    )(qkv, qkv, vt, p32, bias_tiles, subln_w, lq1, lk1, lq2, lk2)


def _sample_attn_kernel(q_ref, kn_ref, vn_ref, kp_ref, vp_ref, g_ref, bp_ref, bn_ref, sw_ref,
                        lq1, lk1, lq2, lk2, o_ref, *, lam0):
    d = ATT_HEAD_DIM
    q = q_ref[0]
    kn = kn_ref[0]
    vn = vn_ref[0]
    kp = kp_ref[0].astype(BF16)
    vp = vp_ref[0].astype(BF16)
    nt = (((1,), (1,)), ((), ()))
    outs = []
    for mi in range(2):
        qm = q[:, mi * d:(mi + 1) * d]
        sp = lax.dot_general(qm, kp[:, mi * d:(mi + 1) * d], nt, preferred_element_type=F32)
        sn = lax.dot_general(qm, kn[:, mi * d:(mi + 1) * d], nt, preferred_element_type=F32)
        sp = sp + bp_ref[0]
        sn = sn + bn_ref[0]
        m = jnp.maximum(jnp.max(sp, axis=-1, keepdims=True), jnp.max(sn, axis=-1, keepdims=True))
        pp = jnp.exp2(sp - m)
        pn = jnp.exp2(sn - m)
        l = jnp.sum(pp, axis=-1, keepdims=True) + jnp.sum(pn, axis=-1, keepdims=True)
        acc = (jnp.dot(pp.astype(BF16), vp, preferred_element_type=F32)
               + jnp.dot(pn.astype(BF16), vn, preferred_element_type=F32))
        outs.append(acc / l)
    lam = _lam_from_refs(lq1, lk1, lq2, lk2, lam0)
    o_ref[0] = _attn_epilogue(outs[0], outs[1], lam, g_ref[0], sw_ref[...], lam0).astype(BF16)


def _sample_bias(rel_table, past_len, s):
    qpos = past_len + jnp.arange(s, dtype=jnp.int32)[:, None]
    kpos = jnp.arange(past_len + s, dtype=jnp.int32)[None, :]
    table = rel_table.astype(F32) * LOG2E
    bias = _toeplitz(lambda dd: table[_rel_bucket(dd - past_len)], s, past_len + s)
    visible = (kpos // CHUNK) <= (qpos // CHUNK)
    return jnp.where(visible[None], bias, MASK_VALUE)


def _sample_attention(qkv, p32, k_past, v_past, bias, subln_w, lq1, lk1, lq2, lk2, *, lam0):
    b, s, _ = qkv.shape
    past = k_past.shape[1]
    hw = 2 * ATT_HEAD_DIM
    nh = N_ATT_HEADS
    bias_p = bias[:, :, :past]
    bias_n = bias[:, :, past:]
    kern = functools.partial(_sample_attn_kernel, lam0=lam0)
    vec = pl.BlockSpec((1, ATT_HEAD_DIM), lambda bi, h: (0, 0))
    return pl.pallas_call(
        kern,
        grid=(b, nh),
        in_specs=[
            pl.BlockSpec((1, s, hw), lambda bi, h: (bi, 0, h)),
            pl.BlockSpec((1, s, hw), lambda bi, h: (bi, 0, nh + h)),
            pl.BlockSpec((1, s, hw), lambda bi, h: (bi, 0, 2 * nh + h)),
            pl.BlockSpec((1, past, hw), lambda bi, h: (bi, 0, h)),
            pl.BlockSpec((1, past, hw), lambda bi, h: (bi, 0, h)),
            pl.BlockSpec((1, s, hw), lambda bi, h: (bi, 0, COL_G // hw + h)),
            pl.BlockSpec((1, s, past), lambda bi, h: (h, 0, 0)),
            pl.BlockSpec((1, s, s), lambda bi, h: (h, 0, 0)),
            pl.BlockSpec((1, hw), lambda bi, h: (0, 0)),
            vec, vec, vec, vec,
        ],
        out_specs=pl.BlockSpec((1, s, hw), lambda bi, h: (bi, 0, h)),
        out_shape=jax.ShapeDtypeStruct((b, s, D_ATT), BF16),
        compiler_params=pltpu.CompilerParams(
            dimension_semantics=("parallel", "parallel"), vmem_limit_bytes=VMEM_LIMIT),
        name="sample_attention",
    )(qkv, qkv, qkv, k_past, v_past, p32, bias_p, bias_n, subln_w, lq1, lk1, lq2, lk2)


def _ssd_kernel(z_ref, xs_ref, b_ref, c_ref, dtT_ref, hist_ref, h0_ref, cw_ref, cb_ref, dtb_ref,
                alog_ref, dskip_ref, nw_ref, y_ref, hout_ref, h_s, xpad_s, xT_s, yT_s, *, valid):
    c = pl.program_id(1)
    L = xs_ref.shape[1]
    P, N, R = SSM_HEAD_DIM, SSM_STATE, HEADS_PER_GROUP

    @pl.when(c == 0)
    def _():
        h_s[...] = h0_ref[0]
        xpad_s[:SUBLANES, :] = hist_ref[0]

    def conv_silu(x_ref, lo, hi):
        xpad_s[SUBLANES:, lo:hi] = x_ref[0]
        acc = cb_ref[:, lo:hi]
        for k in range(CONV_WIDTH):
            acc = acc + (xpad_s[SUBLANES - k:SUBLANES - k + L, lo:hi]
                         * cw_ref[CONV_WIDTH - 1 - k:CONV_WIDTH - k, lo:hi])
        return _silu(acc)

    xs = conv_silu(xs_ref, 0, D_SSM)
    bm = conv_silu(b_ref, D_SSM, D_SSM + D_BC)
    cm = conv_silu(c_ref, D_SSM + D_BC, D_CONV)
    xpad_s[:SUBLANES, :] = xpad_s[L:, :]

    dt = _softplus(dtT_ref[0] + dtb_ref[...])
    if valid < L:
        dt = jnp.where(lax.broadcasted_iota(jnp.int32, dt.shape, 1) < valid, dt, 0.0)
    a = dt * (-jnp.exp(alog_ref[...]))
    s_idx = lax.broadcasted_iota(jnp.int32, (L, L), 0)
    t_idx = lax.broadcasted_iota(jnp.int32, (L, L), 1)
    causal = s_idx <= t_idx
    upper = jnp.where(causal, 1.0, 0.0).astype(BF16)
    a1 = a.astype(BF16)
    r1 = a - a1.astype(F32)
    a2 = r1.astype(BF16)
    a3 = (r1 - a2.astype(F32)).astype(BF16)
    acs = (jnp.dot(a1, upper, preferred_element_type=F32)
           + jnp.dot(a2, upper, preferred_element_type=F32)
           + jnp.dot(a3, upper, preferred_element_type=F32))
    tot = acs[:, L - 1:L]
    e_row = jnp.exp(acs)
    w_row = dt * jnp.exp(tot - acs)
    d_row = jnp.broadcast_to(jnp.exp(tot), (N_SSM_HEADS, N))
    acs_col = jnp.concatenate([acs, jnp.zeros((L - N_SSM_HEADS, L), F32)], axis=0).T

    xT_s[...] = xs.T

    for g in range(N_SSM_GROUPS):
        bg = bm[:, g * N:(g + 1) * N].astype(BF16)
        cgT = cm[:, g * N:(g + 1) * N].T.astype(BF16)
        cbT = jnp.dot(bg, cgT, preferred_element_type=F32)
        for r8 in range(R):
            r = g * R + r8
            rows = slice(r * P, (r + 1) * P)
            seg = acs[r:r + 1, :] - acs_col[:, r:r + 1]
            decay = jnp.exp(jnp.where(causal, seg, MASK_VALUE))
            mT = (cbT * decay).astype(BF16)
            xr = xT_s[rows, :]
            hr = h_s[rows, :]
            y_intra = jnp.dot((xr * dt[r:r + 1, :]).astype(BF16), mT, preferred_element_type=F32)
            y_inter = jnp.dot(hr.astype(BF16), cgT, preferred_element_type=F32)
            yT_s[rows, :] = y_intra + y_inter * e_row[r:r + 1, :]
            upd = jnp.dot((xr * w_row[r:r + 1, :]).astype(BF16), bg, preferred_element_type=F32)
            h_s[rows, :] = hr * d_row[r:r + 1, :] + upd

    y = yT_s[...].T
    y = y + dskip_ref[...] * xs
    y = y * _silu(z_ref[0])
    gs = D_SSM // N_SSM_GROUPS
    outs = []
    for g in range(N_SSM_GROUPS):
        yg = y[:, g * gs:(g + 1) * gs]
        ms = jnp.mean(yg * yg, axis=-1, keepdims=True)
        outs.append(yg * lax.rsqrt(ms + NORM_EPS) * nw_ref[:, g * gs:(g + 1) * gs])
    y_ref[0] = jnp.concatenate(outs, axis=-1).astype(BF16)

    @pl.when(c == pl.num_programs(1) - 1)
    def _():
        hout_ref[0] = h_s[...]


def _ssd(src, cols, dtT, hist8, h0, conv_w, conv_b, dt_bias, a_log, d_full, norm_w, *, valid):
    b, s, _ = src.shape
    L = SSD_L
    col_z, col_xs, col_b, col_c = cols
    kern = functools.partial(_ssd_kernel, valid=valid)
    const2 = lambda shape: pl.BlockSpec(shape, lambda bi, c: (0, 0))
    hp = N_SSM_HEADS * SSM_HEAD_DIM
    return pl.pallas_call(
        kern,
        grid=(b, s // L),
        in_specs=[
            pl.BlockSpec((1, L, D_SSM), lambda bi, c: (bi, c, col_z // D_SSM)),
            pl.BlockSpec((1, L, D_SSM), lambda bi, c: (bi, c, col_xs // D_SSM)),
            pl.BlockSpec((1, L, D_BC), lambda bi, c: (bi, c, col_b // D_BC)),
            pl.BlockSpec((1, L, D_BC), lambda bi, c: (bi, c, col_c // D_BC)),
            pl.BlockSpec((1, N_SSM_HEADS, L), lambda bi, c: (bi, 0, c)),
            pl.BlockSpec((1, SUBLANES, D_CONV), lambda bi, c: (bi, 0, 0)),
            pl.BlockSpec((1, hp, SSM_STATE), lambda bi, c: (bi, 0, 0)),
            const2((CONV_WIDTH, D_CONV)),
            const2((1, D_CONV)),
            const2((N_SSM_HEADS, 1)),
            const2((N_SSM_HEADS, 1)),
            const2((1, D_SSM)),
            const2((1, D_SSM)),
        ],
        out_specs=[
            pl.BlockSpec((1, L, D_SSM), lambda bi, c: (bi, c, 0)),
            pl.BlockSpec((1, hp, SSM_STATE), lambda bi, c: (bi, 0, 0)),
        ],
        out_shape=[
            jax.ShapeDtypeStruct((b, s, D_SSM), BF16),
            jax.ShapeDtypeStruct((b, hp, SSM_STATE), F32),
        ],
        scratch_shapes=[
            pltpu.VMEM((hp, SSM_STATE), F32),
            pltpu.VMEM((SUBLANES + L, D_CONV), F32),
            pltpu.VMEM((D_SSM, L), F32),
            pltpu.VMEM((D_SSM, L), F32),
        ],
        compiler_params=pltpu.CompilerParams(
            dimension_semantics=("parallel", "arbitrary"), vmem_limit_bytes=VMEM_LIMIT),
        name="ssd",
    )(src, src, src, src, dtT, hist8, h0, conv_w, conv_b, dt_bias, a_log, d_full, norm_w)


def _outproj_kernel(a_ref, y_ref, x_ref, w1_ref, w2_ref, fw_ref, o_ref):
    acc = (jnp.dot(a_ref[...], w1_ref[...], preferred_element_type=F32)
           + jnp.dot(y_ref[...], w2_ref[...], preferred_element_type=F32))
    h = x_ref[...] + acc
    ms = jnp.mean(h * h, axis=-1, keepdims=True)
    o_ref[...] = h * lax.rsqrt(ms + NORM_EPS) * fw_ref[...]


def _out_proj(att, y, x2d, w1, w2, final_w, *, tm):
    m = x2d.shape[0]
    const = pl.BlockSpec((D_ATT, D_MODEL), lambda i: (0, 0), pipeline_mode=pl.Buffered(1))
    return pl.pallas_call(
        _outproj_kernel,
        grid=(m // tm,),
        in_specs=[
            pl.BlockSpec((tm, D_ATT), lambda i: (i, 0)),
            pl.BlockSpec((tm, D_SSM), lambda i: (i, 0)),
            pl.BlockSpec((tm, D_MODEL), lambda i: (i, 0)),
            const, const,
            pl.BlockSpec((1, D_MODEL), lambda i: (0, 0)),
        ],
        out_specs=pl.BlockSpec((tm, D_MODEL), lambda i: (i, 0)),
        out_shape=jax.ShapeDtypeStruct((m, D_MODEL), F32),
        compiler_params=pltpu.CompilerParams(
            dimension_semantics=("parallel",), vmem_limit_bytes=VMEM_LIMIT),
        name="out_proj",
    )(att, y, x2d, w1, w2, final_w)


def _tiles(m):
    tm = min(m, 1024)
    assert m % tm == 0
    return tm


def _layer(h, k_past, v_past, conv_past, ssm_past, layer, rel_bias, norm_w, w_main, w_dt, lq1, lk1, lq2, lk2,
           subln_w, conv_w, conv_b, dt_bias, a_log, d_full, ssm_norm_w, w_out1, w_out2, out_norm_w):
    b, s, _ = h.shape
    m = b * s
    lam0 = _lambda_init(layer)
    x2d = h.reshape(m, D_MODEL)
    p32, k_new, v_new, qkv, dt_raw = _in_proj(x2d, norm_w, w_main, w_dt, tm=_tiles(m), tn=512)
    k_new = k_new.reshape(b, s, N_ATT_HEADS, 2, ATT_HEAD_DIM)
    v_new = v_new.reshape(b, s, N_ATT_HEADS, 2 * ATT_HEAD_DIM)
    p32 = p32.reshape(b, s, D_P32)
    qkv = qkv.reshape(b, s, D_QKV)

    if k_past is None:
        tile = min(s, 512)
        vt = jnp.transpose(qkv[:, :, 2 * D_ATT:], (0, 2, 1))
        att = _prompt_attention(qkv, vt, p32, _prompt_bias_tiles(rel_bias, tile), subln_w, lq1, lk1, lq2, lk2,
                                tile=tile, lam0=lam0)
    else:
        past = k_past.shape[1]
        att = _sample_attention(qkv, p32, k_past.reshape(b, past, D_ATT), v_past.reshape(b, past, D_ATT),
                                _sample_bias(rel_bias, past, s), subln_w, lq1, lk1, lq2, lk2, lam0=lam0)

    assert s >= CONV_WIDTH - 1
    conv_new = p32[:, s - (CONV_WIDTH - 1):, COL_XS:COL_XS + D_CONV]
    hist8 = jnp.pad(conv_past, ((0, 0), (SUBLANES - (CONV_WIDTH - 1), 0), (0, 0)))
    dtT = jnp.transpose(dt_raw[:, :N_SSM_HEADS].reshape(b, s, N_SSM_HEADS), (0, 2, 1))
    h0 = ssm_past.reshape(b, N_SSM_HEADS * SSM_HEAD_DIM, SSM_STATE)
    if s % SSD_L == 0:
        src, cols = p32, (COL_Z, COL_XS, COL_B, COL_C)
    else:
        assert s < SSD_L
        pad = SSD_L - s
        src = jnp.pad(p32[:, :, COL_Z:], ((0, 0), (0, pad), (0, 0)))
        cols = (0, D_SSM, 2 * D_SSM, 2 * D_SSM + D_BC)
        dtT = jnp.pad(dtT, ((0, 0), (0, 0), (0, pad)))
    y, ssm_new = _ssd(src, cols, dtT, hist8, h0, conv_w, conv_b, dt_bias, a_log, d_full, ssm_norm_w,
                      valid=min(s, SSD_L))
    y = y[:, :s].reshape(m, D_SSM)
    ssm_new = ssm_new.reshape(b, N_SSM_HEADS, SSM_HEAD_DIM, SSM_STATE)

    out = _out_proj(att.reshape(m, D_ATT), y, x2d, w_out1, w_out2, out_norm_w, tm=min(m, 512))
    return out.reshape(b, s, D_MODEL), k_new, v_new, conv_new, ssm_new


def kernel(x_prompt, x_sample, cache_k, cache_v, cache_conv, state_ssm, rel_bias, norm_w, w_in, lambda_q1,
           lambda_k1, lambda_q2, lambda_k2, subln_w, conv_w, conv_b, dt_bias, A_log, D_skip, ssm_norm_w, w_out,
           final_norm_w):
    depth = w_in.shape[0]
    assert depth == 1, "the final norm is fused into the (single) layer's output projection"
    bp = x_prompt.shape[0]
    l = 0
    col_q = 0
    col_dt = 4 * D_ATT + D_SSM + D_CONV
    w = w_in[l]
    w_main = w[:, col_q:col_dt].astype(BF16)
    w_dt = jnp.pad(w[:, col_dt:], ((0, 0), (0, LANES - N_SSM_HEADS))).astype(BF16)
    row = lambda t: t.reshape(1, -1).astype(F32)
    col = lambda t: t.reshape(-1, 1).astype(F32)
    params = (rel_bias, row(norm_w[l]), w_main, w_dt, row(lambda_q1[l]), row(lambda_k1[l]), row(lambda_q2[l]),
              row(lambda_k2[l]), row(subln_w[l]), conv_w[l].astype(F32), row(conv_b[l]), col(dt_bias[l]),
              col(A_log[l]), row(jnp.repeat(D_skip[l], SSM_HEAD_DIM)), row(ssm_norm_w[l]),
              w_out[l, :D_ATT].astype(BF16), w_out[l, D_ATT:].astype(BF16), row(final_norm_w))
    conv0 = jnp.zeros((bp, CONV_WIDTH - 1, D_CONV), x_prompt.dtype)
    ssm0 = jnp.zeros((bp, N_SSM_HEADS, SSM_HEAD_DIM, SSM_STATE), state_ssm.dtype)
    yp, k1, v1, c1, s1 = _layer(x_prompt, None, None, conv0, ssm0, l, *params)
    ys, k2, v2, c2, s2 = _layer(x_sample, cache_k[l], cache_v[l], cache_conv[l], state_ssm[l], l, *params)
    return (yp, ys, k1[None], v1[None], c1[None], s1[None], k2[None], v2[None], c2[None], s2[None])
```

```python
import functools
import math

import jax
import jax.numpy as jnp
from jax import lax
from jax.experimental import pallas as pl
from jax.experimental.pallas import tpu as pltpu

F32 = jnp.float32
BF16 = jnp.bfloat16

D_MODEL = 2048
CHUNK = 64
NORM_EPS = 1e-5
N_ATT_HEADS = 8
ATT_HEAD_DIM = 128
D_ATT = N_ATT_HEADS * 2 * ATT_HEAD_DIM
N_REL_BUCKETS = 32
REL_MAX_DIST = 128
D_SSM = 2048
SSM_HEAD_DIM = 64
N_SSM_HEADS = D_SSM // SSM_HEAD_DIM
N_SSM_GROUPS = 4
HEADS_PER_GROUP = N_SSM_HEADS // N_SSM_GROUPS
SSM_STATE = 128
CONV_WIDTH = 4
D_BC = N_SSM_GROUPS * SSM_STATE
D_CONV = D_SSM + 2 * D_BC
D_MIX = D_ATT + D_SSM
D_QKV = 3 * D_ATT
D_P32 = D_ATT + D_SSM + D_CONV
COL_G, COL_Z, COL_XS = 0, D_ATT, D_ATT + D_SSM
COL_B, COL_C = COL_XS + D_SSM, COL_XS + D_SSM + D_BC

LANES = 128
SUBLANES = 8
VMEM_LIMIT = 56 * 1024 * 1024
MASK_VALUE = -1e30
LOG2E = math.log2(math.e)
Q_SCALE = ATT_HEAD_DIM ** -0.5 * LOG2E
SSD_L = 128


def _silu(x):
    return x * (1.0 / (1.0 + jnp.exp(-x)))


def _softplus(x):
    return jnp.maximum(x, 0.0) + jnp.log1p(jnp.exp(-jnp.abs(x)))


def _inproj_kernel(x_ref, nw_ref, w_ref, wdt_ref, p_ref, k_ref, v_ref, qkv_ref, dt_ref, u_ref, *, n_q_tiles,
                   slab):
    j = pl.program_id(1)
    tm = x_ref.shape[0]

    @pl.when(j == 0)
    def _():
        def body(r, carry):
            rows = pl.ds(pl.multiple_of(r * slab, slab), slab)
            x = x_ref[rows, :]
            ms = jnp.mean(x * x, axis=-1, keepdims=True)
            u = x * lax.rsqrt(ms + NORM_EPS) * nw_ref[...]
            u_ref[rows, :] = u.astype(BF16)
            return carry
        lax.fori_loop(0, tm // slab, body, 0)
        dt_ref[...] = jnp.dot(u_ref[...], wdt_ref[...], preferred_element_type=F32)

    def project():
        return jnp.dot(u_ref[...], w_ref[...], preferred_element_type=F32)

    @pl.when(j < n_q_tiles)
    def _():
        qkv_ref[...] = (project() * Q_SCALE).astype(BF16)

    @pl.when(jnp.logical_and(j >= n_q_tiles, j < 2 * n_q_tiles))
    def _():
        res = project()
        k_ref[...] = res
        qkv_ref[...] = res.astype(BF16)

    @pl.when(jnp.logical_and(j >= 2 * n_q_tiles, j < 3 * n_q_tiles))
    def _():
        res = project()
        v_ref[...] = res
        qkv_ref[...] = res.astype(BF16)

    @pl.when(j >= 3 * n_q_tiles)
    def _():
        p_ref[...] = project()


def _in_proj(x2d, norm_w, w_main, w_dt, *, tm, tn):
    m = x2d.shape[0]
    n_tiles = w_main.shape[1] // tn
    n_q_tiles = D_ATT // tn
    slab = min(tm, 256)
    kern = functools.partial(_inproj_kernel, n_q_tiles=n_q_tiles, slab=slab)
    clamp = lambda j, lo: jnp.clip(j - lo * n_q_tiles, 0, n_q_tiles - 1)
    return pl.pallas_call(
        kern,
        grid=(m // tm, n_tiles),
        in_specs=[
            pl.BlockSpec((tm, D_MODEL), lambda i, j: (i, 0)),
            pl.BlockSpec((1, D_MODEL), lambda i, j: (0, 0)),
            pl.BlockSpec((D_MODEL, tn), lambda i, j: (0, j)),
            pl.BlockSpec((D_MODEL, LANES), lambda i, j: (0, 0)),
        ],
        out_specs=[
            pl.BlockSpec((tm, tn), lambda i, j: (i, jnp.maximum(j - 3 * n_q_tiles, 0))),
            pl.BlockSpec((tm, tn), lambda i, j: (i, clamp(j, 1))),
            pl.BlockSpec((tm, tn), lambda i, j: (i, clamp(j, 2))),
            pl.BlockSpec((tm, tn), lambda i, j: (i, jnp.minimum(j, 3 * n_q_tiles - 1))),
            pl.BlockSpec((tm, LANES), lambda i, j: (i, 0)),
        ],
        out_shape=[
            jax.ShapeDtypeStruct((m, D_P32), F32),
            jax.ShapeDtypeStruct((m, D_ATT), F32),
            jax.ShapeDtypeStruct((m, D_ATT), F32),
            jax.ShapeDtypeStruct((m, D_QKV), BF16),
            jax.ShapeDtypeStruct((m, LANES), F32),
        ],
        scratch_shapes=[pltpu.VMEM((tm, D_MODEL), BF16)],
        compiler_params=pltpu.CompilerParams(
            dimension_semantics=("parallel", "arbitrary"), vmem_limit_bytes=VMEM_LIMIT),
        name="in_proj",
    )(x2d, norm_w, w_main, w_dt)


def _rel_bucket(rel):
    half = N_REL_BUCKETS // 2
    max_exact = half // 2
    ret = jnp.where(rel > 0, half, 0)
    n = jnp.abs(rel)
    nf = jnp.maximum(n, 1).astype(F32)
    large = max_exact + (jnp.log(nf / max_exact) / math.log(REL_MAX_DIST / max_exact)
                         * (half - max_exact)).astype(jnp.int32)
    large = jnp.minimum(large, half - 1)
    return ret + jnp.where(n < max_exact, n, large)


def _lambda_init(layer):
    return 0.8 - 0.6 * math.exp(-0.3 * layer)


def _lam_from_refs(lq1, lk1, lq2, lk2, lam0):
    return (jnp.exp(jnp.sum(lq1[...] * lk1[...], axis=-1, keepdims=True))
            - jnp.exp(jnp.sum(lq2[...] * lk2[...], axis=-1, keepdims=True)) + lam0)


def _attn_epilogue(a1, a2, lam, g, subln_w, lam0):
    o = a1 - lam * a2
    ms = jnp.mean(o * o, axis=-1, keepdims=True)
    o = o * lax.rsqrt(ms + NORM_EPS) * subln_w
    o = o * (1.0 - lam0)
    return o * _silu(g)


def _prompt_attn_kernel(q_ref, k_ref, vt_ref, g_ref, bias_ref, sw_ref, lq1, lk1, lq2, lk2,
                        o_ref, m_ref, l_ref, acc_ref, qt_ref, on_ref, *, tile, qblk, lam0):
    qi = pl.program_id(2)
    d = ATT_HEAD_DIM
    units = [(c, mi) for c in range(tile // qblk) for mi in range(2)]
    qt_ref[...] = q_ref[0].astype(F32).T.astype(BF16)

    def reset():
        m_ref[...] = jnp.full(m_ref.shape, MASK_VALUE, F32)
        l_ref[...] = jnp.zeros(l_ref.shape, F32)
        acc_ref[...] = jnp.zeros(acc_ref.shape, F32)

    def kv_span(j, n, bias_idx, lagged):
        keys = pl.ds(pl.multiple_of(j * tile, tile), n * tile)
        k = k_ref[0, keys, :]
        vt = vt_ref[0, :, keys]
        ss = [jnp.dot(k[:, mi * d:(mi + 1) * d], qt_ref[mi * d:(mi + 1) * d, c * qblk:(c + 1) * qblk],
                      preferred_element_type=F32) for c, mi in units]
        for (c, mi), s in zip(units, ss):
            cols = slice(c * qblk, (c + 1) * qblk)
            if bias_idx is not None:
                s = s + bias_ref[bias_idx, 0, :, cols]
            m_old = m_ref[mi, :, cols]
            if lagged:
                p = jnp.exp2(s - m_old)
                l_new = l_ref[mi, :, cols] + jnp.sum(p, axis=0, keepdims=True)
                acc_new = acc_ref[mi, :, cols] + jnp.dot(vt, p.astype(BF16), preferred_element_type=F32)
                m_new = jnp.maximum(m_old, jnp.max(s, axis=0, keepdims=True))
                alpha = jnp.exp2(m_old - m_new)
                l_ref[mi, :, cols] = l_new * alpha
                acc_ref[mi, :, cols] = acc_new * alpha
            else:
                m_new = jnp.maximum(m_old, jnp.max(s, axis=0, keepdims=True))
                alpha = jnp.exp2(m_old - m_new)
                p = jnp.exp2(s - m_new)
                l_ref[mi, :, cols] = alpha * l_ref[mi, :, cols] + jnp.sum(p, axis=0, keepdims=True)
                pv = jnp.dot(vt, p.astype(BF16), preferred_element_type=F32)
                acc_ref[mi, :, cols] = alpha * acc_ref[mi, :, cols] + pv
            m_ref[mi, :, cols] = m_new

    n_far = jnp.maximum(qi - 1, 0)
    n_quads = n_far // 4
    n_pairs = n_far // 2

    def all_tiles(lagged):
        reset()

        kv_span(qi, 1, 1, False)

        def off_body(j, carry):
            kv_span(j, 1, 0, lagged)
            return carry
        lax.fori_loop(n_far, qi, off_body, 0)

        def far_body(width):
            def body(i, carry):
                kv_span(width * i, width, None, lagged)
                return carry
            return body
        lax.fori_loop(0, n_quads, far_body(4), 0)
        lax.fori_loop(2 * n_quads, n_pairs, far_body(2), 0)
        lax.fori_loop(2 * n_pairs, n_far, far_body(1), 0)

    lam = _lam_from_refs(lq1, lk1, lq2, lk2, lam0)

    def combine():
        o = acc_ref[0] * (1.0 / l_ref[0]) - lam * (acc_ref[1] * (1.0 / l_ref[1]))
        ms = jnp.mean(o * o, axis=0, keepdims=True)
        on_ref[...] = o * lax.rsqrt(ms + NORM_EPS)
        return ms

    all_tiles(True)
    ms = combine()
    finite = jnp.logical_and(jnp.all(jnp.isfinite(l_ref[...])), jnp.all(jnp.isfinite(ms)))

    @pl.when(jnp.logical_not(finite))
    def _():
        all_tiles(False)
        combine()

    o_ref[0] = (on_ref[...].T * sw_ref[...] * (1.0 - lam0) * _silu(g_ref[0])).astype(BF16)


def _toeplitz(fn, rows, cols):
    period = rows + cols
    d = jnp.arange(period, dtype=jnp.int32)
    g = jnp.moveaxis(fn(jnp.where(d < cols, d, d - period)), 0, -1)
    x = jnp.tile(g, rows)[..., :rows * (period - 1)].reshape(g.shape[:-1] + (rows, period - 1))
    return x[..., :cols]


def _prompt_bias_tiles(rel_table, tile):
    assert tile >= REL_MAX_DIST and tile % CHUNK == 0
    table = rel_table.astype(F32) * LOG2E
    far = table[_rel_bucket(jnp.full((1,), -(tile + 1), jnp.int32))]
    diag = _toeplitz(lambda dd: table[_rel_bucket(-dd)] - far, tile, tile)
    off = _toeplitz(lambda dd: table[_rel_bucket(-dd - tile)] - far, tile, tile)
    kpos = jnp.arange(tile, dtype=jnp.int32)[:, None]
    qpos = jnp.arange(tile, dtype=jnp.int32)[None, :]
    visible = (kpos // CHUNK) <= (qpos // CHUNK)
    diag = jnp.where(visible[None], diag, MASK_VALUE)
    return jnp.stack([off, diag])


def _prompt_attention(qkv, vt, p32, bias_tiles, subln_w, lq1, lk1, lq2, lk2, *, tile, lam0):
    b, s, _ = qkv.shape
    hw = 2 * ATT_HEAD_DIM
    nh = N_ATT_HEADS
    kern = functools.partial(_prompt_attn_kernel, tile=tile, qblk=min(tile, 256), lam0=lam0)
    vec = pl.BlockSpec((1, ATT_HEAD_DIM), lambda bi, h, qi: (0, 0))
    return pl.pallas_call(
        kern,
        grid=(b, nh, s // tile),
        in_specs=[
            pl.BlockSpec((1, tile, hw), lambda bi, h, qi: (bi, qi, h)),
            pl.BlockSpec((1, s, hw), lambda bi, h, qi: (bi, 0, nh + h)),
            pl.BlockSpec((1, hw, s), lambda bi, h, qi: (bi, h, 0)),
            pl.BlockSpec((1, tile, hw), lambda bi, h, qi: (bi, qi, COL_G // hw + h)),
            pl.BlockSpec((2, 1, tile, tile), lambda bi, h, qi: (0, h, 0, 0)),
            pl.BlockSpec((1, hw), lambda bi, h, qi: (0, 0)),
            vec, vec, vec, vec,
        ],
        out_specs=pl.BlockSpec((1, tile, hw), lambda bi, h, qi: (bi, qi, h)),
        out_shape=jax.ShapeDtypeStruct((b, s, D_ATT), BF16),
        scratch_shapes=[
            pltpu.VMEM((2, 1, tile), F32),
            pltpu.VMEM((2, 1, tile), F32),
            pltpu.VMEM((2, hw, tile), F32),
            pltpu.VMEM((hw, tile), BF16),
            pltpu.VMEM((hw, tile), F32),
        ],
        compiler_params=pltpu.CompilerParams(
            dimension_semantics=("parallel", "parallel", "arbitrary"), vmem_limit_bytes=VMEM_LIMIT),
        name="prompt_attention",
    )(qkv, qkv, vt, p32, bias_tiles, subln_w, lq1, lk1, lq2, lk2)


def _sample_attn_kernel(q_ref, kn_ref, vn_ref, kp_ref, vp_ref, g_ref, bp_ref, bn_ref, sw_ref,
                        lq1, lk1, lq2, lk2, o_ref, *, lam0):
    d = ATT_HEAD_DIM
    q = q_ref[0]
    kn = kn_ref[0]
    vn = vn_ref[0]
    kp = kp_ref[0].astype(BF16)
    vp = vp_ref[0].astype(BF16)
    nt = (((1,), (1,)), ((), ()))
    outs = []
    for mi in range(2):
        qm = q[:, mi * d:(mi + 1) * d]
        sp = lax.dot_general(qm, kp[:, mi * d:(mi + 1) * d], nt, preferred_element_type=F32)
        sn = lax.dot_general(qm, kn[:, mi * d:(mi + 1) * d], nt, preferred_element_type=F32)
        sp = sp + bp_ref[0]
        sn = sn + bn_ref[0]
        m = jnp.maximum(jnp.max(sp, axis=-1, keepdims=True), jnp.max(sn, axis=-1, keepdims=True))
        pp = jnp.exp2(sp - m)
        pn = jnp.exp2(sn - m)
        l = jnp.sum(pp, axis=-1, keepdims=True) + jnp.sum(pn, axis=-1, keepdims=True)
        acc = (jnp.dot(pp.astype(BF16), vp, preferred_element_type=F32)
               + jnp.dot(pn.astype(BF16), vn, preferred_element_type=F32))
        outs.append(acc / l)
    lam = _lam_from_refs(lq1, lk1, lq2, lk2, lam0)
    o_ref[0] = _attn_epilogue(outs[0], outs[1], lam, g_ref[0], sw_ref[...], lam0).astype(BF16)


def _sample_bias(rel_table, past_len, s):
    qpos = past_len + jnp.arange(s, dtype=jnp.int32)[:, None]
    kpos = jnp.arange(past_len + s, dtype=jnp.int32)[None, :]
    table = rel_table.astype(F32) * LOG2E
    bias = _toeplitz(lambda dd: table[_rel_bucket(dd - past_len)], s, past_len + s)
    visible = (kpos // CHUNK) <= (qpos // CHUNK)
    return jnp.where(visible[None], bias, MASK_VALUE)


def _sample_attention(qkv, p32, k_past, v_past, bias, subln_w, lq1, lk1, lq2, lk2, *, lam0):
    b, s, _ = qkv.shape
    past = k_past.shape[1]
    hw = 2 * ATT_HEAD_DIM
    nh = N_ATT_HEADS
    bias_p = bias[:, :, :past]
    bias_n = bias[:, :, past:]
    kern = functools.partial(_sample_attn_kernel, lam0=lam0)
    vec = pl.BlockSpec((1, ATT_HEAD_DIM), lambda bi, h: (0, 0))
    return pl.pallas_call(
        kern,
        grid=(b, nh),
        in_specs=[
            pl.BlockSpec((1, s, hw), lambda bi, h: (bi, 0, h)),
            pl.BlockSpec((1, s, hw), lambda bi, h: (bi, 0, nh + h)),
            pl.BlockSpec((1, s, hw), lambda bi, h: (bi, 0, 2 * nh + h)),
            pl.BlockSpec((1, past, hw), lambda bi, h: (bi, 0, h)),
            pl.BlockSpec((1, past, hw), lambda bi, h: (bi, 0, h)),
            pl.BlockSpec((1, s, hw), lambda bi, h: (bi, 0, COL_G // hw + h)),
            pl.BlockSpec((1, s, past), lambda bi, h: (h, 0, 0)),
            pl.BlockSpec((1, s, s), lambda bi, h: (h, 0, 0)),
            pl.BlockSpec((1, hw), lambda bi, h: (0, 0)),
            vec, vec, vec, vec,
        ],
        out_specs=pl.BlockSpec((1, s, hw), lambda bi, h: (bi, 0, h)),
        out_shape=jax.ShapeDtypeStruct((b, s, D_ATT), BF16),
        compiler_params=pltpu.CompilerParams(
            dimension_semantics=("parallel", "parallel"), vmem_limit_bytes=VMEM_LIMIT),
        name="sample_attention",
    )(qkv, qkv, qkv, k_past, v_past, p32, bias_p, bias_n, subln_w, lq1, lk1, lq2, lk2)


def _ssd_kernel(z_ref, xs_ref, b_ref, c_ref, dtT_ref, hist_ref, h0_ref, cw_ref, cb_ref, dtb_ref,
                alog_ref, dskip_ref, nw_ref, y_ref, hout_ref, h_s, xpad_s, xT_s, yT_s, *, valid):
    c = pl.program_id(1)
    L = xs_ref.shape[1]
    P, N, R = SSM_HEAD_DIM, SSM_STATE, HEADS_PER_GROUP

    @pl.when(c == 0)
    def _():
        h_s[...] = h0_ref[0]
        xpad_s[:SUBLANES, :] = hist_ref[0]

    def conv_silu(x_ref, lo, hi):
        xpad_s[SUBLANES:, lo:hi] = x_ref[0]
        acc = cb_ref[:, lo:hi]
        for k in range(CONV_WIDTH):
            acc = acc + (xpad_s[SUBLANES - k:SUBLANES - k + L, lo:hi]
                         * cw_ref[CONV_WIDTH - 1 - k:CONV_WIDTH - k, lo:hi])
        return _silu(acc)

    xs = conv_silu(xs_ref, 0, D_SSM)
    bm = conv_silu(b_ref, D_SSM, D_SSM + D_BC)
    cm = conv_silu(c_ref, D_SSM + D_BC, D_CONV)
    xpad_s[:SUBLANES, :] = xpad_s[L:, :]

    dt = _softplus(dtT_ref[0] + dtb_ref[...])
    if valid < L:
        dt = jnp.where(lax.broadcasted_iota(jnp.int32, dt.shape, 1) < valid, dt, 0.0)
    a = dt * (-jnp.exp(alog_ref[...]))
    s_idx = lax.broadcasted_iota(jnp.int32, (L, L), 0)
    t_idx = lax.broadcasted_iota(jnp.int32, (L, L), 1)
    causal = s_idx <= t_idx
    upper = jnp.where(causal, 1.0, 0.0).astype(BF16)
    a1 = a.astype(BF16)
    r1 = a - a1.astype(F32)
    a2 = r1.astype(BF16)
    a3 = (r1 - a2.astype(F32)).astype(BF16)
    acs = (jnp.dot(a1, upper, preferred_element_type=F32)
           + jnp.dot(a2, upper, preferred_element_type=F32)
           + jnp.dot(a3, upper, preferred_element_type=F32))
    tot = acs[:, L - 1:L]
    e_row = jnp.exp(acs)
    w_row = dt * jnp.exp(tot - acs)
    d_row = jnp.broadcast_to(jnp.exp(tot), (N_SSM_HEADS, N))
    acs_col = jnp.concatenate([acs, jnp.zeros((L - N_SSM_HEADS, L), F32)], axis=0).T

    xT_s[...] = xs.T

    for g in range(N_SSM_GROUPS):
        bg = bm[:, g * N:(g + 1) * N].astype(BF16)
        cgT = cm[:, g * N:(g + 1) * N].T.astype(BF16)
        cbT = jnp.dot(bg, cgT, preferred_element_type=F32)
        for r8 in range(R):
            r = g * R + r8
            rows = slice(r * P, (r + 1) * P)
            seg = acs[r:r + 1, :] - acs_col[:, r:r + 1]
            decay = jnp.exp(jnp.where(causal, seg, MASK_VALUE))
            mT = (cbT * decay).astype(BF16)
            xr = xT_s[rows, :]
            hr = h_s[rows, :]
            y_intra = jnp.dot((xr * dt[r:r + 1, :]).astype(BF16), mT, preferred_element_type=F32)
            y_inter = jnp.dot(hr.astype(BF16), cgT, preferred_element_type=F32)
            yT_s[rows, :] = y_intra + y_inter * e_row[r:r + 1, :]
            upd = jnp.dot((xr * w_row[r:r + 1, :]).astype(BF16), bg, preferred_element_type=F32)
            h_s[rows, :] = hr * d_row[r:r + 1, :] + upd

    y = yT_s[...].T
    y = y + dskip_ref[...] * xs
    y = y * _silu(z_ref[0])
    gs = D_SSM // N_SSM_GROUPS
    outs = []
    for g in range(N_SSM_GROUPS):
        yg = y[:, g * gs:(g + 1) * gs]
        ms = jnp.mean(yg * yg, axis=-1, keepdims=True)
        outs.append(yg * lax.rsqrt(ms + NORM_EPS) * nw_ref[:, g * gs:(g + 1) * gs])
    y_ref[0] = jnp.concatenate(outs, axis=-1).astype(BF16)

    @pl.when(c == pl.num_programs(1) - 1)
    def _():
        hout_ref[0] = h_s[...]


def _ssd(src, cols, dtT, hist8, h0, conv_w, conv_b, dt_bias, a_log, d_full, norm_w, *, valid):
    b, s, _ = src.shape
    L = SSD_L
    col_z, col_xs, col_b, col_c = cols
    kern = functools.partial(_ssd_kernel, valid=valid)
    const2 = lambda shape: pl.BlockSpec(shape, lambda bi, c: (0, 0))
    hp = N_SSM_HEADS * SSM_HEAD_DIM
    return pl.pallas_call(
        kern,
        grid=(b, s // L),
        in_specs=[
            pl.BlockSpec((1, L, D_SSM), lambda bi, c: (bi, c, col_z // D_SSM)),
            pl.BlockSpec((1, L, D_SSM), lambda bi, c: (bi, c, col_xs // D_SSM)),
            pl.BlockSpec((1, L, D_BC), lambda bi, c: (bi, c, col_b // D_BC)),
            pl.BlockSpec((1, L, D_BC), lambda bi, c: (bi, c, col_c // D_BC)),
            pl.BlockSpec((1, N_SSM_HEADS, L), lambda bi, c: (bi, 0, c)),
            pl.BlockSpec((1, SUBLANES, D_CONV), lambda bi, c: (bi, 0, 0)),
            pl.BlockSpec((1, hp, SSM_STATE), lambda bi, c: (bi, 0, 0)),
            const2((CONV_WIDTH, D_CONV)),
            const2((1, D_CONV)),
            const2((N_SSM_HEADS, 1)),
            const2((N_SSM_HEADS, 1)),
            const2((1, D_SSM)),
            const2((1, D_SSM)),
        ],
        out_specs=[
            pl.BlockSpec((1, L, D_SSM), lambda bi, c: (bi, c, 0)),
            pl.BlockSpec((1, hp, SSM_STATE), lambda bi, c: (bi, 0, 0)),
        ],
        out_shape=[
            jax.ShapeDtypeStruct((b, s, D_SSM), BF16),
            jax.ShapeDtypeStruct((b, hp, SSM_STATE), F32),
        ],
        scratch_shapes=[
            pltpu.VMEM((hp, SSM_STATE), F32),
            pltpu.VMEM((SUBLANES + L, D_CONV), F32),
            pltpu.VMEM((D_SSM, L), F32),
            pltpu.VMEM((D_SSM, L), F32),
        ],
        compiler_params=pltpu.CompilerParams(
            dimension_semantics=("parallel", "arbitrary"), vmem_limit_bytes=VMEM_LIMIT),
        name="ssd",
    )(src, src, src, src, dtT, hist8, h0, conv_w, conv_b, dt_bias, a_log, d_full, norm_w)


def _outproj_kernel(a_ref, y_ref, x_ref, w1_ref, w2_ref, fw_ref, o_ref):
    acc = (jnp.dot(a_ref[...], w1_ref[...], preferred_element_type=F32)
           + jnp.dot(y_ref[...], w2_ref[...], preferred_element_type=F32))
    h = x_ref[...] + acc
    ms = jnp.mean(h * h, axis=-1, keepdims=True)
    o_ref[...] = h * lax.rsqrt(ms + NORM_EPS) * fw_ref[...]


def _out_proj(att, y, x2d, w1, w2, final_w, *, tm):
    m = x2d.shape[0]
    const = pl.BlockSpec((D_ATT, D_MODEL), lambda i: (0, 0), pipeline_mode=pl.Buffered(1))
    return pl.pallas_call(
        _outproj_kernel,
        grid=(m // tm,),
        in_specs=[
            pl.BlockSpec((tm, D_ATT), lambda i: (i, 0)),
            pl.BlockSpec((tm, D_SSM), lambda i: (i, 0)),
            pl.BlockSpec((tm, D_MODEL), lambda i: (i, 0)),
            const, const,
            pl.BlockSpec((1, D_MODEL), lambda i: (0, 0)),
        ],
        out_specs=pl.BlockSpec((tm, D_MODEL), lambda i: (i, 0)),
        out_shape=jax.ShapeDtypeStruct((m, D_MODEL), F32),
        compiler_params=pltpu.CompilerParams(
            dimension_semantics=("parallel",), vmem_limit_bytes=VMEM_LIMIT),
        name="out_proj",
    )(att, y, x2d, w1, w2, final_w)


def _tiles(m):
    tm = min(m, 1024)
    assert m % tm == 0
    return tm


def _layer(h, k_past, v_past, conv_past, ssm_past, layer, rel_bias, norm_w, w_main, w_dt, lq1, lk1, lq2, lk2,
           subln_w, conv_w, conv_b, dt_bias, a_log, d_full, ssm_norm_w, w_out1, w_out2, out_norm_w):
    b, s, _ = h.shape
    m = b * s
    lam0 = _lambda_init(layer)
    x2d = h.reshape(m, D_MODEL)
    p32, k_new, v_new, qkv, dt_raw = _in_proj(x2d, norm_w, w_main, w_dt, tm=_tiles(m), tn=512)
    k_new = k_new.reshape(b, s, N_ATT_HEADS, 2, ATT_HEAD_DIM)
    v_new = v_new.reshape(b, s, N_ATT_HEADS, 2 * ATT_HEAD_DIM)
    p32 = p32.reshape(b, s, D_P32)
    qkv = qkv.reshape(b, s, D_QKV)

    if k_past is None:
        tile = min(s, 512)
        vt = jnp.transpose(qkv[:, :, 2 * D_ATT:], (0, 2, 1))
        att = _prompt_attention(qkv, vt, p32, _prompt_bias_tiles(rel_bias, tile), subln_w, lq1, lk1, lq2, lk2,
                                tile=tile, lam0=lam0)
    else:
        past = k_past.shape[1]
        att = _sample_attention(qkv, p32, k_past.reshape(b, past, D_ATT), v_past.reshape(b, past, D_ATT),
                                _sample_bias(rel_bias, past, s), subln_w, lq1, lk1, lq2, lk2, lam0=lam0)

    assert s >= CONV_WIDTH - 1
    conv_new = p32[:, s - (CONV_WIDTH - 1):, COL_XS:COL_XS + D_CONV]
    hist8 = jnp.pad(conv_past, ((0, 0), (SUBLANES - (CONV_WIDTH - 1), 0), (0, 0)))
    dtT = jnp.transpose(dt_raw[:, :N_SSM_HEADS].reshape(b, s, N_SSM_HEADS), (0, 2, 1))
    h0 = ssm_past.reshape(b, N_SSM_HEADS * SSM_HEAD_DIM, SSM_STATE)
    if s % SSD_L == 0:
        src, cols = p32, (COL_Z, COL_XS, COL_B, COL_C)
    else:
        assert s < SSD_L
        pad = SSD_L - s
        src = jnp.pad(p32[:, :, COL_Z:], ((0, 0), (0, pad), (0, 0)))
        cols = (0, D_SSM, 2 * D_SSM, 2 * D_SSM + D_BC)
        dtT = jnp.pad(dtT, ((0, 0), (0, 0), (0, pad)))
    y, ssm_new = _ssd(src, cols, dtT, hist8, h0, conv_w, conv_b, dt_bias, a_log, d_full, ssm_norm_w,
                      valid=min(s, SSD_L))
    y = y[:, :s].reshape(m, D_SSM)
    ssm_new = ssm_new.reshape(b, N_SSM_HEADS, SSM_HEAD_DIM, SSM_STATE)

    out = _out_proj(att.reshape(m, D_ATT), y, x2d, w_out1, w_out2, out_norm_w, tm=min(m, 512))
    return out.reshape(b, s, D_MODEL), k_new, v_new, conv_new, ssm_new


def kernel(x_prompt, x_sample, cache_k, cache_v, cache_conv, state_ssm, rel_bias, norm_w, w_in, lambda_q1,
           lambda_k1, lambda_q2, lambda_k2, subln_w, conv_w, conv_b, dt_bias, A_log, D_skip, ssm_norm_w, w_out,
           final_norm_w):
    depth = w_in.shape[0]
    assert depth == 1, "the final norm is fused into the (single) layer's output projection"
    bp = x_prompt.shape[0]
    l = 0
    col_q = 0
    col_dt = 4 * D_ATT + D_SSM + D_CONV
    w = w_in[l]
    w_main = w[:, col_q:col_dt].astype(BF16)
    w_dt = jnp.pad(w[:, col_dt:], ((0, 0), (0, LANES - N_SSM_HEADS))).astype(BF16)
    row = lambda t: t.reshape(1, -1).astype(F32)
    col = lambda t: t.reshape(-1, 1).astype(F32)
    params = (rel_bias, row(norm_w[l]), w_main, w_dt, row(lambda_q1[l]), row(lambda_k1[l]), row(lambda_q2[l]),
              row(lambda_k2[l]), row(subln_w[l]), conv_w[l].astype(F32), row(conv_b[l]), col(dt_bias[l]),
              col(A_log[l]), row(jnp.repeat(D_skip[l], SSM_HEAD_DIM)), row(ssm_norm_w[l]),
              w_out[l, :D_ATT].astype(BF16), w_out[l, D_ATT:].astype(BF16), row(final_norm_w))
    conv0 = jnp.zeros((bp, CONV_WIDTH - 1, D_CONV), x_prompt.dtype)
    ssm0 = jnp.zeros((bp, N_SSM_HEADS, SSM_HEAD_DIM, SSM_STATE), state_ssm.dtype)
    yp, k1, v1, c1, s1 = _layer(x_prompt, None, None, conv0, ssm0, l, *params)
    ys, k2, v2, c2, s2 = _layer(x_sample, cache_k[l], cache_v[l], cache_conv[l], state_ssm[l], l, *params)
    return (yp, ys, k1[None], v1[None], c1[None], s1[None], k2[None], v2[None], c2[None], s2[None])
```

```python
import functools
import math

import jax
import jax.numpy as jnp
from jax import lax
from jax.experimental import pallas as pl
from jax.experimental.pallas import tpu as pltpu

F32 = jnp.float32
BF16 = jnp.bfloat16

D_MODEL = 2048
CHUNK = 64
NORM_EPS = 1e-5
N_ATT_HEADS = 8
ATT_HEAD_DIM = 128
D_ATT = N_ATT_HEADS * 2 * ATT_HEAD_DIM
N_REL_BUCKETS = 32
REL_MAX_DIST = 128
D_SSM = 2048
SSM_HEAD_DIM = 64
N_SSM_HEADS = D_SSM // SSM_HEAD_DIM
N_SSM_GROUPS = 4
HEADS_PER_GROUP = N_SSM_HEADS // N_SSM_GROUPS
SSM_STATE = 128
CONV_WIDTH = 4
D_BC = N_SSM_GROUPS * SSM_STATE
D_CONV = D_SSM + 2 * D_BC
D_MIX = D_ATT + D_SSM
D_QKV = 3 * D_ATT
D_P32 = D_ATT + D_SSM + D_CONV
COL_G, COL_Z, COL_XS = 0, D_ATT, D_ATT + D_SSM
COL_B, COL_C = COL_XS + D_SSM, COL_XS + D_SSM + D_BC

LANES = 128
SUBLANES = 8
VMEM_LIMIT = 56 * 1024 * 1024
MASK_VALUE = -1e30
LOG2E = math.log2(math.e)
Q_SCALE = ATT_HEAD_DIM ** -0.5 * LOG2E
SSD_L = 128
IN_PROJ_TN = 1024


def _silu(x):
    return x * (1.0 / (1.0 + jnp.exp(-x)))


def _softplus(x):
    return jnp.maximum(x, 0.0) + jnp.log1p(jnp.exp(-jnp.abs(x)))


def _rms_to_scratch(x_ref, nw_ref, u_ref, slab):
    def body(r, carry):
        rows = pl.ds(pl.multiple_of(r * slab, slab), slab)
        x = x_ref[rows, :]
        ms = jnp.mean(x * x, axis=-1, keepdims=True)
        u = x * lax.rsqrt(ms + NORM_EPS) * nw_ref[...]
        u_ref[rows, :] = u.astype(BF16)
        return carry
    lax.fori_loop(0, x_ref.shape[0] // slab, body, 0)


def _inproj_qkv_kernel(x_ref, nw_ref, w_ref, k_ref, v_ref, qkv_ref, u_ref, *, n_q_tiles, slab):
    j = pl.program_id(1)

    @pl.when(j == 0)
    def _():
        _rms_to_scratch(x_ref, nw_ref, u_ref, slab)

    def project():
        return jnp.dot(u_ref[...], w_ref[0], preferred_element_type=F32)

    @pl.when(j < n_q_tiles)
    def _():
        qkv_ref[...] = (project() * Q_SCALE).astype(BF16)

    @pl.when(jnp.logical_and(j >= n_q_tiles, j < 2 * n_q_tiles))
    def _():
        res = project()
        k_ref[...] = res
        qkv_ref[...] = res.astype(BF16)

    @pl.when(j >= 2 * n_q_tiles)
    def _():
        res = project()
        v_ref[...] = res
        qkv_ref[...] = res.astype(BF16)


def _inproj_rest_kernel(x_ref, nw_ref, w_ref, wdt_ref, p_ref, dt_ref, u_ref, *, slab):
    @pl.when(pl.program_id(1) == 0)
    def _():
        _rms_to_scratch(x_ref, nw_ref, u_ref, slab)
        dt_ref[...] = jnp.dot(u_ref[...], wdt_ref[...], preferred_element_type=F32)

    p_ref[...] = jnp.dot(u_ref[...], w_ref[0], preferred_element_type=F32)


def _in_proj(x2d, norm_w, w_tiles, w_dt, *, tm):
    m = x2d.shape[0]
    n_tiles, _, tn = w_tiles.shape
    n_q_tiles = D_ATT // tn
    n_qkv_tiles = 3 * n_q_tiles
    slab = min(tm, 256)
    clamp = lambda j, lo: jnp.clip(j - lo * n_q_tiles, 0, n_q_tiles - 1)
    params = pltpu.CompilerParams(dimension_semantics=("parallel", "arbitrary"), vmem_limit_bytes=VMEM_LIMIT)
    x_spec = pl.BlockSpec((tm, D_MODEL), lambda i, j: (i, 0))
    nw_spec = pl.BlockSpec((1, D_MODEL), lambda i, j: (0, 0))
    k_new, v_new, qkv = pl.pallas_call(
        functools.partial(_inproj_qkv_kernel, n_q_tiles=n_q_tiles, slab=slab),
        grid=(m // tm, n_qkv_tiles),
        in_specs=[x_spec, nw_spec, pl.BlockSpec((1, D_MODEL, tn), lambda i, j: (j, 0, 0))],
        out_specs=[
            pl.BlockSpec((tm, tn), lambda i, j: (i, clamp(j, 1))),
            pl.BlockSpec((tm, tn), lambda i, j: (i, clamp(j, 2))),
            pl.BlockSpec((tm, tn), lambda i, j: (i, j)),
        ],
        out_shape=[
            jax.ShapeDtypeStruct((m, D_ATT), F32),
            jax.ShapeDtypeStruct((m, D_ATT), F32),
            jax.ShapeDtypeStruct((m, D_QKV), BF16),
        ],
        scratch_shapes=[pltpu.VMEM((tm, D_MODEL), BF16)],
        compiler_params=params,
        name="in_proj_qkv",
    )(x2d, norm_w, w_tiles)
    p32, dt_raw = pl.pallas_call(
        functools.partial(_inproj_rest_kernel, slab=slab),
        grid=(m // tm, n_tiles - n_qkv_tiles),
        in_specs=[x_spec, nw_spec, pl.BlockSpec((1, D_MODEL, tn), lambda i, j: (n_qkv_tiles + j, 0, 0)),
                  pl.BlockSpec((D_MODEL, LANES), lambda i, j: (0, 0))],
        out_specs=[
            pl.BlockSpec((tm, tn), lambda i, j: (i, j)),
            pl.BlockSpec((tm, LANES), lambda i, j: (i, 0)),
        ],
        out_shape=[
            jax.ShapeDtypeStruct((m, D_P32), F32),
            jax.ShapeDtypeStruct((m, LANES), F32),
        ],
        scratch_shapes=[pltpu.VMEM((tm, D_MODEL), BF16)],
        compiler_params=params,
        name="in_proj_rest",
    )(x2d, norm_w, w_tiles, w_dt)
    return p32, k_new, v_new, qkv, dt_raw


def _rel_bucket(rel):
    half = N_REL_BUCKETS // 2
    max_exact = half // 2
    ret = jnp.where(rel > 0, half, 0)
    n = jnp.abs(rel)
    nf = jnp.maximum(n, 1).astype(F32)
    large = max_exact + (jnp.log(nf / max_exact) / math.log(REL_MAX_DIST / max_exact)
                         * (half - max_exact)).astype(jnp.int32)
    large = jnp.minimum(large, half - 1)
    return ret + jnp.where(n < max_exact, n, large)


def _lambda_init(layer):
    return 0.8 - 0.6 * math.exp(-0.3 * layer)


def _lam_from_refs(lq1, lk1, lq2, lk2, lam0):
    return (jnp.exp(jnp.sum(lq1[...] * lk1[...], axis=-1, keepdims=True))
            - jnp.exp(jnp.sum(lq2[...] * lk2[...], axis=-1, keepdims=True)) + lam0)


def _attn_epilogue(a1, a2, lam, g, subln_w, lam0):
    o = a1 - lam * a2
    ms = jnp.mean(o * o, axis=-1, keepdims=True)
    o = o * lax.rsqrt(ms + NORM_EPS) * subln_w
    o = o * (1.0 - lam0)
    return o * _silu(g)


def _prompt_attn_kernel(q_ref, k_ref, vt_ref, g_ref, bias_ref, sw_ref, lq1, lk1, lq2, lk2,
                        o_ref, m_ref, l_ref, acc_ref, qt_ref, on_ref, *, tile, qblk, lam0):
    qi = pl.program_id(2)
    d = ATT_HEAD_DIM
    units = [(c, mi) for c in range(tile // qblk) for mi in range(2)]
    qt_ref[...] = q_ref[0].astype(F32).T.astype(BF16)

    def reset(m_init):
        m_ref[...] = jnp.full(m_ref.shape, m_init, F32)
        l_ref[...] = jnp.zeros(l_ref.shape, F32)
        acc_ref[...] = jnp.zeros(acc_ref.shape, F32)

    def kv_span(j, n, bias_idx, lagged):
        keys = pl.ds(pl.multiple_of(j * tile, tile), n * tile)
        k = k_ref[0, keys, :]
        vt = vt_ref[0, :, keys]
        ss = [jnp.dot(k[:, mi * d:(mi + 1) * d], qt_ref[mi * d:(mi + 1) * d, c * qblk:(c + 1) * qblk],
                      preferred_element_type=F32) for c, mi in units]
        for (c, mi), s in zip(units, ss):
            cols = slice(c * qblk, (c + 1) * qblk)
            if bias_idx is not None:
                s = s + bias_ref[bias_idx, 0, :, cols]
            m_old = m_ref[mi, :, cols]
            if lagged:
                p = jnp.exp2(s - m_old)
                l_new = l_ref[mi, :, cols] + jnp.sum(p, axis=0, keepdims=True)
                acc_new = acc_ref[mi, :, cols] + jnp.dot(vt, p.astype(BF16), preferred_element_type=F32)
                m_new = jnp.maximum(m_old, jnp.max(s, axis=0, keepdims=True))
                alpha = jnp.exp2(m_old - m_new)
                l_ref[mi, :, cols] = l_new * alpha
                acc_ref[mi, :, cols] = acc_new * alpha
            else:
                m_new = jnp.maximum(m_old, jnp.max(s, axis=0, keepdims=True))
                alpha = jnp.exp2(m_old - m_new)
                p = jnp.exp2(s - m_new)
                l_ref[mi, :, cols] = alpha * l_ref[mi, :, cols] + jnp.sum(p, axis=0, keepdims=True)
                pv = jnp.dot(vt, p.astype(BF16), preferred_element_type=F32)
                acc_ref[mi, :, cols] = alpha * acc_ref[mi, :, cols] + pv
            m_ref[mi, :, cols] = m_new

    n_far = jnp.maximum(qi - 1, 0)
    n_quads = n_far // 4
    n_pairs = n_far // 2

    def all_tiles(lagged):
        reset(0.0 if lagged else MASK_VALUE)

        kv_span(qi, 1, 1, lagged)

        def off_body(j, carry):
            kv_span(j, 1, 0, lagged)
            return carry
        lax.fori_loop(n_far, qi, off_body, 0)

        def far_body(width):
            def body(i, carry):
                kv_span(width * i, width, None, lagged)
                return carry
            return body
        lax.fori_loop(0, n_quads, far_body(4), 0)
        lax.fori_loop(2 * n_quads, n_pairs, far_body(2), 0)
        lax.fori_loop(2 * n_pairs, n_far, far_body(1), 0)

    lam = _lam_from_refs(lq1, lk1, lq2, lk2, lam0)

    def combine():
        o = acc_ref[0] * (1.0 / l_ref[0]) - lam * (acc_ref[1] * (1.0 / l_ref[1]))
        ms = jnp.mean(o * o, axis=0, keepdims=True)
        on_ref[...] = o * lax.rsqrt(ms + NORM_EPS)
        return ms

    all_tiles(True)
    ms = combine()
    finite = jnp.logical_and(jnp.all(jnp.isfinite(l_ref[...])), jnp.all(jnp.isfinite(ms)))

    @pl.when(jnp.logical_not(finite))
    def _():
        all_tiles(False)
        combine()

    o_ref[0] = (on_ref[...].T * sw_ref[...] * (1.0 - lam0) * _silu(g_ref[0])).astype(BF16)


def _toeplitz(fn, rows, cols):
    period = rows + cols
    d = jnp.arange(period, dtype=jnp.int32)
    g = jnp.moveaxis(fn(jnp.where(d < cols, d, d - period)), 0, -1)
    x = jnp.tile(g, rows)[..., :rows * (period - 1)].reshape(g.shape[:-1] + (rows, period - 1))
    return x[..., :cols]


def _prompt_bias_tiles(rel_table, tile):
    assert tile >= REL_MAX_DIST and tile % CHUNK == 0
    table = rel_table.astype(F32) * LOG2E
    far = table[_rel_bucket(jnp.full((1,), -(tile + 1), jnp.int32))]
    diag = _toeplitz(lambda dd: table[_rel_bucket(-dd)] - far, tile, tile)
    off = _toeplitz(lambda dd: table[_rel_bucket(-dd - tile)] - far, tile, tile)
    kpos = jnp.arange(tile, dtype=jnp.int32)[:, None]
    qpos = jnp.arange(tile, dtype=jnp.int32)[None, :]
    visible = (kpos // CHUNK) <= (qpos // CHUNK)
    diag = jnp.where(visible[None], diag, MASK_VALUE)
    return jnp.stack([off, diag])


def _prompt_attention(qkv, vt, p32, bias_tiles, subln_w, lq1, lk1, lq2, lk2, *, tile, lam0):
    b, s, _ = qkv.shape
    hw = 2 * ATT_HEAD_DIM
    nh = N_ATT_HEADS
    kern = functools.partial(_prompt_attn_kernel, tile=tile, qblk=min(tile, 256), lam0=lam0)
    vec = pl.BlockSpec((1, ATT_HEAD_DIM), lambda bi, h, qi: (0, 0))
    return pl.pallas_call(
        kern,
        grid=(b, nh, s // tile),
        in_specs=[
            pl.BlockSpec((1, tile, hw), lambda bi, h, qi: (bi, qi, h)),
            pl.BlockSpec((1, s, hw), lambda bi, h, qi: (bi, 0, nh + h)),
            pl.BlockSpec((1, hw, s), lambda bi, h, qi: (bi, h, 0)),
            pl.BlockSpec((1, tile, hw), lambda bi, h, qi: (bi, qi, COL_G // hw + h)),
            pl.BlockSpec((2, 1, tile, tile), lambda bi, h, qi: (0, h, 0, 0)),
            pl.BlockSpec((1, hw), lambda bi, h, qi: (0, 0)),
            vec, vec, vec, vec,
        ],
        out_specs=pl.BlockSpec((1, tile, hw), lambda bi, h, qi: (bi, qi, h)),
        out_shape=jax.ShapeDtypeStruct((b, s, D_ATT), BF16),
        scratch_shapes=[
            pltpu.VMEM((2, 1, tile), F32),
            pltpu.VMEM((2, 1, tile), F32),
            pltpu.VMEM((2, hw, tile), F32),
            pltpu.VMEM((hw, tile), BF16),
            pltpu.VMEM((hw, tile), F32),
        ],
        compiler_params=pltpu.CompilerParams(
            dimension_semantics=("parallel", "parallel", "arbitrary"), vmem_limit_bytes=VMEM_LIMIT),
        name="prompt_attention",
    )(qkv, qkv, vt, p32, bias_tiles, subln_w, lq1, lk1, lq2, lk2)


def _sample_attn_kernel(q_ref, kn_ref, vn_ref, kp_ref, vp_ref, g_ref, bp_ref, bn_ref, sw_ref,
                        lq1, lk1, lq2, lk2, o_ref, *, lam0):
    d = ATT_HEAD_DIM
    q = q_ref[0]
    kn = kn_ref[0]
    vn = vn_ref[0]
    kp = kp_ref[0]
    vp = vp_ref[0]
    nt = (((1,), (1,)), ((), ()))
    outs = []
    for mi in range(2):
        qm = q[:, mi * d:(mi + 1) * d]
        sp = lax.dot_general(qm, kp[:, mi * d:(mi + 1) * d], nt, preferred_element_type=F32)
        sn = lax.dot_general(qm, kn[:, mi * d:(mi + 1) * d], nt, preferred_element_type=F32)
        sp = sp + bp_ref[0]
        sn = sn + bn_ref[0]
        m = jnp.maximum(jnp.max(sp, axis=-1, keepdims=True), jnp.max(sn, axis=-1, keepdims=True))
        pp = jnp.exp2(sp - m)
        pn = jnp.exp2(sn - m)
        l = jnp.sum(pp, axis=-1, keepdims=True) + jnp.sum(pn, axis=-1, keepdims=True)
        acc = (jnp.dot(pp.astype(BF16), vp, preferred_element_type=F32)
               + jnp.dot(pn.astype(BF16), vn, preferred_element_type=F32))
        outs.append(acc / l)
    lam = _lam_from_refs(lq1, lk1, lq2, lk2, lam0)
    o_ref[0] = _attn_epilogue(outs[0], outs[1], lam, g_ref[0], sw_ref[...], lam0).astype(BF16)


def _sample_bias(rel_table, past_len, s):
    qpos = past_len + jnp.arange(s, dtype=jnp.int32)[:, None]
    kpos = jnp.arange(past_len + s, dtype=jnp.int32)[None, :]
    table = rel_table.astype(F32) * LOG2E
    bias = _toeplitz(lambda dd: table[_rel_bucket(dd - past_len)], s, past_len + s)
    visible = (kpos // CHUNK) <= (qpos // CHUNK)
    return jnp.where(visible[None], bias, MASK_VALUE)


def _sample_attention(qkv, p32, k_past, v_past, bias, subln_w, lq1, lk1, lq2, lk2, *, lam0):
    b, s, _ = qkv.shape
    past = k_past.shape[1]
    hw = 2 * ATT_HEAD_DIM
    nh = N_ATT_HEADS
    bias_p = bias[:, :, :past]
    bias_n = bias[:, :, past:]
    kern = functools.partial(_sample_attn_kernel, lam0=lam0)
    vec = pl.BlockSpec((1, ATT_HEAD_DIM), lambda bi, h: (0, 0))
    return pl.pallas_call(
        kern,
        grid=(b, nh),
        in_specs=[
            pl.BlockSpec((1, s, hw), lambda bi, h: (bi, 0, h)),
            pl.BlockSpec((1, s, hw), lambda bi, h: (bi, 0, nh + h)),
            pl.BlockSpec((1, s, hw), lambda bi, h: (bi, 0, 2 * nh + h)),
            pl.BlockSpec((1, past, hw), lambda bi, h: (bi, 0, h)),
            pl.BlockSpec((1, past, hw), lambda bi, h: (bi, 0, h)),
            pl.BlockSpec((1, s, hw), lambda bi, h: (bi, 0, COL_G // hw + h)),
            pl.BlockSpec((1, s, past), lambda bi, h: (h, 0, 0)),
            pl.BlockSpec((1, s, s), lambda bi, h: (h, 0, 0)),
            pl.BlockSpec((1, hw), lambda bi, h: (0, 0)),
            vec, vec, vec, vec,
        ],
        out_specs=pl.BlockSpec((1, s, hw), lambda bi, h: (bi, 0, h)),
        out_shape=jax.ShapeDtypeStruct((b, s, D_ATT), BF16),
        compiler_params=pltpu.CompilerParams(
            dimension_semantics=("parallel", "parallel"), vmem_limit_bytes=VMEM_LIMIT),
        name="sample_attention",
    )(qkv, qkv, qkv, k_past, v_past, p32, bias_p, bias_n, subln_w, lq1, lk1, lq2, lk2)


def _ssd_kernel(z_ref, xs_ref, b_ref, c_ref, dtT_ref, hist_ref, h0_ref, cw_ref, cb_ref, dtb_ref,
                alog_ref, dskip_ref, nw_ref, y_ref, hout_ref, h_s, xpad_s, xT_s, yT_s, *, valid):
    c = pl.program_id(1)
    L = xs_ref.shape[1]
    P, N, R = SSM_HEAD_DIM, SSM_STATE, HEADS_PER_GROUP

    @pl.when(c == 0)
    def _():
        h_s[...] = h0_ref[0]
        xpad_s[:SUBLANES, :] = hist_ref[0]

    def conv_silu(x_ref, lo, hi):
        xpad_s[SUBLANES:, lo:hi] = x_ref[0]
        acc = cb_ref[:, lo:hi]
        for k in range(CONV_WIDTH):
            acc = acc + (xpad_s[SUBLANES - k:SUBLANES - k + L, lo:hi]
                         * cw_ref[CONV_WIDTH - 1 - k:CONV_WIDTH - k, lo:hi])
        return _silu(acc)

    xs = conv_silu(xs_ref, 0, D_SSM)
    bm = conv_silu(b_ref, D_SSM, D_SSM + D_BC)
    cm = conv_silu(c_ref, D_SSM + D_BC, D_CONV)
    xpad_s[:SUBLANES, :] = xpad_s[L:, :]

    dt = _softplus(dtT_ref[0] + dtb_ref[...])
    if valid < L:
        dt = jnp.where(lax.broadcasted_iota(jnp.int32, dt.shape, 1) < valid, dt, 0.0)
    a = dt * (-jnp.exp(alog_ref[...]))
    s_idx = lax.broadcasted_iota(jnp.int32, (L, L), 0)
    t_idx = lax.broadcasted_iota(jnp.int32, (L, L), 1)
    causal = s_idx <= t_idx
    upper = jnp.where(causal, 1.0, 0.0).astype(BF16)
    a1 = a.astype(BF16)
    r1 = a - a1.astype(F32)
    a2 = r1.astype(BF16)
    a3 = (r1 - a2.astype(F32)).astype(BF16)
    acs = (jnp.dot(a1, upper, preferred_element_type=F32)
           + jnp.dot(a2, upper, preferred_element_type=F32)
           + jnp.dot(a3, upper, preferred_element_type=F32))
    tot = acs[:, L - 1:L]
    e_row = jnp.exp(acs)
    w_row = dt * jnp.exp(tot - acs)
    d_row = jnp.broadcast_to(jnp.exp(tot), (N_SSM_HEADS, N))
    acs_col = jnp.concatenate([acs, jnp.zeros((L - N_SSM_HEADS, L), F32)], axis=0).T

    xT_s[...] = xs.T

    for g in range(N_SSM_GROUPS):
        bg = bm[:, g * N:(g + 1) * N].astype(BF16)
        cgT = cm[:, g * N:(g + 1) * N].T.astype(BF16)
        cbT = jnp.dot(bg, cgT, preferred_element_type=F32)
        for r8 in range(R):
            r = g * R + r8
            rows = slice(r * P, (r + 1) * P)
            seg = acs[r:r + 1, :] - acs_col[:, r:r + 1]
            decay = jnp.exp(jnp.where(causal, seg, MASK_VALUE))
            mT = (cbT * decay).astype(BF16)
            xr = xT_s[rows, :]
            hr = h_s[rows, :]
            y_intra = jnp.dot((xr * dt[r:r + 1, :]).astype(BF16), mT, preferred_element_type=F32)
            y_inter = jnp.dot(hr.astype(BF16), cgT, preferred_element_type=F32)
            yT_s[rows, :] = y_intra + y_inter * e_row[r:r + 1, :]
            upd = jnp.dot((xr * w_row[r:r + 1, :]).astype(BF16), bg, preferred_element_type=F32)
            h_s[rows, :] = hr * d_row[r:r + 1, :] + upd

    y = yT_s[...].T
    y = y + dskip_ref[...] * xs
    y = y * _silu(z_ref[0])
    gs = D_SSM // N_SSM_GROUPS
    outs = []
    for g in range(N_SSM_GROUPS):
        yg = y[:, g * gs:(g + 1) * gs]
        ms = jnp.mean(yg * yg, axis=-1, keepdims=True)
        outs.append(yg * lax.rsqrt(ms + NORM_EPS) * nw_ref[:, g * gs:(g + 1) * gs])
    y_ref[0] = jnp.concatenate(outs, axis=-1).astype(BF16)

    @pl.when(c == pl.num_programs(1) - 1)
    def _():
        hout_ref[0] = h_s[...]


def _ssd(src, cols, dtT, hist8, h0, conv_w, conv_b, dt_bias, a_log, d_full, norm_w, *, valid):
    b, s, _ = src.shape
    L = SSD_L
    col_z, col_xs, col_b, col_c = cols
    kern = functools.partial(_ssd_kernel, valid=valid)
    const2 = lambda shape: pl.BlockSpec(shape, lambda bi, c: (0, 0))
    hp = N_SSM_HEADS * SSM_HEAD_DIM
    return pl.pallas_call(
        kern,
        grid=(b, s // L),
        in_specs=[
            pl.BlockSpec((1, L, D_SSM), lambda bi, c: (bi, c, col_z // D_SSM)),
            pl.BlockSpec((1, L, D_SSM), lambda bi, c: (bi, c, col_xs // D_SSM)),
            pl.BlockSpec((1, L, D_BC), lambda bi, c: (bi, c, col_b // D_BC)),
            pl.BlockSpec((1, L, D_BC), lambda bi, c: (bi, c, col_c // D_BC)),
            pl.BlockSpec((1, N_SSM_HEADS, L), lambda bi, c: (bi, 0, c)),
            pl.BlockSpec((1, SUBLANES, D_CONV), lambda bi, c: (bi, 0, 0)),
            pl.BlockSpec((1, hp, SSM_STATE), lambda bi, c: (bi, 0, 0)),
            const2((CONV_WIDTH, D_CONV)),
            const2((1, D_CONV)),
            const2((N_SSM_HEADS, 1)),
            const2((N_SSM_HEADS, 1)),
            const2((1, D_SSM)),
            const2((1, D_SSM)),
        ],
        out_specs=[
            pl.BlockSpec((1, L, D_SSM), lambda bi, c: (bi, c, 0)),
            pl.BlockSpec((1, hp, SSM_STATE), lambda bi, c: (bi, 0, 0)),
        ],
        out_shape=[
            jax.ShapeDtypeStruct((b, s, D_SSM), BF16),
            jax.ShapeDtypeStruct((b, hp, SSM_STATE), F32),
        ],
        scratch_shapes=[
            pltpu.VMEM((hp, SSM_STATE), F32),
            pltpu.VMEM((SUBLANES + L, D_CONV), F32),
            pltpu.VMEM((D_SSM, L), F32),
            pltpu.VMEM((D_SSM, L), F32),
        ],
        compiler_params=pltpu.CompilerParams(
            dimension_semantics=("parallel", "arbitrary"), vmem_limit_bytes=VMEM_LIMIT),
        name="ssd",
    )(src, src, src, src, dtT, hist8, h0, conv_w, conv_b, dt_bias, a_log, d_full, norm_w)


def _outproj_kernel(a_ref, y_ref, x_ref, w1_ref, w2_ref, fw_ref, o_ref):
    acc = (jnp.dot(a_ref[...], w1_ref[...], preferred_element_type=F32)
           + jnp.dot(y_ref[...], w2_ref[...], preferred_element_type=F32))
    h = x_ref[...] + acc
    ms = jnp.mean(h * h, axis=-1, keepdims=True)
    o_ref[...] = h * lax.rsqrt(ms + NORM_EPS) * fw_ref[...]


def _out_proj(att, y, x2d, w1, w2, final_w, *, tm):
    m = x2d.shape[0]
    const = pl.BlockSpec((D_ATT, D_MODEL), lambda i: (0, 0), pipeline_mode=pl.Buffered(1))
    return pl.pallas_call(
        _outproj_kernel,
        grid=(m // tm,),
        in_specs=[
            pl.BlockSpec((tm, D_ATT), lambda i: (i, 0)),
            pl.BlockSpec((tm, D_SSM), lambda i: (i, 0)),
            pl.BlockSpec((tm, D_MODEL), lambda i: (i, 0)),
            const, const,
            pl.BlockSpec((1, D_MODEL), lambda i: (0, 0)),
        ],
        out_specs=pl.BlockSpec((tm, D_MODEL), lambda i: (i, 0)),
        out_shape=jax.ShapeDtypeStruct((m, D_MODEL), F32),
        compiler_params=pltpu.CompilerParams(
            dimension_semantics=("parallel",), vmem_limit_bytes=VMEM_LIMIT),
        name="out_proj",
    )(att, y, x2d, w1, w2, final_w)


def _split_lanes_kernel(x_ref, o_ref):
    for c in range(o_ref.shape[1]):
        o_ref[:, c, :] = x_ref[:, c * LANES:(c + 1) * LANES]


def _split_lanes(x2d, *, tm):
    m, n = x2d.shape
    return pl.pallas_call(
        _split_lanes_kernel,
        grid=(m // tm,),
        in_specs=[pl.BlockSpec((tm, n), lambda i: (i, 0))],
        out_specs=pl.BlockSpec((tm, n // LANES, LANES), lambda i: (i, 0, 0)),
        out_shape=jax.ShapeDtypeStruct((m, n // LANES, LANES), x2d.dtype),
        compiler_params=pltpu.CompilerParams(
            dimension_semantics=("parallel",), vmem_limit_bytes=VMEM_LIMIT),
        name="split_lanes",
    )(x2d)


def _merge_cache_kernel(k_ref, v_ref, ko_ref, vo_ref):
    for c in range(k_ref.shape[1]):
        ko_ref[:, c * LANES:(c + 1) * LANES] = k_ref[:, c, :].astype(BF16)
    hw = v_ref.shape[2]
    for h in range(v_ref.shape[1]):
        vo_ref[:, h * hw:(h + 1) * hw] = v_ref[:, h, :].astype(BF16)


def _merge_cache(k3, v3, *, tm):
    rows = k3.shape[0]
    out = jax.ShapeDtypeStruct((rows, D_ATT), BF16)
    return pl.pallas_call(
        _merge_cache_kernel,
        grid=(rows // tm,),
        in_specs=[pl.BlockSpec((tm,) + k3.shape[1:], lambda i: (i, 0, 0)),
                  pl.BlockSpec((tm,) + v3.shape[1:], lambda i: (i, 0, 0))],
        out_specs=[pl.BlockSpec((tm, D_ATT), lambda i: (i, 0)), pl.BlockSpec((tm, D_ATT), lambda i: (i, 0))],
        out_shape=[out, out],
        compiler_params=pltpu.CompilerParams(
            dimension_semantics=("parallel",), vmem_limit_bytes=VMEM_LIMIT),
        name="merge_cache",
    )(k3, v3)


def _tiles(m):
    tm = min(m, 1024)
    assert m % tm == 0
    return tm


def _layer(h, k_past, v_past, conv_past, ssm_past, layer, rel_bias, norm_w, w_main, w_dt, lq1, lk1, lq2, lk2,
           subln_w, conv_w, conv_b, dt_bias, a_log, d_full, ssm_norm_w, w_out1, w_out2, out_norm_w):
    b, s, _ = h.shape
    m = b * s
    lam0 = _lambda_init(layer)
    x2d = h.reshape(m, D_MODEL)
    p32, k_new, v_new, qkv, dt_raw = _in_proj(x2d, norm_w, w_main, w_dt, tm=_tiles(m))
    k_new = _split_lanes(k_new, tm=min(m, 512)).reshape(b, s, N_ATT_HEADS, 2, ATT_HEAD_DIM)
    v_new = v_new.reshape(b, s, N_ATT_HEADS, 2 * ATT_HEAD_DIM)
    p32 = p32.reshape(b, s, D_P32)
    qkv = qkv.reshape(b, s, D_QKV)

    if k_past is None:
        tile = min(s, 512)
        vt = jnp.transpose(qkv[:, :, 2 * D_ATT:], (0, 2, 1))
        att = _prompt_attention(qkv, vt, p32, _prompt_bias_tiles(rel_bias, tile), subln_w, lq1, lk1, lq2, lk2,
                                tile=tile, lam0=lam0)
    else:
        past = k_past.shape[1]
        kp, vp = _merge_cache(k_past.reshape(b * past, D_ATT // LANES, LANES),
                              v_past.reshape(b * past, N_ATT_HEADS, 2 * ATT_HEAD_DIM), tm=min(b * past, 512))
        att = _sample_attention(qkv, p32, kp.reshape(b, past, D_ATT), vp.reshape(b, past, D_ATT),
                                _sample_bias(rel_bias, past, s), subln_w, lq1, lk1, lq2, lk2, lam0=lam0)

    assert s >= CONV_WIDTH - 1
    conv_new = p32[:, s - (CONV_WIDTH - 1):, COL_XS:COL_XS + D_CONV]
    hist8 = jnp.pad(conv_past, ((0, 0), (SUBLANES - (CONV_WIDTH - 1), 0), (0, 0)))
    dtT = jnp.transpose(dt_raw[:, :N_SSM_HEADS].reshape(b, s, N_SSM_HEADS), (0, 2, 1))
    h0 = ssm_past.reshape(b, N_SSM_HEADS * SSM_HEAD_DIM, SSM_STATE)
    if s % SSD_L == 0:
        src, cols = p32, (COL_Z, COL_XS, COL_B, COL_C)
    else:
        assert s < SSD_L
        pad = SSD_L - s
        src = jnp.pad(p32[:, :, COL_Z:], ((0, 0), (0, pad), (0, 0)))
        cols = (0, D_SSM, 2 * D_SSM, 2 * D_SSM + D_BC)
        dtT = jnp.pad(dtT, ((0, 0), (0, 0), (0, pad)))
    y, ssm_new = _ssd(src, cols, dtT, hist8, h0, conv_w, conv_b, dt_bias, a_log, d_full, ssm_norm_w,
                      valid=min(s, SSD_L))
    y = y[:, :s].reshape(m, D_SSM)
    ssm_new = ssm_new.reshape(b, N_SSM_HEADS, SSM_HEAD_DIM, SSM_STATE)

    out = _out_proj(att.reshape(m, D_ATT), y, x2d, w_out1, w_out2, out_norm_w, tm=min(m, 512))
    return out.reshape(b, s, D_MODEL), k_new, v_new, conv_new, ssm_new


def kernel(x_prompt, x_sample, cache_k, cache_v, cache_conv, state_ssm, rel_bias, norm_w, w_in, lambda_q1,
           lambda_k1, lambda_q2, lambda_k2, subln_w, conv_w, conv_b, dt_bias, A_log, D_skip, ssm_norm_w, w_out,
           final_norm_w):
    depth = w_in.shape[0]
    assert depth == 1, "the final norm is fused into the (single) layer's output projection"
    bp = x_prompt.shape[0]
    l = 0
    col_q = 0
    col_dt = 4 * D_ATT + D_SSM + D_CONV
    w = w_in[l]
    w_main = jnp.transpose(w[:, col_q:col_dt].astype(BF16).reshape(D_MODEL, -1, IN_PROJ_TN), (1, 0, 2))
    w_dt = jnp.pad(w[:, col_dt:], ((0, 0), (0, LANES - N_SSM_HEADS))).astype(BF16)
    row = lambda t: t.reshape(1, -1).astype(F32)
    col = lambda t: t.reshape(-1, 1).astype(F32)
    params = (rel_bias, row(norm_w[l]), w_main, w_dt, row(lambda_q1[l]), row(lambda_k1[l]), row(lambda_q2[l]),
              row(lambda_k2[l]), row(subln_w[l]), conv_w[l].astype(F32), row(conv_b[l]), col(dt_bias[l]),
              col(A_log[l]), row(jnp.repeat(D_skip[l], SSM_HEAD_DIM)), row(ssm_norm_w[l]),
              w_out[l, :D_ATT].astype(BF16), w_out[l, D_ATT:].astype(BF16), row(final_norm_w))
    conv0 = jnp.zeros((bp, CONV_WIDTH - 1, D_CONV), x_prompt.dtype)
    ssm0 = jnp.zeros((bp, N_SSM_HEADS, SSM_HEAD_DIM, SSM_STATE), state_ssm.dtype)
    yp, k1, v1, c1, s1 = _layer(x_prompt, None, None, conv0, ssm0, l, *params)
    ys, k2, v2, c2, s2 = _layer(x_sample, cache_k[l], cache_v[l], cache_conv[l], state_ssm[l], l, *params)
    return (yp, ys, k1[None], v1[None], c1[None], s1[None], k2[None], v2[None], c2[None], s2[None])
```

```python
import functools
import math

import jax
import jax.numpy as jnp
from jax import lax
from jax.experimental import pallas as pl
from jax.experimental.pallas import tpu as pltpu

F32 = jnp.float32
BF16 = jnp.bfloat16

D_MODEL = 2048
CHUNK = 64
NORM_EPS = 1e-5
N_ATT_HEADS = 8
ATT_HEAD_DIM = 128
D_ATT = N_ATT_HEADS * 2 * ATT_HEAD_DIM
N_REL_BUCKETS = 32
REL_MAX_DIST = 128
D_SSM = 2048
SSM_HEAD_DIM = 64
N_SSM_HEADS = D_SSM // SSM_HEAD_DIM
N_SSM_GROUPS = 4
HEADS_PER_GROUP = N_SSM_HEADS // N_SSM_GROUPS
SSM_STATE = 128
CONV_WIDTH = 4
D_BC = N_SSM_GROUPS * SSM_STATE
D_CONV = D_SSM + 2 * D_BC
D_MIX = D_ATT + D_SSM
D_QKV = 3 * D_ATT
D_P32 = D_ATT + D_SSM + D_CONV
COL_G, COL_Z, COL_XS = 0, D_ATT, D_ATT + D_SSM
COL_B, COL_C = COL_XS + D_SSM, COL_XS + D_SSM + D_BC

LANES = 128
SUBLANES = 8
VMEM_LIMIT = 56 * 1024 * 1024
MASK_VALUE = -1e30
LOG2E = math.log2(math.e)
Q_SCALE = ATT_HEAD_DIM ** -0.5 * LOG2E
SSD_L = 128
IN_PROJ_TN = 1024


def _silu(x):
    return x * (1.0 / (1.0 + jnp.exp(-x)))


def _softplus(x):
    return jnp.maximum(x, 0.0) + jnp.log1p(jnp.exp(-jnp.abs(x)))


def _rms_to_scratch(x_ref, nw_ref, u_ref, slab):
    def body(r, carry):
        rows = pl.ds(pl.multiple_of(r * slab, slab), slab)
        x = x_ref[rows, :]
        ms = jnp.mean(x * x, axis=-1, keepdims=True)
        u = x * lax.rsqrt(ms + NORM_EPS) * nw_ref[...]
        u_ref[rows, :] = u.astype(BF16)
        return carry
    lax.fori_loop(0, x_ref.shape[0] // slab, body, 0)


def _inproj_qkv_kernel(x_ref, nw_ref, w_ref, k_ref, v_ref, qkv_ref, u_ref, *, n_q_tiles, slab):
    j = pl.program_id(1)

    @pl.when(j == 0)
    def _():
        _rms_to_scratch(x_ref, nw_ref, u_ref, slab)

    def project():
        return jnp.dot(u_ref[...], w_ref[0], preferred_element_type=F32)

    @pl.when(j < n_q_tiles)
    def _():
        qkv_ref[...] = (project() * Q_SCALE).astype(BF16)

    @pl.when(jnp.logical_and(j >= n_q_tiles, j < 2 * n_q_tiles))
    def _():
        res = project()
        k_ref[...] = res
        qkv_ref[...] = res.astype(BF16)

    @pl.when(j >= 2 * n_q_tiles)
    def _():
        res = project()
        v_ref[...] = res
        qkv_ref[...] = res.astype(BF16)


def _inproj_rest_kernel(x_ref, nw_ref, w_ref, wdt_ref, p_ref, dt_ref, u_ref, *, slab):
    @pl.when(pl.program_id(1) == 0)
    def _():
        _rms_to_scratch(x_ref, nw_ref, u_ref, slab)
        dt_ref[...] = jnp.dot(u_ref[...], wdt_ref[...], preferred_element_type=F32)

    p_ref[...] = jnp.dot(u_ref[...], w_ref[0], preferred_element_type=F32)


def _in_proj(x2d, norm_w, w_tiles, w_dt, *, tm):
    m = x2d.shape[0]
    n_tiles, _, tn = w_tiles.shape
    n_q_tiles = D_ATT // tn
    n_qkv_tiles = 3 * n_q_tiles
    slab = min(tm, 256)
    clamp = lambda j, lo: jnp.clip(j - lo * n_q_tiles, 0, n_q_tiles - 1)
    params = pltpu.CompilerParams(dimension_semantics=("parallel", "arbitrary"), vmem_limit_bytes=VMEM_LIMIT)
    x_spec = pl.BlockSpec((tm, D_MODEL), lambda i, j: (i, 0))
    nw_spec = pl.BlockSpec((1, D_MODEL), lambda i, j: (0, 0))
    k_new, v_new, qkv = pl.pallas_call(
        functools.partial(_inproj_qkv_kernel, n_q_tiles=n_q_tiles, slab=slab),
        grid=(m // tm, n_qkv_tiles),
        in_specs=[x_spec, nw_spec, pl.BlockSpec((1, D_MODEL, tn), lambda i, j: (j, 0, 0))],
        out_specs=[
            pl.BlockSpec((tm, tn), lambda i, j: (i, clamp(j, 1))),
            pl.BlockSpec((tm, tn), lambda i, j: (i, clamp(j, 2))),
            pl.BlockSpec((tm, tn), lambda i, j: (i, j)),
        ],
        out_shape=[
            jax.ShapeDtypeStruct((m, D_ATT), F32),
            jax.ShapeDtypeStruct((m, D_ATT), F32),
            jax.ShapeDtypeStruct((m, D_QKV), BF16),
        ],
        scratch_shapes=[pltpu.VMEM((tm, D_MODEL), BF16)],
        compiler_params=params,
        name="in_proj_qkv",
    )(x2d, norm_w, w_tiles)
    p32, dt_raw = pl.pallas_call(
        functools.partial(_inproj_rest_kernel, slab=slab),
        grid=(m // tm, n_tiles - n_qkv_tiles),
        in_specs=[x_spec, nw_spec, pl.BlockSpec((1, D_MODEL, tn), lambda i, j: (n_qkv_tiles + j, 0, 0)),
                  pl.BlockSpec((D_MODEL, LANES), lambda i, j: (0, 0))],
        out_specs=[
            pl.BlockSpec((tm, tn), lambda i, j: (i, j)),
            pl.BlockSpec((tm, LANES), lambda i, j: (i, 0)),
        ],
        out_shape=[
            jax.ShapeDtypeStruct((m, D_P32), F32),
            jax.ShapeDtypeStruct((m, LANES), F32),
        ],
        scratch_shapes=[pltpu.VMEM((tm, D_MODEL), BF16)],
        compiler_params=params,
        name="in_proj_rest",
    )(x2d, norm_w, w_tiles, w_dt)
    return p32, k_new, v_new, qkv, dt_raw


def _rel_bucket(rel):
    half = N_REL_BUCKETS // 2
    max_exact = half // 2
    ret = jnp.where(rel > 0, half, 0)
    n = jnp.abs(rel)
    nf = jnp.maximum(n, 1).astype(F32)
    large = max_exact + (jnp.log(nf / max_exact) / math.log(REL_MAX_DIST / max_exact)
                         * (half - max_exact)).astype(jnp.int32)
    large = jnp.minimum(large, half - 1)
    return ret + jnp.where(n < max_exact, n, large)


def _lambda_init(layer):
    return 0.8 - 0.6 * math.exp(-0.3 * layer)


def _lam_from_refs(lq1, lk1, lq2, lk2, lam0):
    return (jnp.exp(jnp.sum(lq1[...] * lk1[...], axis=-1, keepdims=True))
            - jnp.exp(jnp.sum(lq2[...] * lk2[...], axis=-1, keepdims=True)) + lam0)


def _attn_epilogue(a1, a2, lam, g, subln_w, lam0):
    o = a1 - lam * a2
    ms = jnp.mean(o * o, axis=-1, keepdims=True)
    o = o * lax.rsqrt(ms + NORM_EPS) * subln_w
    o = o * (1.0 - lam0)
    return o * _silu(g)


def _prompt_attn_kernel(q_ref, k_ref, vt_ref, g_ref, brow_ref, sw_ref, lq1, lk1, lq2, lk2,
                        o_ref, m_ref, l_ref, acc_ref, qt_ref, on_ref, bias_ref, *, tile, qblk, lam0):
    qi = pl.program_id(2)
    d = ATT_HEAD_DIM
    units = [(c, mi) for c in range(tile // qblk) for mi in range(2)]
    qt_ref[...] = q_ref[0].astype(F32).T.astype(BF16)

    @pl.when(qi == 0)
    def _():
        for t in range(2):
            rows = jnp.broadcast_to(brow_ref[t, 0], (tile, 2 * tile))
            bias_ref[t] = pltpu.roll(rows, 0, 1, stride=1, stride_axis=0)[:, :tile]
        kc = lax.broadcasted_iota(jnp.int32, (tile, tile), 0) // CHUNK
        qc = lax.broadcasted_iota(jnp.int32, (tile, tile), 1) // CHUNK
        bias_ref[1] = jnp.where(kc <= qc, bias_ref[1], MASK_VALUE)

    def reset(m_init):
        m_ref[...] = jnp.full(m_ref.shape, m_init, F32)
        l_ref[...] = jnp.zeros(l_ref.shape, F32)
        acc_ref[...] = jnp.zeros(acc_ref.shape, F32)

    def kv_span(j, n, bias_idx, lagged):
        keys = pl.ds(pl.multiple_of(j * tile, tile), n * tile)
        k = k_ref[0, keys, :]
        vt = vt_ref[0, :, keys]
        ss = [jnp.dot(k[:, mi * d:(mi + 1) * d], qt_ref[mi * d:(mi + 1) * d, c * qblk:(c + 1) * qblk],
                      preferred_element_type=F32) for c, mi in units]
        for (c, mi), s in zip(units, ss):
            cols = slice(c * qblk, (c + 1) * qblk)
            if bias_idx is not None:
                s = s + bias_ref[bias_idx, :, cols]
            m_old = m_ref[mi, :, cols]
            if lagged:
                p = jnp.exp2(s - m_old)
                l_new = l_ref[mi, :, cols] + jnp.sum(p, axis=0, keepdims=True)
                acc_new = acc_ref[mi, :, cols] + jnp.dot(vt, p.astype(BF16), preferred_element_type=F32)
                m_new = jnp.maximum(m_old, jnp.max(s, axis=0, keepdims=True))
                alpha = jnp.exp2(m_old - m_new)
                l_ref[mi, :, cols] = l_new * alpha
                acc_ref[mi, :, cols] = acc_new * alpha
            else:
                m_new = jnp.maximum(m_old, jnp.max(s, axis=0, keepdims=True))
                alpha = jnp.exp2(m_old - m_new)
                p = jnp.exp2(s - m_new)
                l_ref[mi, :, cols] = alpha * l_ref[mi, :, cols] + jnp.sum(p, axis=0, keepdims=True)
                pv = jnp.dot(vt, p.astype(BF16), preferred_element_type=F32)
                acc_ref[mi, :, cols] = alpha * acc_ref[mi, :, cols] + pv
            m_ref[mi, :, cols] = m_new

    n_far = jnp.maximum(qi - 1, 0)
    n_quads = n_far // 4
    n_pairs = n_far // 2

    def all_tiles(lagged):
        reset(0.0 if lagged else MASK_VALUE)

        kv_span(qi, 1, 1, lagged)

        def off_body(j, carry):
            kv_span(j, 1, 0, lagged)
            return carry
        lax.fori_loop(n_far, qi, off_body, 0)

        def far_body(width):
            def body(i, carry):
                kv_span(width * i, width, None, lagged)
                return carry
            return body
        lax.fori_loop(0, n_quads, far_body(4), 0)
        lax.fori_loop(2 * n_quads, n_pairs, far_body(2), 0)
        lax.fori_loop(2 * n_pairs, n_far, far_body(1), 0)

    lam = _lam_from_refs(lq1, lk1, lq2, lk2, lam0)

    def combine():
        o = acc_ref[0] * (1.0 / l_ref[0]) - lam * (acc_ref[1] * (1.0 / l_ref[1]))
        ms = jnp.mean(o * o, axis=0, keepdims=True)
        on_ref[...] = o * lax.rsqrt(ms + NORM_EPS)
        return ms

    all_tiles(True)
    ms = combine()
    finite = jnp.logical_and(jnp.all(jnp.isfinite(l_ref[...])), jnp.all(jnp.isfinite(ms)))

    @pl.when(jnp.logical_not(finite))
    def _():
        all_tiles(False)
        combine()

    o_ref[0] = (on_ref[...].T * sw_ref[...] * (1.0 - lam0) * _silu(g_ref[0])).astype(BF16)


def _toeplitz(fn, rows, cols):
    period = rows + cols
    d = jnp.arange(period, dtype=jnp.int32)
    g = jnp.moveaxis(fn(jnp.where(d < cols, d, d - period)), 0, -1)
    x = jnp.tile(g, rows)[..., :rows * (period - 1)].reshape(g.shape[:-1] + (rows, period - 1))
    return x[..., :cols]


def _prompt_bias_rows(rel_table, tile):
    assert tile >= REL_MAX_DIST and tile % CHUNK == 0
    table = rel_table.astype(F32) * LOG2E
    far = table[_rel_bucket(jnp.full((1,), -(tile + 1), jnp.int32))]
    d = jnp.arange(2 * tile, dtype=jnp.int32)
    dd = jnp.where(d < tile, d, d - 2 * tile)
    diag = table[_rel_bucket(-dd)] - far
    off = table[_rel_bucket(-dd - tile)] - far
    return jnp.transpose(jnp.stack([off, diag]), (0, 2, 1))[:, :, None, :]


def _prompt_attention(qkv, vt, p32, bias_rows, subln_w, lq1, lk1, lq2, lk2, *, tile, lam0):
    b, s, _ = qkv.shape
    hw = 2 * ATT_HEAD_DIM
    nh = N_ATT_HEADS
    kern = functools.partial(_prompt_attn_kernel, tile=tile, qblk=min(tile, 256), lam0=lam0)
    vec = pl.BlockSpec((1, ATT_HEAD_DIM), lambda bi, h, qi: (0, 0))
    return pl.pallas_call(
        kern,
        grid=(b, nh, s // tile),
        in_specs=[
            pl.BlockSpec((1, tile, hw), lambda bi, h, qi: (bi, qi, h)),
            pl.BlockSpec((1, s, hw), lambda bi, h, qi: (bi, 0, nh + h)),
            pl.BlockSpec((1, hw, s), lambda bi, h, qi: (bi, h, 0)),
            pl.BlockSpec((1, tile, hw), lambda bi, h, qi: (bi, qi, COL_G // hw + h)),
            pl.BlockSpec((2, 1, 1, 2 * tile), lambda bi, h, qi: (0, h, 0, 0)),
            pl.BlockSpec((1, hw), lambda bi, h, qi: (0, 0)),
            vec, vec, vec, vec,
        ],
        out_specs=pl.BlockSpec((1, tile, hw), lambda bi, h, qi: (bi, qi, h)),
        out_shape=jax.ShapeDtypeStruct((b, s, D_ATT), BF16),
        scratch_shapes=[
            pltpu.VMEM((2, 1, tile), F32),
            pltpu.VMEM((2, 1, tile), F32),
            pltpu.VMEM((2, hw, tile), F32),
            pltpu.VMEM((hw, tile), BF16),
            pltpu.VMEM((hw, tile), F32),
            pltpu.VMEM((2, tile, tile), F32),
        ],
        compiler_params=pltpu.CompilerParams(
            dimension_semantics=("parallel", "parallel", "arbitrary"), vmem_limit_bytes=VMEM_LIMIT),
        name="prompt_attention",
    )(qkv, qkv, vt, p32, bias_rows, subln_w, lq1, lk1, lq2, lk2)


def _sample_attn_kernel(q_ref, kn_ref, vn_ref, kp_ref, vp_ref, g_ref, bp_ref, bn_ref, sw_ref,
                        lq1, lk1, lq2, lk2, o_ref, *, lam0):
    d = ATT_HEAD_DIM
    q = q_ref[0]
    kn = kn_ref[0]
    vn = vn_ref[0]
    kp = kp_ref[0]
    vp = vp_ref[0]
    nt = (((1,), (1,)), ((), ()))
    outs = []
    for mi in range(2):
        qm = q[:, mi * d:(mi + 1) * d]
        sp = lax.dot_general(qm, kp[:, mi * d:(mi + 1) * d], nt, preferred_element_type=F32)
        sn = lax.dot_general(qm, kn[:, mi * d:(mi + 1) * d], nt, preferred_element_type=F32)
        sp = sp + bp_ref[0]
        sn = sn + bn_ref[0]
        m = jnp.maximum(jnp.max(sp, axis=-1, keepdims=True), jnp.max(sn, axis=-1, keepdims=True))
        pp = jnp.exp2(sp - m)
        pn = jnp.exp2(sn - m)
        l = jnp.sum(pp, axis=-1, keepdims=True) + jnp.sum(pn, axis=-1, keepdims=True)
        acc = (jnp.dot(pp.astype(BF16), vp, preferred_element_type=F32)
               + jnp.dot(pn.astype(BF16), vn, preferred_element_type=F32))
        outs.append(acc / l)
    lam = _lam_from_refs(lq1, lk1, lq2, lk2, lam0)
    o_ref[0] = _attn_epilogue(outs[0], outs[1], lam, g_ref[0], sw_ref[...], lam0).astype(BF16)


def _sample_bias(rel_table, past_len, s):
    qpos = past_len + jnp.arange(s, dtype=jnp.int32)[:, None]
    kpos = jnp.arange(past_len + s, dtype=jnp.int32)[None, :]
    table = rel_table.astype(F32) * LOG2E
    bias = _toeplitz(lambda dd: table[_rel_bucket(dd - past_len)], s, past_len + s)
    visible = (kpos // CHUNK) <= (qpos // CHUNK)
    return jnp.where(visible[None], bias, MASK_VALUE)


def _sample_attention(qkv, p32, k_past, v_past, bias, subln_w, lq1, lk1, lq2, lk2, *, lam0):
    b, s, _ = qkv.shape
    past = k_past.shape[1]
    hw = 2 * ATT_HEAD_DIM
    nh = N_ATT_HEADS
    bias_p = bias[:, :, :past]
    bias_n = bias[:, :, past:]
    kern = functools.partial(_sample_attn_kernel, lam0=lam0)
    vec = pl.BlockSpec((1, ATT_HEAD_DIM), lambda bi, h: (0, 0))
    return pl.pallas_call(
        kern,
        grid=(b, nh),
        in_specs=[
            pl.BlockSpec((1, s, hw), lambda bi, h: (bi, 0, h)),
            pl.BlockSpec((1, s, hw), lambda bi, h: (bi, 0, nh + h)),
            pl.BlockSpec((1, s, hw), lambda bi, h: (bi, 0, 2 * nh + h)),
            pl.BlockSpec((1, past, hw), lambda bi, h: (bi, 0, h)),
            pl.BlockSpec((1, past, hw), lambda bi, h: (bi, 0, h)),
            pl.BlockSpec((1, s, hw), lambda bi, h: (bi, 0, COL_G // hw + h)),
            pl.BlockSpec((1, s, past), lambda bi, h: (h, 0, 0)),
            pl.BlockSpec((1, s, s), lambda bi, h: (h, 0, 0)),
            pl.BlockSpec((1, hw), lambda bi, h: (0, 0)),
            vec, vec, vec, vec,
        ],
        out_specs=pl.BlockSpec((1, s, hw), lambda bi, h: (bi, 0, h)),
        out_shape=jax.ShapeDtypeStruct((b, s, D_ATT), BF16),
        compiler_params=pltpu.CompilerParams(
            dimension_semantics=("parallel", "parallel"), vmem_limit_bytes=VMEM_LIMIT),
        name="sample_attention",
    )(qkv, qkv, qkv, k_past, v_past, p32, bias_p, bias_n, subln_w, lq1, lk1, lq2, lk2)


def _ssd_kernel(z_ref, xs_ref, b_ref, c_ref, dtT_ref, hist_ref, h0_ref, cw_ref, cb_ref, dtb_ref,
                alog_ref, dskip_ref, nw_ref, y_ref, hout_ref, h_s, xpad_s, xT_s, yT_s, *, valid):
    c = pl.program_id(1)
    L = xs_ref.shape[1]
    P, N, R = SSM_HEAD_DIM, SSM_STATE, HEADS_PER_GROUP

    @pl.when(c == 0)
    def _():
        h_s[...] = h0_ref[0]
        xpad_s[:SUBLANES, :] = hist_ref[0]

    def conv_silu(x_ref, lo, hi):
        xpad_s[SUBLANES:, lo:hi] = x_ref[0]
        acc = cb_ref[:, lo:hi]
        for k in range(CONV_WIDTH):
            acc = acc + (xpad_s[SUBLANES - k:SUBLANES - k + L, lo:hi]
                         * cw_ref[CONV_WIDTH - 1 - k:CONV_WIDTH - k, lo:hi])
        return _silu(acc)

    xs = conv_silu(xs_ref, 0, D_SSM)
    bm = conv_silu(b_ref, D_SSM, D_SSM + D_BC)
    cm = conv_silu(c_ref, D_SSM + D_BC, D_CONV)
    xpad_s[:SUBLANES, :] = xpad_s[L:, :]

    dt = _softplus(dtT_ref[0] + dtb_ref[...])
    if valid < L:
        dt = jnp.where(lax.broadcasted_iota(jnp.int32, dt.shape, 1) < valid, dt, 0.0)
    a = dt * (-jnp.exp(alog_ref[...]))
    s_idx = lax.broadcasted_iota(jnp.int32, (L, L), 0)
    t_idx = lax.broadcasted_iota(jnp.int32, (L, L), 1)
    causal = s_idx <= t_idx
    upper = jnp.where(causal, 1.0, 0.0).astype(BF16)
    a1 = a.astype(BF16)
    r1 = a - a1.astype(F32)
    a2 = r1.astype(BF16)
    a3 = (r1 - a2.astype(F32)).astype(BF16)
    acs = (jnp.dot(a1, upper, preferred_element_type=F32)
           + jnp.dot(a2, upper, preferred_element_type=F32)
           + jnp.dot(a3, upper, preferred_element_type=F32))
    tot = acs[:, L - 1:L]
    e_row = jnp.exp(acs)
    w_row = dt * jnp.exp(tot - acs)
    d_row = jnp.broadcast_to(jnp.exp(tot), (N_SSM_HEADS, N))
    acs_col = jnp.concatenate([acs, jnp.zeros((L - N_SSM_HEADS, L), F32)], axis=0).T

    xT_s[...] = xs.T

    for g in range(N_SSM_GROUPS):
        bg = bm[:, g * N:(g + 1) * N].astype(BF16)
        cgT = cm[:, g * N:(g + 1) * N].T.astype(BF16)
        cbT = jnp.dot(bg, cgT, preferred_element_type=F32)
        for r8 in range(R):
            r = g * R + r8
            rows = slice(r * P, (r + 1) * P)
            seg = acs[r:r + 1, :] - acs_col[:, r:r + 1]
            decay = jnp.exp(jnp.where(causal, seg, MASK_VALUE))
            mT = (cbT * decay).astype(BF16)
            xr = xT_s[rows, :]
            hr = h_s[rows, :]
            y_intra = jnp.dot((xr * dt[r:r + 1, :]).astype(BF16), mT, preferred_element_type=F32)
            y_inter = jnp.dot(hr.astype(BF16), cgT, preferred_element_type=F32)
            yT_s[rows, :] = y_intra + y_inter * e_row[r:r + 1, :]
            upd = jnp.dot((xr * w_row[r:r + 1, :]).astype(BF16), bg, preferred_element_type=F32)
            h_s[rows, :] = hr * d_row[r:r + 1, :] + upd

    y = yT_s[...].T
    y = y + dskip_ref[...] * xs
    y = y * _silu(z_ref[0])
    gs = D_SSM // N_SSM_GROUPS
    outs = []
    for g in range(N_SSM_GROUPS):
        yg = y[:, g * gs:(g + 1) * gs]
        ms = jnp.mean(yg * yg, axis=-1, keepdims=True)
        outs.append(yg * lax.rsqrt(ms + NORM_EPS) * nw_ref[:, g * gs:(g + 1) * gs])
    y_ref[0] = jnp.concatenate(outs, axis=-1).astype(BF16)

    @pl.when(c == pl.num_programs(1) - 1)
    def _():
        hout_ref[0] = h_s[...]


def _ssd(src, cols, dtT, hist8, h0, conv_w, conv_b, dt_bias, a_log, d_full, norm_w, *, valid):
    b, s, _ = src.shape
    L = SSD_L
    col_z, col_xs, col_b, col_c = cols
    kern = functools.partial(_ssd_kernel, valid=valid)
    const2 = lambda shape: pl.BlockSpec(shape, lambda bi, c: (0, 0))
    hp = N_SSM_HEADS * SSM_HEAD_DIM
    return pl.pallas_call(
        kern,
        grid=(b, s // L),
        in_specs=[
            pl.BlockSpec((1, L, D_SSM), lambda bi, c: (bi, c, col_z // D_SSM)),
            pl.BlockSpec((1, L, D_SSM), lambda bi, c: (bi, c, col_xs // D_SSM)),
            pl.BlockSpec((1, L, D_BC), lambda bi, c: (bi, c, col_b // D_BC)),
            pl.BlockSpec((1, L, D_BC), lambda bi, c: (bi, c, col_c // D_BC)),
            pl.BlockSpec((1, N_SSM_HEADS, L), lambda bi, c: (bi, 0, c)),
            pl.BlockSpec((1, SUBLANES, D_CONV), lambda bi, c: (bi, 0, 0)),
            pl.BlockSpec((1, hp, SSM_STATE), lambda bi, c: (bi, 0, 0)),
            const2((CONV_WIDTH, D_CONV)),
            const2((1, D_CONV)),
            const2((N_SSM_HEADS, 1)),
            const2((N_SSM_HEADS, 1)),
            const2((1, D_SSM)),
            const2((1, D_SSM)),
        ],
        out_specs=[
            pl.BlockSpec((1, L, D_SSM), lambda bi, c: (bi, c, 0)),
            pl.BlockSpec((1, hp, SSM_STATE), lambda bi, c: (bi, 0, 0)),
        ],
        out_shape=[
            jax.ShapeDtypeStruct((b, s, D_SSM), BF16),
            jax.ShapeDtypeStruct((b, hp, SSM_STATE), F32),
        ],
        scratch_shapes=[
            pltpu.VMEM((hp, SSM_STATE), F32),
            pltpu.VMEM((SUBLANES + L, D_CONV), F32),
            pltpu.VMEM((D_SSM, L), F32),
            pltpu.VMEM((D_SSM, L), F32),
        ],
        compiler_params=pltpu.CompilerParams(
            dimension_semantics=("parallel", "arbitrary"), vmem_limit_bytes=VMEM_LIMIT),
        name="ssd",
    )(src, src, src, src, dtT, hist8, h0, conv_w, conv_b, dt_bias, a_log, d_full, norm_w)


def _outproj_kernel(a_ref, y_ref, x_ref, w1_ref, w2_ref, fw_ref, o_ref):
    acc = (jnp.dot(a_ref[...], w1_ref[...], preferred_element_type=F32)
           + jnp.dot(y_ref[...], w2_ref[...], preferred_element_type=F32))
    h = x_ref[...] + acc
    ms = jnp.mean(h * h, axis=-1, keepdims=True)
    o_ref[...] = h * lax.rsqrt(ms + NORM_EPS) * fw_ref[...]


def _out_proj(att, y, x2d, w1, w2, final_w, *, tm):
    m = x2d.shape[0]
    const = pl.BlockSpec((D_ATT, D_MODEL), lambda i: (0, 0), pipeline_mode=pl.Buffered(1))
    return pl.pallas_call(
        _outproj_kernel,
        grid=(m // tm,),
        in_specs=[
            pl.BlockSpec((tm, D_ATT), lambda i: (i, 0)),
            pl.BlockSpec((tm, D_SSM), lambda i: (i, 0)),
            pl.BlockSpec((tm, D_MODEL), lambda i: (i, 0)),
            const, const,
            pl.BlockSpec((1, D_MODEL), lambda i: (0, 0)),
        ],
        out_specs=pl.BlockSpec((tm, D_MODEL), lambda i: (i, 0)),
        out_shape=jax.ShapeDtypeStruct((m, D_MODEL), F32),
        compiler_params=pltpu.CompilerParams(
            dimension_semantics=("parallel",), vmem_limit_bytes=VMEM_LIMIT),
        name="out_proj",
    )(att, y, x2d, w1, w2, final_w)


def _split_lanes_kernel(x_ref, o_ref):
    for c in range(o_ref.shape[1]):
        o_ref[:, c, :] = x_ref[:, c * LANES:(c + 1) * LANES]


def _split_lanes(x2d, *, tm):
    m, n = x2d.shape
    return pl.pallas_call(
        _split_lanes_kernel,
        grid=(m // tm,),
        in_specs=[pl.BlockSpec((tm, n), lambda i: (i, 0))],
        out_specs=pl.BlockSpec((tm, n // LANES, LANES), lambda i: (i, 0, 0)),
        out_shape=jax.ShapeDtypeStruct((m, n // LANES, LANES), x2d.dtype),
        compiler_params=pltpu.CompilerParams(
            dimension_semantics=("parallel",), vmem_limit_bytes=VMEM_LIMIT),
        name="split_lanes",
    )(x2d)


def _merge_cache_kernel(k_ref, v_ref, ko_ref, vo_ref):
    for c in range(k_ref.shape[1]):
        ko_ref[:, c * LANES:(c + 1) * LANES] = k_ref[:, c, :].astype(BF16)
    hw = v_ref.shape[2]
    for h in range(v_ref.shape[1]):
        vo_ref[:, h * hw:(h + 1) * hw] = v_ref[:, h, :].astype(BF16)


def _merge_cache(k3, v3, *, tm):
    rows = k3.shape[0]
    out = jax.ShapeDtypeStruct((rows, D_ATT), BF16)
    return pl.pallas_call(
        _merge_cache_kernel,
        grid=(rows // tm,),
        in_specs=[pl.BlockSpec((tm,) + k3.shape[1:], lambda i: (i, 0, 0)),
                  pl.BlockSpec((tm,) + v3.shape[1:], lambda i: (i, 0, 0))],
        out_specs=[pl.BlockSpec((tm, D_ATT), lambda i: (i, 0)), pl.BlockSpec((tm, D_ATT), lambda i: (i, 0))],
        out_shape=[out, out],
        compiler_params=pltpu.CompilerParams(
            dimension_semantics=("parallel",), vmem_limit_bytes=VMEM_LIMIT),
        name="merge_cache",
    )(k3, v3)


def _tiles(m):
    tm = min(m, 1024)
    assert m % tm == 0
    return tm


def _layer(h, k_past, v_past, conv_past, ssm_past, layer, rel_bias, norm_w, w_main, w_dt, lq1, lk1, lq2, lk2,
           subln_w, conv_w, conv_b, dt_bias, a_log, d_full, ssm_norm_w, w_out1, w_out2, out_norm_w):
    b, s, _ = h.shape
    m = b * s
    lam0 = _lambda_init(layer)
    x2d = h.reshape(m, D_MODEL)
    p32, k_new, v_new, qkv, dt_raw = _in_proj(x2d, norm_w, w_main, w_dt, tm=_tiles(m))
    k_new = _split_lanes(k_new, tm=min(m, 512)).reshape(b, s, N_ATT_HEADS, 2, ATT_HEAD_DIM)
    v_new = v_new.reshape(b, s, N_ATT_HEADS, 2 * ATT_HEAD_DIM)
    p32 = p32.reshape(b, s, D_P32)
    qkv = qkv.reshape(b, s, D_QKV)

    if k_past is None:
        tile = min(s, 512)
        vt = jnp.transpose(qkv[:, :, 2 * D_ATT:], (0, 2, 1))
        att = _prompt_attention(qkv, vt, p32, _prompt_bias_rows(rel_bias, tile), subln_w, lq1, lk1, lq2, lk2,
                                tile=tile, lam0=lam0)
    else:
        past = k_past.shape[1]
        kp, vp = _merge_cache(k_past.reshape(b * past, D_ATT // LANES, LANES),
                              v_past.reshape(b * past, N_ATT_HEADS, 2 * ATT_HEAD_DIM), tm=min(b * past, 512))
        att = _sample_attention(qkv, p32, kp.reshape(b, past, D_ATT), vp.reshape(b, past, D_ATT),
                                _sample_bias(rel_bias, past, s), subln_w, lq1, lk1, lq2, lk2, lam0=lam0)

    assert s >= CONV_WIDTH - 1
    conv_new = p32[:, s - (CONV_WIDTH - 1):, COL_XS:COL_XS + D_CONV]
    hist8 = jnp.pad(conv_past, ((0, 0), (SUBLANES - (CONV_WIDTH - 1), 0), (0, 0)))
    dtT = jnp.transpose(dt_raw[:, :N_SSM_HEADS].reshape(b, s, N_SSM_HEADS), (0, 2, 1))
    h0 = ssm_past.reshape(b, N_SSM_HEADS * SSM_HEAD_DIM, SSM_STATE)
    if s % SSD_L == 0:
        src, cols = p32, (COL_Z, COL_XS, COL_B, COL_C)
    else:
        assert s < SSD_L
        pad = SSD_L - s
        src = jnp.pad(p32[:, :, COL_Z:], ((0, 0), (0, pad), (0, 0)))
        cols = (0, D_SSM, 2 * D_SSM, 2 * D_SSM + D_BC)
        dtT = jnp.pad(dtT, ((0, 0), (0, 0), (0, pad)))
    y, ssm_new = _ssd(src, cols, dtT, hist8, h0, conv_w, conv_b, dt_bias, a_log, d_full, ssm_norm_w,
                      valid=min(s, SSD_L))
    y = y[:, :s].reshape(m, D_SSM)
    ssm_new = ssm_new.reshape(b, N_SSM_HEADS, SSM_HEAD_DIM, SSM_STATE)

    out = _out_proj(att.reshape(m, D_ATT), y, x2d, w_out1, w_out2, out_norm_w, tm=min(m, 512))
    return out.reshape(b, s, D_MODEL), k_new, v_new, conv_new, ssm_new


def kernel(x_prompt, x_sample, cache_k, cache_v, cache_conv, state_ssm, rel_bias, norm_w, w_in, lambda_q1,
           lambda_k1, lambda_q2, lambda_k2, subln_w, conv_w, conv_b, dt_bias, A_log, D_skip, ssm_norm_w, w_out,
           final_norm_w):
    depth = w_in.shape[0]
    assert depth == 1, "the final norm is fused into the (single) layer's output projection"
    bp = x_prompt.shape[0]
    l = 0
    col_q = 0
    col_dt = 4 * D_ATT + D_SSM + D_CONV
    w = w_in[l]
    w_main = jnp.transpose(w[:, col_q:col_dt].astype(BF16).reshape(D_MODEL, -1, IN_PROJ_TN), (1, 0, 2))
    w_dt = jnp.pad(w[:, col_dt:], ((0, 0), (0, LANES - N_SSM_HEADS))).astype(BF16)
    row = lambda t: t.reshape(1, -1).astype(F32)
    col = lambda t: t.reshape(-1, 1).astype(F32)
    params = (rel_bias, row(norm_w[l]), w_main, w_dt, row(lambda_q1[l]), row(lambda_k1[l]), row(lambda_q2[l]),
              row(lambda_k2[l]), row(subln_w[l]), conv_w[l].astype(F32), row(conv_b[l]), col(dt_bias[l]),
              col(A_log[l]), row(jnp.repeat(D_skip[l], SSM_HEAD_DIM)), row(ssm_norm_w[l]),
              w_out[l, :D_ATT].astype(BF16), w_out[l, D_ATT:].astype(BF16), row(final_norm_w))
    conv0 = jnp.zeros((bp, CONV_WIDTH - 1, D_CONV), x_prompt.dtype)
    ssm0 = jnp.zeros((bp, N_SSM_HEADS, SSM_HEAD_DIM, SSM_STATE), state_ssm.dtype)
    yp, k1, v1, c1, s1 = _layer(x_prompt, None, None, conv0, ssm0, l, *params)
    ys, k2, v2, c2, s2 = _layer(x_sample, cache_k[l], cache_v[l], cache_conv[l], state_ssm[l], l, *params)
    return (yp, ys, k1[None], v1[None], c1[None], s1[None], k2[None], v2[None], c2[None], s2[None])
```

```python
import functools
import math

import jax
import jax.numpy as jnp
from jax import lax
from jax.experimental import pallas as pl
from jax.experimental.pallas import tpu as pltpu

F32 = jnp.float32
BF16 = jnp.bfloat16

D_MODEL = 2048
CHUNK = 64
NORM_EPS = 1e-5
N_ATT_HEADS = 8
ATT_HEAD_DIM = 128
D_ATT = N_ATT_HEADS * 2 * ATT_HEAD_DIM
N_REL_BUCKETS = 32
REL_MAX_DIST = 128
D_SSM = 2048
SSM_HEAD_DIM = 64
N_SSM_HEADS = D_SSM // SSM_HEAD_DIM
N_SSM_GROUPS = 4
HEADS_PER_GROUP = N_SSM_HEADS // N_SSM_GROUPS
SSM_STATE = 128
CONV_WIDTH = 4
D_BC = N_SSM_GROUPS * SSM_STATE
D_CONV = D_SSM + 2 * D_BC
D_MIX = D_ATT + D_SSM
D_QKV = 3 * D_ATT
D_P32 = D_ATT + D_SSM + D_CONV
COL_G, COL_Z, COL_XS = 0, D_ATT, D_ATT + D_SSM
COL_B, COL_C = COL_XS + D_SSM, COL_XS + D_SSM + D_BC

LANES = 128
SUBLANES = 8
VMEM_LIMIT = 56 * 1024 * 1024
MASK_VALUE = -1e30
LOG2E = math.log2(math.e)
Q_SCALE = ATT_HEAD_DIM ** -0.5 * LOG2E
SSD_L = 128
IN_PROJ_TN = 1024


def _silu(x):
    return x * (1.0 / (1.0 + jnp.exp(-x)))


def _softplus(x):
    return jnp.maximum(x, 0.0) + jnp.log1p(jnp.exp(-jnp.abs(x)))


def _rms_to_scratch(x_ref, nw_ref, u_ref, slab):
    def body(r, carry):
        rows = pl.ds(pl.multiple_of(r * slab, slab), slab)
        x = x_ref[rows, :]
        ms = jnp.mean(x * x, axis=-1, keepdims=True)
        u = x * lax.rsqrt(ms + NORM_EPS) * nw_ref[...]
        u_ref[rows, :] = u.astype(BF16)
        return carry
    lax.fori_loop(0, x_ref.shape[0] // slab, body, 0)


def _inproj_qkv_kernel(x_ref, nw_ref, w_ref, k_ref, v_ref, qkv_ref, u_ref, *, n_q_tiles, slab):
    j = pl.program_id(1)

    @pl.when(j == 0)
    def _():
        _rms_to_scratch(x_ref, nw_ref, u_ref, slab)

    def project():
        return jnp.dot(u_ref[...], w_ref[0], preferred_element_type=F32)

    @pl.when(j < n_q_tiles)
    def _():
        qkv_ref[...] = (project() * Q_SCALE).astype(BF16)

    @pl.when(jnp.logical_and(j >= n_q_tiles, j < 2 * n_q_tiles))
    def _():
        res = project()
        for c in range(k_ref.shape[1]):
            k_ref[:, c, :] = res[:, c * LANES:(c + 1) * LANES]
        qkv_ref[...] = res.astype(BF16)

    @pl.when(j >= 2 * n_q_tiles)
    def _():
        res = project()
        v_ref[...] = res
        qkv_ref[...] = res.astype(BF16)


def _inproj_rest_kernel(x_ref, nw_ref, w_ref, wdt_ref, p_ref, dt_ref, u_ref, *, slab):
    @pl.when(pl.program_id(1) == 0)
    def _():
        _rms_to_scratch(x_ref, nw_ref, u_ref, slab)
        dt_ref[...] = jnp.dot(u_ref[...], wdt_ref[...], preferred_element_type=F32)

    p_ref[...] = jnp.dot(u_ref[...], w_ref[0], preferred_element_type=F32)


def _in_proj(x2d, norm_w, w_tiles, w_dt, *, tm):
    m = x2d.shape[0]
    n_tiles, _, tn = w_tiles.shape
    n_q_tiles = D_ATT // tn
    n_qkv_tiles = 3 * n_q_tiles
    slab = min(tm, 256)
    clamp = lambda j, lo: jnp.clip(j - lo * n_q_tiles, 0, n_q_tiles - 1)
    params = pltpu.CompilerParams(dimension_semantics=("parallel", "arbitrary"), vmem_limit_bytes=VMEM_LIMIT)
    x_spec = pl.BlockSpec((tm, D_MODEL), lambda i, j: (i, 0))
    nw_spec = pl.BlockSpec((1, D_MODEL), lambda i, j: (0, 0))
    k_new, v_new, qkv = pl.pallas_call(
        functools.partial(_inproj_qkv_kernel, n_q_tiles=n_q_tiles, slab=slab),
        grid=(m // tm, n_qkv_tiles),
        in_specs=[x_spec, nw_spec, pl.BlockSpec((1, D_MODEL, tn), lambda i, j: (j, 0, 0))],
        out_specs=[
            pl.BlockSpec((tm, tn // LANES, LANES), lambda i, j: (i, clamp(j, 1), 0)),
            pl.BlockSpec((tm, tn), lambda i, j: (i, clamp(j, 2))),
            pl.BlockSpec((tm, tn), lambda i, j: (i, j)),
        ],
        out_shape=[
            jax.ShapeDtypeStruct((m, D_ATT // LANES, LANES), F32),
            jax.ShapeDtypeStruct((m, D_ATT), F32),
            jax.ShapeDtypeStruct((m, D_QKV), BF16),
        ],
        scratch_shapes=[pltpu.VMEM((tm, D_MODEL), BF16)],
        compiler_params=params,
        name="in_proj_qkv",
    )(x2d, norm_w, w_tiles)
    p32, dt_raw = pl.pallas_call(
        functools.partial(_inproj_rest_kernel, slab=slab),
        grid=(m // tm, n_tiles - n_qkv_tiles),
        in_specs=[x_spec, nw_spec, pl.BlockSpec((1, D_MODEL, tn), lambda i, j: (n_qkv_tiles + j, 0, 0)),
                  pl.BlockSpec((D_MODEL, LANES), lambda i, j: (0, 0))],
        out_specs=[
            pl.BlockSpec((tm, tn), lambda i, j: (i, j)),
            pl.BlockSpec((tm, LANES), lambda i, j: (i, 0)),
        ],
        out_shape=[
            jax.ShapeDtypeStruct((m, D_P32), F32),
            jax.ShapeDtypeStruct((m, LANES), F32),
        ],
        scratch_shapes=[pltpu.VMEM((tm, D_MODEL), BF16)],
        compiler_params=params,
        name="in_proj_rest",
    )(x2d, norm_w, w_tiles, w_dt)
    return p32, k_new, v_new, qkv, dt_raw


def _rel_bucket(rel):
    half = N_REL_BUCKETS // 2
    max_exact = half // 2
    ret = jnp.where(rel > 0, half, 0)
    n = jnp.abs(rel)
    nf = jnp.maximum(n, 1).astype(F32)
    large = max_exact + (jnp.log(nf / max_exact) / math.log(REL_MAX_DIST / max_exact)
                         * (half - max_exact)).astype(jnp.int32)
    large = jnp.minimum(large, half - 1)
    return ret + jnp.where(n < max_exact, n, large)


def _lambda_init(layer):
    return 0.8 - 0.6 * math.exp(-0.3 * layer)


def _lam_from_refs(lq1, lk1, lq2, lk2, lam0):
    return (jnp.exp(jnp.sum(lq1[...] * lk1[...], axis=-1, keepdims=True))
            - jnp.exp(jnp.sum(lq2[...] * lk2[...], axis=-1, keepdims=True)) + lam0)


def _attn_epilogue(a1, a2, lam, g, subln_w, lam0):
    o = a1 - lam * a2
    ms = jnp.mean(o * o, axis=-1, keepdims=True)
    o = o * lax.rsqrt(ms + NORM_EPS) * subln_w
    o = o * (1.0 - lam0)
    return o * _silu(g)


def _prompt_attn_kernel(q_ref, k_ref, vt_ref, g_ref, brow_ref, sw_ref, lq1, lk1, lq2, lk2,
                        o_ref, m_ref, l_ref, acc_ref, qt_ref, on_ref, bias_ref, *, tile, qblk, lam0):
    qi = pl.program_id(2)
    d = ATT_HEAD_DIM
    units = [(c, mi) for c in range(tile // qblk) for mi in range(2)]
    qt_ref[...] = q_ref[0].astype(F32).T.astype(BF16)

    @pl.when(qi == 0)
    def _():
        for t in range(2):
            rows = jnp.broadcast_to(brow_ref[t, 0], (tile, 2 * tile))
            bias_ref[t] = pltpu.roll(rows, 0, 1, stride=1, stride_axis=0)[:, :tile]
        kc = lax.broadcasted_iota(jnp.int32, (tile, tile), 0) // CHUNK
        qc = lax.broadcasted_iota(jnp.int32, (tile, tile), 1) // CHUNK
        bias_ref[1] = jnp.where(kc <= qc, bias_ref[1], MASK_VALUE)

    def reset(m_init):
        m_ref[...] = jnp.full(m_ref.shape, m_init, F32)
        l_ref[...] = jnp.zeros(l_ref.shape, F32)
        acc_ref[...] = jnp.zeros(acc_ref.shape, F32)

    def kv_span(j, n, bias_idx, lagged, diagonal=False):
        keys = pl.ds(pl.multiple_of(j * tile, tile), n * tile)
        k = k_ref[0, keys, :]
        vt_all = vt_ref[0, :, keys]
        n_keys = [(c + 1) * qblk if diagonal else n * tile for c, _ in units]
        ss = [jnp.dot(k[:nk, mi * d:(mi + 1) * d], qt_ref[mi * d:(mi + 1) * d, c * qblk:(c + 1) * qblk],
                      preferred_element_type=F32) for (c, mi), nk in zip(units, n_keys)]
        for (c, mi), nk, s in zip(units, n_keys, ss):
            cols = slice(c * qblk, (c + 1) * qblk)
            vt = vt_all[:, :nk]
            if bias_idx is not None:
                s = s + bias_ref[bias_idx, :nk, cols]
            m_old = m_ref[mi, :, cols]
            if lagged:
                p = jnp.exp2(s - m_old)
                l_new = l_ref[mi, :, cols] + jnp.sum(p, axis=0, keepdims=True)
                acc_new = acc_ref[mi, :, cols] + jnp.dot(vt, p.astype(BF16), preferred_element_type=F32)
                m_new = jnp.maximum(m_old, jnp.max(s, axis=0, keepdims=True))
                alpha = jnp.exp2(m_old - m_new)
                l_ref[mi, :, cols] = l_new * alpha
                acc_ref[mi, :, cols] = acc_new * alpha
            else:
                m_new = jnp.maximum(m_old, jnp.max(s, axis=0, keepdims=True))
                alpha = jnp.exp2(m_old - m_new)
                p = jnp.exp2(s - m_new)
                l_ref[mi, :, cols] = alpha * l_ref[mi, :, cols] + jnp.sum(p, axis=0, keepdims=True)
                pv = jnp.dot(vt, p.astype(BF16), preferred_element_type=F32)
                acc_ref[mi, :, cols] = alpha * acc_ref[mi, :, cols] + pv
            m_ref[mi, :, cols] = m_new

    n_far = jnp.maximum(qi - 1, 0)
    n_quads = n_far // 4
    n_pairs = n_far // 2

    def all_tiles(lagged):
        reset(0.0 if lagged else MASK_VALUE)

        kv_span(qi, 1, 1, lagged, diagonal=True)

        def off_body(j, carry):
            kv_span(j, 1, 0, lagged)
            return carry
        lax.fori_loop(n_far, qi, off_body, 0)

        def far_body(width):
            def body(i, carry):
                kv_span(width * i, width, None, lagged)
                return carry
            return body
        lax.fori_loop(0, n_quads, far_body(4), 0)
        lax.fori_loop(2 * n_quads, n_pairs, far_body(2), 0)
        lax.fori_loop(2 * n_pairs, n_far, far_body(1), 0)

    lam = _lam_from_refs(lq1, lk1, lq2, lk2, lam0)

    def combine():
        o = acc_ref[0] * (1.0 / l_ref[0]) - lam * (acc_ref[1] * (1.0 / l_ref[1]))
        ms = jnp.mean(o * o, axis=0, keepdims=True)
        on_ref[...] = o * lax.rsqrt(ms + NORM_EPS)
        return ms

    all_tiles(True)
    ms = combine()
    finite = jnp.logical_and(jnp.all(jnp.isfinite(l_ref[...])), jnp.all(jnp.isfinite(ms)))

    @pl.when(jnp.logical_not(finite))
    def _():
        all_tiles(False)
        combine()

    o_ref[0] = (on_ref[...].T * sw_ref[...] * (1.0 - lam0) * _silu(g_ref[0])).astype(BF16)


def _toeplitz(fn, rows, cols):
    period = rows + cols
    d = jnp.arange(period, dtype=jnp.int32)
    g = jnp.moveaxis(fn(jnp.where(d < cols, d, d - period)), 0, -1)
    x = jnp.tile(g, rows)[..., :rows * (period - 1)].reshape(g.shape[:-1] + (rows, period - 1))
    return x[..., :cols]


def _prompt_bias_rows(rel_table, tile):
    assert tile >= REL_MAX_DIST and tile % CHUNK == 0
    table = rel_table.astype(F32) * LOG2E
    far = table[_rel_bucket(jnp.full((1,), -(tile + 1), jnp.int32))]
    d = jnp.arange(2 * tile, dtype=jnp.int32)
    dd = jnp.where(d < tile, d, d - 2 * tile)
    diag = table[_rel_bucket(-dd)] - far
    off = table[_rel_bucket(-dd - tile)] - far
    return jnp.transpose(jnp.stack([off, diag]), (0, 2, 1))[:, :, None, :]


def _prompt_attention(qkv, vt, p32, bias_rows, subln_w, lq1, lk1, lq2, lk2, *, tile, lam0):
    b, s, _ = qkv.shape
    hw = 2 * ATT_HEAD_DIM
    nh = N_ATT_HEADS
    kern = functools.partial(_prompt_attn_kernel, tile=tile, qblk=min(tile, 256), lam0=lam0)
    vec = pl.BlockSpec((1, ATT_HEAD_DIM), lambda bi, h, qi: (0, 0))
    return pl.pallas_call(
        kern,
        grid=(b, nh, s // tile),
        in_specs=[
            pl.BlockSpec((1, tile, hw), lambda bi, h, qi: (bi, qi, h)),
            pl.BlockSpec((1, s, hw), lambda bi, h, qi: (bi, 0, nh + h)),
            pl.BlockSpec((1, hw, s), lambda bi, h, qi: (bi, h, 0)),
            pl.BlockSpec((1, tile, hw), lambda bi, h, qi: (bi, qi, COL_G // hw + h)),
            pl.BlockSpec((2, 1, 1, 2 * tile), lambda bi, h, qi: (0, h, 0, 0)),
            pl.BlockSpec((1, hw), lambda bi, h, qi: (0, 0)),
            vec, vec, vec, vec,
        ],
        out_specs=pl.BlockSpec((1, tile, hw), lambda bi, h, qi: (bi, qi, h)),
        out_shape=jax.ShapeDtypeStruct((b, s, D_ATT), BF16),
        scratch_shapes=[
            pltpu.VMEM((2, 1, tile), F32),
            pltpu.VMEM((2, 1, tile), F32),
            pltpu.VMEM((2, hw, tile), F32),
            pltpu.VMEM((hw, tile), BF16),
            pltpu.VMEM((hw, tile), F32),
            pltpu.VMEM((2, tile, tile), F32),
        ],
        compiler_params=pltpu.CompilerParams(
            dimension_semantics=("parallel", "parallel", "arbitrary"), vmem_limit_bytes=VMEM_LIMIT),
        name="prompt_attention",
    )(qkv, qkv, vt, p32, bias_rows, subln_w, lq1, lk1, lq2, lk2)


def _sample_attn_kernel(q_ref, kn_ref, vn_ref, kp_ref, vp_ref, g_ref, bp_ref, bn_ref, sw_ref,
                        lq1, lk1, lq2, lk2, o_ref, *, lam0):
    d = ATT_HEAD_DIM
    q = q_ref[0]
    kn = kn_ref[0]
    vn = vn_ref[0]
    kp = kp_ref[0]
    vp = vp_ref[0]
    nt = (((1,), (1,)), ((), ()))
    outs = []
    for mi in range(2):
        qm = q[:, mi * d:(mi + 1) * d]
        sp = lax.dot_general(qm, kp[:, mi * d:(mi + 1) * d], nt, preferred_element_type=F32)
        sn = lax.dot_general(qm, kn[:, mi * d:(mi + 1) * d], nt, preferred_element_type=F32)
        sp = sp + bp_ref[0]
        sn = sn + bn_ref[0]
        m = jnp.maximum(jnp.max(sp, axis=-1, keepdims=True), jnp.max(sn, axis=-1, keepdims=True))
        pp = jnp.exp2(sp - m)
        pn = jnp.exp2(sn - m)
        l = jnp.sum(pp, axis=-1, keepdims=True) + jnp.sum(pn, axis=-1, keepdims=True)
        acc = (jnp.dot(pp.astype(BF16), vp, preferred_element_type=F32)
               + jnp.dot(pn.astype(BF16), vn, preferred_element_type=F32))
        outs.append(acc / l)
    lam = _lam_from_refs(lq1, lk1, lq2, lk2, lam0)
    o_ref[0] = _attn_epilogue(outs[0], outs[1], lam, g_ref[0], sw_ref[...], lam0).astype(BF16)


def _sample_bias(rel_table, past_len, s):
    qpos = past_len + jnp.arange(s, dtype=jnp.int32)[:, None]
    kpos = jnp.arange(past_len + s, dtype=jnp.int32)[None, :]
    table = rel_table.astype(F32) * LOG2E
    bias = _toeplitz(lambda dd: table[_rel_bucket(dd - past_len)], s, past_len + s)
    visible = (kpos // CHUNK) <= (qpos // CHUNK)
    return jnp.where(visible[None], bias, MASK_VALUE)


def _sample_attention(qkv, p32, k_past, v_past, bias, subln_w, lq1, lk1, lq2, lk2, *, lam0):
    b, s, _ = qkv.shape
    past = k_past.shape[1]
    hw = 2 * ATT_HEAD_DIM
    nh = N_ATT_HEADS
    bias_p = bias[:, :, :past]
    bias_n = bias[:, :, past:]
    kern = functools.partial(_sample_attn_kernel, lam0=lam0)
    vec = pl.BlockSpec((1, ATT_HEAD_DIM), lambda bi, h: (0, 0))
    return pl.pallas_call(
        kern,
        grid=(b, nh),
        in_specs=[
            pl.BlockSpec((1, s, hw), lambda bi, h: (bi, 0, h)),
            pl.BlockSpec((1, s, hw), lambda bi, h: (bi, 0, nh + h)),
            pl.BlockSpec((1, s, hw), lambda bi, h: (bi, 0, 2 * nh + h)),
            pl.BlockSpec((1, past, hw), lambda bi, h: (bi, 0, h)),
            pl.BlockSpec((1, past, hw), lambda bi, h: (bi, 0, h)),
            pl.BlockSpec((1, s, hw), lambda bi, h: (bi, 0, COL_G // hw + h)),
            pl.BlockSpec((1, s, past), lambda bi, h: (h, 0, 0)),
            pl.BlockSpec((1, s, s), lambda bi, h: (h, 0, 0)),
            pl.BlockSpec((1, hw), lambda bi, h: (0, 0)),
            vec, vec, vec, vec,
        ],
        out_specs=pl.BlockSpec((1, s, hw), lambda bi, h: (bi, 0, h)),
        out_shape=jax.ShapeDtypeStruct((b, s, D_ATT), BF16),
        compiler_params=pltpu.CompilerParams(
            dimension_semantics=("parallel", "parallel"), vmem_limit_bytes=VMEM_LIMIT),
        name="sample_attention",
    )(qkv, qkv, qkv, k_past, v_past, p32, bias_p, bias_n, subln_w, lq1, lk1, lq2, lk2)


def _ssd_kernel(z_ref, xs_ref, b_ref, c_ref, dtT_ref, hist_ref, h0_ref, cw_ref, cb_ref, dtb_ref,
                alog_ref, dskip_ref, nw_ref, y_ref, hout_ref, h_s, xpad_s, xT_s, yT_s, *, valid):
    c = pl.program_id(1)
    L = xs_ref.shape[1]
    P, N, R = SSM_HEAD_DIM, SSM_STATE, HEADS_PER_GROUP

    @pl.when(c == 0)
    def _():
        h_s[...] = h0_ref[0]
        xpad_s[:SUBLANES, :] = hist_ref[0]

    def conv_silu(x_ref, lo, hi):
        xpad_s[SUBLANES:, lo:hi] = x_ref[0]
        acc = cb_ref[:, lo:hi]
        for k in range(CONV_WIDTH):
            acc = acc + (xpad_s[SUBLANES - k:SUBLANES - k + L, lo:hi]
                         * cw_ref[CONV_WIDTH - 1 - k:CONV_WIDTH - k, lo:hi])
        return _silu(acc)

    xs = conv_silu(xs_ref, 0, D_SSM)
    bm = conv_silu(b_ref, D_SSM, D_SSM + D_BC)
    cm = conv_silu(c_ref, D_SSM + D_BC, D_CONV)
    xpad_s[:SUBLANES, :] = xpad_s[L:, :]

    dt = _softplus(dtT_ref[0] + dtb_ref[...])
    if valid < L:
        dt = jnp.where(lax.broadcasted_iota(jnp.int32, dt.shape, 1) < valid, dt, 0.0)
    a = dt * (-jnp.exp(alog_ref[...]))
    s_idx = lax.broadcasted_iota(jnp.int32, (L, L), 0)
    t_idx = lax.broadcasted_iota(jnp.int32, (L, L), 1)
    causal = s_idx <= t_idx
    upper = jnp.where(causal, 1.0, 0.0).astype(BF16)
    a1 = a.astype(BF16)
    r1 = a - a1.astype(F32)
    a2 = r1.astype(BF16)
    a3 = (r1 - a2.astype(F32)).astype(BF16)
    acs = (jnp.dot(a1, upper, preferred_element_type=F32)
           + jnp.dot(a2, upper, preferred_element_type=F32)
           + jnp.dot(a3, upper, preferred_element_type=F32))
    tot = acs[:, L - 1:L]
    e_row = jnp.exp(acs)
    w_row = dt * jnp.exp(tot - acs)
    d_row = jnp.broadcast_to(jnp.exp(tot), (N_SSM_HEADS, N))
    acs_col = jnp.concatenate([acs, jnp.zeros((L - N_SSM_HEADS, L), F32)], axis=0).T

    xT_s[...] = xs.T

    for g in range(N_SSM_GROUPS):
        bg = bm[:, g * N:(g + 1) * N].astype(BF16)
        cgT = cm[:, g * N:(g + 1) * N].T.astype(BF16)
        cbT = jnp.dot(bg, cgT, preferred_element_type=F32)
        for r8 in range(R):
            r = g * R + r8
            rows = slice(r * P, (r + 1) * P)
            seg = acs[r:r + 1, :] - acs_col[:, r:r + 1]
            decay = jnp.exp(jnp.where(causal, seg, MASK_VALUE))
            mT = (cbT * decay).astype(BF16)
            xr = xT_s[rows, :]
            hr = h_s[rows, :]
            y_intra = jnp.dot((xr * dt[r:r + 1, :]).astype(BF16), mT, preferred_element_type=F32)
            y_inter = jnp.dot(hr.astype(BF16), cgT, preferred_element_type=F32)
            yT_s[rows, :] = y_intra + y_inter * e_row[r:r + 1, :]
            upd = jnp.dot((xr * w_row[r:r + 1, :]).astype(BF16), bg, preferred_element_type=F32)
            h_s[rows, :] = hr * d_row[r:r + 1, :] + upd

    y = yT_s[...].T
    y = y + dskip_ref[...] * xs
    y = y * _silu(z_ref[0])
    gs = D_SSM // N_SSM_GROUPS
    outs = []
    for g in range(N_SSM_GROUPS):
        yg = y[:, g * gs:(g + 1) * gs]
        ms = jnp.mean(yg * yg, axis=-1, keepdims=True)
        outs.append(yg * lax.rsqrt(ms + NORM_EPS) * nw_ref[:, g * gs:(g + 1) * gs])
    y_ref[0] = jnp.concatenate(outs, axis=-1).astype(BF16)

    @pl.when(c == pl.num_programs(1) - 1)
    def _():
        hout_ref[0] = h_s[...]


def _ssd(src, cols, dtT, hist8, h0, conv_w, conv_b, dt_bias, a_log, d_full, norm_w, *, valid):
    b, s, _ = src.shape
    L = SSD_L
    col_z, col_xs, col_b, col_c = cols
    kern = functools.partial(_ssd_kernel, valid=valid)
    const2 = lambda shape: pl.BlockSpec(shape, lambda bi, c: (0, 0))
    hp = N_SSM_HEADS * SSM_HEAD_DIM
    return pl.pallas_call(
        kern,
        grid=(b, s // L),
        in_specs=[
            pl.BlockSpec((1, L, D_SSM), lambda bi, c: (bi, c, col_z // D_SSM)),
            pl.BlockSpec((1, L, D_SSM), lambda bi, c: (bi, c, col_xs // D_SSM)),
            pl.BlockSpec((1, L, D_BC), lambda bi, c: (bi, c, col_b // D_BC)),
            pl.BlockSpec((1, L, D_BC), lambda bi, c: (bi, c, col_c // D_BC)),
            pl.BlockSpec((1, N_SSM_HEADS, L), lambda bi, c: (bi, 0, c)),
            pl.BlockSpec((1, SUBLANES, D_CONV), lambda bi, c: (bi, 0, 0)),
            pl.BlockSpec((1, hp, SSM_STATE), lambda bi, c: (bi, 0, 0)),
            const2((CONV_WIDTH, D_CONV)),
            const2((1, D_CONV)),
            const2((N_SSM_HEADS, 1)),
            const2((N_SSM_HEADS, 1)),
            const2((1, D_SSM)),
            const2((1, D_SSM)),
        ],
        out_specs=[
            pl.BlockSpec((1, L, D_SSM), lambda bi, c: (bi, c, 0)),
            pl.BlockSpec((1, hp, SSM_STATE), lambda bi, c: (bi, 0, 0)),
        ],
        out_shape=[
            jax.ShapeDtypeStruct((b, s, D_SSM), BF16),
            jax.ShapeDtypeStruct((b, hp, SSM_STATE), F32),
        ],
        scratch_shapes=[
            pltpu.VMEM((hp, SSM_STATE), F32),
            pltpu.VMEM((SUBLANES + L, D_CONV), F32),
            pltpu.VMEM((D_SSM, L), F32),
            pltpu.VMEM((D_SSM, L), F32),
        ],
        compiler_params=pltpu.CompilerParams(
            dimension_semantics=("parallel", "arbitrary"), vmem_limit_bytes=VMEM_LIMIT),
        name="ssd",
    )(src, src, src, src, dtT, hist8, h0, conv_w, conv_b, dt_bias, a_log, d_full, norm_w)


def _outproj_kernel(a_ref, y_ref, x_ref, w1_ref, w2_ref, fw_ref, o_ref):
    acc = (jnp.dot(a_ref[...], w1_ref[...], preferred_element_type=F32)
           + jnp.dot(y_ref[...], w2_ref[...], preferred_element_type=F32))
    h = x_ref[...] + acc
    ms = jnp.mean(h * h, axis=-1, keepdims=True)
    o_ref[...] = h * lax.rsqrt(ms + NORM_EPS) * fw_ref[...]


def _out_proj(att, y, x2d, w1, w2, final_w, *, tm):
    m = x2d.shape[0]
    const = pl.BlockSpec((D_ATT, D_MODEL), lambda i: (0, 0), pipeline_mode=pl.Buffered(1))
    return pl.pallas_call(
        _outproj_kernel,
        grid=(m // tm,),
        in_specs=[
            pl.BlockSpec((tm, D_ATT), lambda i: (i, 0)),
            pl.BlockSpec((tm, D_SSM), lambda i: (i, 0)),
            pl.BlockSpec((tm, D_MODEL), lambda i: (i, 0)),
            const, const,
            pl.BlockSpec((1, D_MODEL), lambda i: (0, 0)),
        ],
        out_specs=pl.BlockSpec((tm, D_MODEL), lambda i: (i, 0)),
        out_shape=jax.ShapeDtypeStruct((m, D_MODEL), F32),
        compiler_params=pltpu.CompilerParams(
            dimension_semantics=("parallel",), vmem_limit_bytes=VMEM_LIMIT),
        name="out_proj",
    )(att, y, x2d, w1, w2, final_w)


def _merge_cache_kernel(k_ref, v_ref, ko_ref, vo_ref):
    for c in range(k_ref.shape[1]):
        ko_ref[:, c * LANES:(c + 1) * LANES] = k_ref[:, c, :].astype(BF16)
    hw = v_ref.shape[2]
    for h in range(v_ref.shape[1]):
        vo_ref[:, h * hw:(h + 1) * hw] = v_ref[:, h, :].astype(BF16)


def _merge_cache(k3, v3, *, tm):
    rows = k3.shape[0]
    out = jax.ShapeDtypeStruct((rows, D_ATT), BF16)
    return pl.pallas_call(
        _merge_cache_kernel,
        grid=(rows // tm,),
        in_specs=[pl.BlockSpec((tm,) + k3.shape[1:], lambda i: (i, 0, 0)),
                  pl.BlockSpec((tm,) + v3.shape[1:], lambda i: (i, 0, 0))],
        out_specs=[pl.BlockSpec((tm, D_ATT), lambda i: (i, 0)), pl.BlockSpec((tm, D_ATT), lambda i: (i, 0))],
        out_shape=[out, out],
        compiler_params=pltpu.CompilerParams(
            dimension_semantics=("parallel",), vmem_limit_bytes=VMEM_LIMIT),
        name="merge_cache",
    )(k3, v3)


def _tiles(m):
    tm = min(m, 1024)
    assert m % tm == 0
    return tm


def _layer(h, k_past, v_past, conv_past, ssm_past, layer, rel_bias, norm_w, w_main, w_dt, lq1, lk1, lq2, lk2,
           subln_w, conv_w, conv_b, dt_bias, a_log, d_full, ssm_norm_w, w_out1, w_out2, out_norm_w):
    b, s, _ = h.shape
    m = b * s
    lam0 = _lambda_init(layer)
    x2d = h.reshape(m, D_MODEL)
    p32, k_new, v_new, qkv, dt_raw = _in_proj(x2d, norm_w, w_main, w_dt, tm=_tiles(m))
    k_new = k_new.reshape(b, s, N_ATT_HEADS, 2, ATT_HEAD_DIM)
    v_new = v_new.reshape(b, s, N_ATT_HEADS, 2 * ATT_HEAD_DIM)
    p32 = p32.reshape(b, s, D_P32)
    qkv = qkv.reshape(b, s, D_QKV)

    if k_past is None:
        tile = min(s, 512)
        vt = jnp.transpose(qkv[:, :, 2 * D_ATT:], (0, 2, 1))
        att = _prompt_attention(qkv, vt, p32, _prompt_bias_rows(rel_bias, tile), subln_w, lq1, lk1, lq2, lk2,
                                tile=tile, lam0=lam0)
    else:
        past = k_past.shape[1]
        kp, vp = _merge_cache(k_past.reshape(b * past, D_ATT // LANES, LANES),
                              v_past.reshape(b * past, N_ATT_HEADS, 2 * ATT_HEAD_DIM), tm=min(b * past, 512))
        att = _sample_attention(qkv, p32, kp.reshape(b, past, D_ATT), vp.reshape(b, past, D_ATT),
                                _sample_bias(rel_bias, past, s), subln_w, lq1, lk1, lq2, lk2, lam0=lam0)

    assert s >= CONV_WIDTH - 1
    conv_new = p32[:, s - (CONV_WIDTH - 1):, COL_XS:COL_XS + D_CONV]
    hist8 = jnp.pad(conv_past, ((0, 0), (SUBLANES - (CONV_WIDTH - 1), 0), (0, 0)))
    dtT = jnp.transpose(dt_raw[:, :N_SSM_HEADS].reshape(b, s, N_SSM_HEADS), (0, 2, 1))
    h0 = ssm_past.reshape(b, N_SSM_HEADS * SSM_HEAD_DIM, SSM_STATE)
    if s % SSD_L == 0:
        src, cols = p32, (COL_Z, COL_XS, COL_B, COL_C)
    else:
        assert s < SSD_L
        pad = SSD_L - s
        src = jnp.pad(p32[:, :, COL_Z:], ((0, 0), (0, pad), (0, 0)))
        cols = (0, D_SSM, 2 * D_SSM, 2 * D_SSM + D_BC)
        dtT = jnp.pad(dtT, ((0, 0), (0, 0), (0, pad)))
    y, ssm_new = _ssd(src, cols, dtT, hist8, h0, conv_w, conv_b, dt_bias, a_log, d_full, ssm_norm_w,
                      valid=min(s, SSD_L))
    y = y[:, :s].reshape(m, D_SSM)
    ssm_new = ssm_new.reshape(b, N_SSM_HEADS, SSM_HEAD_DIM, SSM_STATE)

    out = _out_proj(att.reshape(m, D_ATT), y, x2d, w_out1, w_out2, out_norm_w, tm=min(m, 512))
    return out.reshape(b, s, D_MODEL), k_new, v_new, conv_new, ssm_new


def kernel(x_prompt, x_sample, cache_k, cache_v, cache_conv, state_ssm, rel_bias, norm_w, w_in, lambda_q1,
           lambda_k1, lambda_q2, lambda_k2, subln_w, conv_w, conv_b, dt_bias, A_log, D_skip, ssm_norm_w, w_out,
           final_norm_w):
    depth = w_in.shape[0]
    assert depth == 1, "the final norm is fused into the (single) layer's output projection"
    bp = x_prompt.shape[0]
    l = 0
    col_q = 0
    col_dt = 4 * D_ATT + D_SSM + D_CONV
    w = w_in[l]
    w_main = jnp.transpose(w[:, col_q:col_dt].astype(BF16).reshape(D_MODEL, -1, IN_PROJ_TN), (1, 0, 2))
    w_dt = jnp.pad(w[:, col_dt:], ((0, 0), (0, LANES - N_SSM_HEADS))).astype(BF16)
    row = lambda t: t.reshape(1, -1).astype(F32)
    col = lambda t: t.reshape(-1, 1).astype(F32)
    params = (rel_bias, row(norm_w[l]), w_main, w_dt, row(lambda_q1[l]), row(lambda_k1[l]), row(lambda_q2[l]),
              row(lambda_k2[l]), row(subln_w[l]), conv_w[l].astype(F32), row(conv_b[l]), col(dt_bias[l]),
              col(A_log[l]), row(jnp.repeat(D_skip[l], SSM_HEAD_DIM)), row(ssm_norm_w[l]),
              w_out[l, :D_ATT].astype(BF16), w_out[l, D_ATT:].astype(BF16), row(final_norm_w))
    conv0 = jnp.zeros((bp, CONV_WIDTH - 1, D_CONV), x_prompt.dtype)
    ssm0 = jnp.zeros((bp, N_SSM_HEADS, SSM_HEAD_DIM, SSM_STATE), state_ssm.dtype)
    yp, k1, v1, c1, s1 = _layer(x_prompt, None, None, conv0, ssm0, l, *params)
    ys, k2, v2, c2, s2 = _layer(x_sample, cache_k[l], cache_v[l], cache_conv[l], state_ssm[l], l, *params)
    return (yp, ys, k1[None], v1[None], c1[None], s1[None], k2[None], v2[None], c2[None], s2[None])
```

```python
import functools
import math

import jax
import jax.numpy as jnp
from jax import lax
from jax.experimental import pallas as pl
from jax.experimental.pallas import tpu as pltpu

F32 = jnp.float32
BF16 = jnp.bfloat16

D_MODEL = 2048
CHUNK = 64
NORM_EPS = 1e-5
N_ATT_HEADS = 8
ATT_HEAD_DIM = 128
D_ATT = N_ATT_HEADS * 2 * ATT_HEAD_DIM
N_REL_BUCKETS = 32
REL_MAX_DIST = 128
D_SSM = 2048
SSM_HEAD_DIM = 64
N_SSM_HEADS = D_SSM // SSM_HEAD_DIM
N_SSM_GROUPS = 4
HEADS_PER_GROUP = N_SSM_HEADS // N_SSM_GROUPS
SSM_STATE = 128
CONV_WIDTH = 4
D_BC = N_SSM_GROUPS * SSM_STATE
D_CONV = D_SSM + 2 * D_BC
D_MIX = D_ATT + D_SSM
D_QKV = 3 * D_ATT
D_P32 = D_ATT + D_SSM + D_CONV
COL_G, COL_Z, COL_XS = 0, D_ATT, D_ATT + D_SSM
COL_B, COL_C = COL_XS + D_SSM, COL_XS + D_SSM + D_BC

LANES = 128
SUBLANES = 8
VMEM_LIMIT = 56 * 1024 * 1024
MASK_VALUE = -1e30
LOG2E = math.log2(math.e)
Q_SCALE = ATT_HEAD_DIM ** -0.5 * LOG2E
SSD_L = 128
IN_PROJ_TN = 1024


def _silu(x):
    return x * (1.0 / (1.0 + jnp.exp(-x)))


def _softplus(x):
    return jnp.maximum(x, 0.0) + jnp.log1p(jnp.exp(-jnp.abs(x)))


def _rms_to_scratch(x_ref, nw_ref, u_ref, slab):
    def body(r, carry):
        rows = pl.ds(pl.multiple_of(r * slab, slab), slab)
        x = x_ref[rows, :]
        ms = jnp.mean(x * x, axis=-1, keepdims=True)
        u = x * lax.rsqrt(ms + NORM_EPS) * nw_ref[...]
        u_ref[rows, :] = u.astype(BF16)
        return carry
    lax.fori_loop(0, x_ref.shape[0] // slab, body, 0)


def _inproj_qkv_kernel(x_ref, nw_ref, w_ref, k_ref, v_ref, qkv_ref, u_ref, *, n_q_tiles, slab):
    j = pl.program_id(1)

    @pl.when(j == 0)
    def _():
        _rms_to_scratch(x_ref, nw_ref, u_ref, slab)

    def project():
        return jnp.dot(u_ref[...], w_ref[0], preferred_element_type=F32)

    @pl.when(j < n_q_tiles)
    def _():
        qkv_ref[...] = (project() * Q_SCALE).astype(BF16)

    @pl.when(jnp.logical_and(j >= n_q_tiles, j < 2 * n_q_tiles))
    def _():
        res = project()
        k_ref[...] = pltpu.einshape("m(cd)->mcd", res, d=LANES)
        qkv_ref[...] = res.astype(BF16)

    @pl.when(j >= 2 * n_q_tiles)
    def _():
        res = project()
        v_ref[...] = res
        qkv_ref[...] = res.astype(BF16)


def _inproj_rest_kernel(x_ref, nw_ref, w_ref, wdt_ref, p_ref, dt_ref, u_ref, *, slab):
    @pl.when(pl.program_id(1) == 0)
    def _():
        _rms_to_scratch(x_ref, nw_ref, u_ref, slab)
        dt_ref[...] = jnp.dot(u_ref[...], wdt_ref[...], preferred_element_type=F32)

    p_ref[...] = jnp.dot(u_ref[...], w_ref[0], preferred_element_type=F32)


def _in_proj(x2d, norm_w, w_tiles, w_dt, *, tm):
    m = x2d.shape[0]
    n_tiles, _, tn = w_tiles.shape
    n_q_tiles = D_ATT // tn
    n_qkv_tiles = 3 * n_q_tiles
    slab = min(tm, 256)
    clamp = lambda j, lo: jnp.clip(j - lo * n_q_tiles, 0, n_q_tiles - 1)
    params = pltpu.CompilerParams(dimension_semantics=("parallel", "arbitrary"), vmem_limit_bytes=VMEM_LIMIT)
    x_spec = pl.BlockSpec((tm, D_MODEL), lambda i, j: (i, 0))
    nw_spec = pl.BlockSpec((1, D_MODEL), lambda i, j: (0, 0))
    k_new, v_new, qkv = pl.pallas_call(
        functools.partial(_inproj_qkv_kernel, n_q_tiles=n_q_tiles, slab=slab),
        grid=(m // tm, n_qkv_tiles),
        in_specs=[x_spec, nw_spec, pl.BlockSpec((1, D_MODEL, tn), lambda i, j: (j, 0, 0))],
        out_specs=[
            pl.BlockSpec((tm, tn // LANES, LANES), lambda i, j: (i, clamp(j, 1), 0)),
            pl.BlockSpec((tm, tn), lambda i, j: (i, clamp(j, 2))),
            pl.BlockSpec((tm, tn), lambda i, j: (i, j)),
        ],
        out_shape=[
            jax.ShapeDtypeStruct((m, D_ATT // LANES, LANES), F32),
            jax.ShapeDtypeStruct((m, D_ATT), F32),
            jax.ShapeDtypeStruct((m, D_QKV), BF16),
        ],
        scratch_shapes=[pltpu.VMEM((tm, D_MODEL), BF16)],
        compiler_params=params,
        name="in_proj_qkv",
    )(x2d, norm_w, w_tiles)
    p32, dt_raw = pl.pallas_call(
        functools.partial(_inproj_rest_kernel, slab=slab),
        grid=(m // tm, n_tiles - n_qkv_tiles),
        in_specs=[x_spec, nw_spec, pl.BlockSpec((1, D_MODEL, tn), lambda i, j: (n_qkv_tiles + j, 0, 0)),
                  pl.BlockSpec((D_MODEL, LANES), lambda i, j: (0, 0))],
        out_specs=[
            pl.BlockSpec((tm, tn), lambda i, j: (i, j)),
            pl.BlockSpec((tm, LANES), lambda i, j: (i, 0)),
        ],
        out_shape=[
            jax.ShapeDtypeStruct((m, D_P32), F32),
            jax.ShapeDtypeStruct((m, LANES), F32),
        ],
        scratch_shapes=[pltpu.VMEM((tm, D_MODEL), BF16)],
        compiler_params=params,
        name="in_proj_rest",
    )(x2d, norm_w, w_tiles, w_dt)
    return p32, k_new, v_new, qkv, dt_raw


def _rel_bucket(rel):
    half = N_REL_BUCKETS // 2
    max_exact = half // 2
    ret = jnp.where(rel > 0, half, 0)
    n = jnp.abs(rel)
    nf = jnp.maximum(n, 1).astype(F32)
    large = max_exact + (jnp.log(nf / max_exact) / math.log(REL_MAX_DIST / max_exact)
                         * (half - max_exact)).astype(jnp.int32)
    large = jnp.minimum(large, half - 1)
    return ret + jnp.where(n < max_exact, n, large)


def _lambda_init(layer):
    return 0.8 - 0.6 * math.exp(-0.3 * layer)


def _lam_from_refs(lq1, lk1, lq2, lk2, lam0):
    return (jnp.exp(jnp.sum(lq1[...] * lk1[...], axis=-1, keepdims=True))
            - jnp.exp(jnp.sum(lq2[...] * lk2[...], axis=-1, keepdims=True)) + lam0)


def _attn_epilogue(a1, a2, lam, g, subln_w, lam0):
    o = a1 - lam * a2
    ms = jnp.mean(o * o, axis=-1, keepdims=True)
    o = o * lax.rsqrt(ms + NORM_EPS) * subln_w
    o = o * (1.0 - lam0)
    return o * _silu(g)


def _prompt_attn_kernel(q_ref, k_ref, vt_ref, g_ref, brow_ref, sw_ref, lq1, lk1, lq2, lk2,
                        o_ref, m_ref, l_ref, acc_ref, qt_ref, on_ref, bias_ref, *, tile, qblk, lam0):
    qi = pl.program_id(2)
    d = ATT_HEAD_DIM
    units = [(c, mi) for c in range(tile // qblk) for mi in range(2)]
    qt_ref[...] = q_ref[0].astype(F32).T.astype(BF16)

    @pl.when(qi == 0)
    def _():
        for t in range(2):
            rows = jnp.broadcast_to(brow_ref[t, 0], (tile, 2 * tile))
            bias_ref[t] = pltpu.roll(rows, 0, 1, stride=1, stride_axis=0)[:, :tile]
        kc = lax.broadcasted_iota(jnp.int32, (tile, tile), 0) // CHUNK
        qc = lax.broadcasted_iota(jnp.int32, (tile, tile), 1) // CHUNK
        bias_ref[1] = jnp.where(kc <= qc, bias_ref[1], MASK_VALUE)

    def reset(m_init):
        m_ref[...] = jnp.full(m_ref.shape, m_init, F32)
        l_ref[...] = jnp.zeros(l_ref.shape, F32)
        acc_ref[...] = jnp.zeros(acc_ref.shape, F32)

    def kv_span(j, n, bias_idx, lagged, diagonal=False):
        keys = pl.ds(pl.multiple_of(j * tile, tile), n * tile)
        k = k_ref[0, keys, :]
        vt_all = vt_ref[0, :, keys]
        n_keys = [(c + 1) * qblk if diagonal else n * tile for c, _ in units]
        ss = [jnp.dot(k[:nk, mi * d:(mi + 1) * d], qt_ref[mi * d:(mi + 1) * d, c * qblk:(c + 1) * qblk],
                      preferred_element_type=F32) for (c, mi), nk in zip(units, n_keys)]
        for (c, mi), nk, s in zip(units, n_keys, ss):
            cols = slice(c * qblk, (c + 1) * qblk)
            vt = vt_all[:, :nk]
            if bias_idx is not None:
                s = s + bias_ref[bias_idx, :nk, cols]
            m_old = m_ref[mi, :, cols]
            if lagged:
                p = jnp.exp2(s - m_old)
                l_new = l_ref[mi, :, cols] + jnp.sum(p, axis=0, keepdims=True)
                acc_new = acc_ref[mi, :, cols] + jnp.dot(vt, p.astype(BF16), preferred_element_type=F32)
                m_new = jnp.maximum(m_old, jnp.max(s, axis=0, keepdims=True))
                alpha = jnp.exp2(m_old - m_new)
                l_ref[mi, :, cols] = l_new * alpha
                acc_ref[mi, :, cols] = acc_new * alpha
            else:
                m_new = jnp.maximum(m_old, jnp.max(s, axis=0, keepdims=True))
                alpha = jnp.exp2(m_old - m_new)
                p = jnp.exp2(s - m_new)
                l_ref[mi, :, cols] = alpha * l_ref[mi, :, cols] + jnp.sum(p, axis=0, keepdims=True)
                pv = jnp.dot(vt, p.astype(BF16), preferred_element_type=F32)
                acc_ref[mi, :, cols] = alpha * acc_ref[mi, :, cols] + pv
            m_ref[mi, :, cols] = m_new

    n_far = jnp.maximum(qi - 1, 0)
    n_quads = n_far // 4
    n_pairs = n_far // 2

    def all_tiles(lagged):
        reset(0.0 if lagged else MASK_VALUE)

        kv_span(qi, 1, 1, lagged, diagonal=True)

        def off_body(j, carry):
            kv_span(j, 1, 0, lagged)
            return carry
        lax.fori_loop(n_far, qi, off_body, 0)

        def far_body(width):
            def body(i, carry):
                kv_span(width * i, width, None, lagged)
                return carry
            return body
        lax.fori_loop(0, n_quads, far_body(4), 0)
        lax.fori_loop(2 * n_quads, n_pairs, far_body(2), 0)
        lax.fori_loop(2 * n_pairs, n_far, far_body(1), 0)

    lam = _lam_from_refs(lq1, lk1, lq2, lk2, lam0)

    def combine():
        o = acc_ref[0] * (1.0 / l_ref[0]) - lam * (acc_ref[1] * (1.0 / l_ref[1]))
        ms = jnp.mean(o * o, axis=0, keepdims=True)
        on_ref[...] = o * lax.rsqrt(ms + NORM_EPS)
        return ms

    all_tiles(True)
    ms = combine()
    finite = jnp.logical_and(jnp.all(jnp.isfinite(l_ref[...])), jnp.all(jnp.isfinite(ms)))

    @pl.when(jnp.logical_not(finite))
    def _():
        all_tiles(False)
        combine()

    o_ref[0] = (on_ref[...].T * sw_ref[...] * (1.0 - lam0) * _silu(g_ref[0])).astype(BF16)


def _toeplitz(fn, rows, cols):
    period = rows + cols
    d = jnp.arange(period, dtype=jnp.int32)
    g = jnp.moveaxis(fn(jnp.where(d < cols, d, d - period)), 0, -1)
    x = jnp.tile(g, rows)[..., :rows * (period - 1)].reshape(g.shape[:-1] + (rows, period - 1))
    return x[..., :cols]


def _prompt_bias_rows(rel_table, tile):
    assert tile >= REL_MAX_DIST and tile % CHUNK == 0
    table = rel_table.astype(F32) * LOG2E
    far = table[_rel_bucket(jnp.full((1,), -(tile + 1), jnp.int32))]
    d = jnp.arange(2 * tile, dtype=jnp.int32)
    dd = jnp.where(d < tile, d, d - 2 * tile)
    diag = table[_rel_bucket(-dd)] - far
    off = table[_rel_bucket(-dd - tile)] - far
    return jnp.transpose(jnp.stack([off, diag]), (0, 2, 1))[:, :, None, :]


def _prompt_attention(qkv, vt, p32, bias_rows, subln_w, lq1, lk1, lq2, lk2, *, tile, lam0):
    b, s, _ = qkv.shape
    hw = 2 * ATT_HEAD_DIM
    nh = N_ATT_HEADS
    kern = functools.partial(_prompt_attn_kernel, tile=tile, qblk=min(tile, 256), lam0=lam0)
    vec = pl.BlockSpec((1, ATT_HEAD_DIM), lambda bi, h, qi: (0, 0))
    return pl.pallas_call(
        kern,
        grid=(b, nh, s // tile),
        in_specs=[
            pl.BlockSpec((1, tile, hw), lambda bi, h, qi: (bi, qi, h)),
            pl.BlockSpec((1, s, hw), lambda bi, h, qi: (bi, 0, nh + h)),
            pl.BlockSpec((1, hw, s), lambda bi, h, qi: (bi, h, 0)),
            pl.BlockSpec((1, tile, hw), lambda bi, h, qi: (bi, qi, COL_G // hw + h)),
            pl.BlockSpec((2, 1, 1, 2 * tile), lambda bi, h, qi: (0, h, 0, 0)),
            pl.BlockSpec((1, hw), lambda bi, h, qi: (0, 0)),
            vec, vec, vec, vec,
        ],
        out_specs=pl.BlockSpec((1, tile, hw), lambda bi, h, qi: (bi, qi, h)),
        out_shape=jax.ShapeDtypeStruct((b, s, D_ATT), BF16),
        scratch_shapes=[
            pltpu.VMEM((2, 1, tile), F32),
            pltpu.VMEM((2, 1, tile), F32),
            pltpu.VMEM((2, hw, tile), F32),
            pltpu.VMEM((hw, tile), BF16),
            pltpu.VMEM((hw, tile), F32),
            pltpu.VMEM((2, tile, tile), F32),
        ],
        compiler_params=pltpu.CompilerParams(
            dimension_semantics=("parallel", "parallel", "arbitrary"), vmem_limit_bytes=VMEM_LIMIT),
        name="prompt_attention",
    )(qkv, qkv, vt, p32, bias_rows, subln_w, lq1, lk1, lq2, lk2)


def _sample_attn_kernel(q_ref, kn_ref, vn_ref, kp_ref, vp_ref, g_ref, bp_ref, bn_ref, sw_ref,
                        lq1, lk1, lq2, lk2, o_ref, *, lam0):
    d = ATT_HEAD_DIM
    q = q_ref[0]
    kn = kn_ref[0]
    vn = vn_ref[0]
    kp = kp_ref[0]
    vp = vp_ref[0]
    nt = (((1,), (1,)), ((), ()))
    outs = []
    for mi in range(2):
        qm = q[:, mi * d:(mi + 1) * d]
        sp = lax.dot_general(qm, kp[:, mi * d:(mi + 1) * d], nt, preferred_element_type=F32)
        sn = lax.dot_general(qm, kn[:, mi * d:(mi + 1) * d], nt, preferred_element_type=F32)
        sp = sp + bp_ref[0]
        sn = sn + bn_ref[0]
        m = jnp.maximum(jnp.max(sp, axis=-1, keepdims=True), jnp.max(sn, axis=-1, keepdims=True))
        pp = jnp.exp2(sp - m)
        pn = jnp.exp2(sn - m)
        l = jnp.sum(pp, axis=-1, keepdims=True) + jnp.sum(pn, axis=-1, keepdims=True)
        acc = (jnp.dot(pp.astype(BF16), vp, preferred_element_type=F32)
               + jnp.dot(pn.astype(BF16), vn, preferred_element_type=F32))
        outs.append(acc / l)
    lam = _lam_from_refs(lq1, lk1, lq2, lk2, lam0)
    o_ref[0] = _attn_epilogue(outs[0], outs[1], lam, g_ref[0], sw_ref[...], lam0).astype(BF16)


def _sample_bias(rel_table, past_len, s):
    qpos = past_len + jnp.arange(s, dtype=jnp.int32)[:, None]
    kpos = jnp.arange(past_len + s, dtype=jnp.int32)[None, :]
    table = rel_table.astype(F32) * LOG2E
    bias = _toeplitz(lambda dd: table[_rel_bucket(dd - past_len)], s, past_len + s)
    visible = (kpos // CHUNK) <= (qpos // CHUNK)
    return jnp.where(visible[None], bias, MASK_VALUE)


def _sample_attention(qkv, p32, k_past, v_past, bias, subln_w, lq1, lk1, lq2, lk2, *, lam0):
    b, s, _ = qkv.shape
    past = k_past.shape[1]
    hw = 2 * ATT_HEAD_DIM
    nh = N_ATT_HEADS
    bias_p = bias[:, :, :past]
    bias_n = bias[:, :, past:]
    kern = functools.partial(_sample_attn_kernel, lam0=lam0)
    vec = pl.BlockSpec((1, ATT_HEAD_DIM), lambda bi, h: (0, 0))
    return pl.pallas_call(
        kern,
        grid=(b, nh),
        in_specs=[
            pl.BlockSpec((1, s, hw), lambda bi, h: (bi, 0, h)),
            pl.BlockSpec((1, s, hw), lambda bi, h: (bi, 0, nh + h)),
            pl.BlockSpec((1, s, hw), lambda bi, h: (bi, 0, 2 * nh + h)),
            pl.BlockSpec((1, past, hw), lambda bi, h: (bi, 0, h)),
            pl.BlockSpec((1, past, hw), lambda bi, h: (bi, 0, h)),
            pl.BlockSpec((1, s, hw), lambda bi, h: (bi, 0, COL_G // hw + h)),
            pl.BlockSpec((1, s, past), lambda bi, h: (h, 0, 0)),
            pl.BlockSpec((1, s, s), lambda bi, h: (h, 0, 0)),
            pl.BlockSpec((1, hw), lambda bi, h: (0, 0)),
            vec, vec, vec, vec,
        ],
        out_specs=pl.BlockSpec((1, s, hw), lambda bi, h: (bi, 0, h)),
        out_shape=jax.ShapeDtypeStruct((b, s, D_ATT), BF16),
        compiler_params=pltpu.CompilerParams(
            dimension_semantics=("parallel", "parallel"), vmem_limit_bytes=VMEM_LIMIT),
        name="sample_attention",
    )(qkv, qkv, qkv, k_past, v_past, p32, bias_p, bias_n, subln_w, lq1, lk1, lq2, lk2)


def _ssd_kernel(z_ref, xs_ref, b_ref, c_ref, dtT_ref, hist_ref, h0_ref, cw_ref, cb_ref, dtb_ref,
                alog_ref, dskip_ref, nw_ref, y_ref, hout_ref, h_s, xpad_s, xT_s, yT_s, *, valid):
    c = pl.program_id(1)
    L = xs_ref.shape[1]
    P, N, R = SSM_HEAD_DIM, SSM_STATE, HEADS_PER_GROUP

    @pl.when(c == 0)
    def _():
        h_s[...] = h0_ref[0]
        xpad_s[:SUBLANES, :] = hist_ref[0]

    def conv_silu(x_ref, lo, hi):
        xpad_s[SUBLANES:, lo:hi] = x_ref[0]
        acc = cb_ref[:, lo:hi]
        for k in range(CONV_WIDTH):
            acc = acc + (xpad_s[SUBLANES - k:SUBLANES - k + L, lo:hi]
                         * cw_ref[CONV_WIDTH - 1 - k:CONV_WIDTH - k, lo:hi])
        return _silu(acc)

    xs = conv_silu(xs_ref, 0, D_SSM)
    bm = conv_silu(b_ref, D_SSM, D_SSM + D_BC)
    cm = conv_silu(c_ref, D_SSM + D_BC, D_CONV)
    xpad_s[:SUBLANES, :] = xpad_s[L:, :]

    dt = _softplus(dtT_ref[0] + dtb_ref[...])
    if valid < L:
        dt = jnp.where(lax.broadcasted_iota(jnp.int32, dt.shape, 1) < valid, dt, 0.0)
    a = dt * (-jnp.exp(alog_ref[...]))
    s_idx = lax.broadcasted_iota(jnp.int32, (L, L), 0)
    t_idx = lax.broadcasted_iota(jnp.int32, (L, L), 1)
    causal = s_idx <= t_idx
    upper = jnp.where(causal, 1.0, 0.0).astype(BF16)
    a1 = a.astype(BF16)
    r1 = a - a1.astype(F32)
    a2 = r1.astype(BF16)
    a3 = (r1 - a2.astype(F32)).astype(BF16)
    acs = (jnp.dot(a1, upper, preferred_element_type=F32)
           + jnp.dot(a2, upper, preferred_element_type=F32)
           + jnp.dot(a3, upper, preferred_element_type=F32))
    tot = acs[:, L - 1:L]
    e_row = jnp.exp(acs)
    w_row = dt * jnp.exp(tot - acs)
    d_row = jnp.broadcast_to(jnp.exp(tot), (N_SSM_HEADS, N))
    acs_col = jnp.concatenate([acs, jnp.zeros((L - N_SSM_HEADS, L), F32)], axis=0).T

    xT_s[...] = xs.T

    for g in range(N_SSM_GROUPS):
        bg = bm[:, g * N:(g + 1) * N].astype(BF16)
        cgT = cm[:, g * N:(g + 1) * N].T.astype(BF16)
        cbT = jnp.dot(bg, cgT, preferred_element_type=F32)
        for r8 in range(R):
            r = g * R + r8
            rows = slice(r * P, (r + 1) * P)
            seg = acs[r:r + 1, :] - acs_col[:, r:r + 1]
            decay = jnp.exp(jnp.where(causal, seg, MASK_VALUE))
            mT = (cbT * decay).astype(BF16)
            xr = xT_s[rows, :]
            hr = h_s[rows, :]
            y_intra = jnp.dot((xr * dt[r:r + 1, :]).astype(BF16), mT, preferred_element_type=F32)
            y_inter = jnp.dot(hr.astype(BF16), cgT, preferred_element_type=F32)
            yT_s[rows, :] = y_intra + y_inter * e_row[r:r + 1, :]
            upd = jnp.dot((xr * w_row[r:r + 1, :]).astype(BF16), bg, preferred_element_type=F32)
            h_s[rows, :] = hr * d_row[r:r + 1, :] + upd

    y = yT_s[...].T
    y = y + dskip_ref[...] * xs
    y = y * _silu(z_ref[0])
    gs = D_SSM // N_SSM_GROUPS
    outs = []
    for g in range(N_SSM_GROUPS):
        yg = y[:, g * gs:(g + 1) * gs]
        ms = jnp.mean(yg * yg, axis=-1, keepdims=True)
        outs.append(yg * lax.rsqrt(ms + NORM_EPS) * nw_ref[:, g * gs:(g + 1) * gs])
    y_ref[0] = jnp.concatenate(outs, axis=-1).astype(BF16)

    @pl.when(c == pl.num_programs(1) - 1)
    def _():
        hout_ref[0] = h_s[...]


def _ssd(src, cols, dtT, hist8, h0, conv_w, conv_b, dt_bias, a_log, d_full, norm_w, *, valid):
    b, s, _ = src.shape
    L = SSD_L
    col_z, col_xs, col_b, col_c = cols
    kern = functools.partial(_ssd_kernel, valid=valid)
    const2 = lambda shape: pl.BlockSpec(shape, lambda bi, c: (0, 0))
    hp = N_SSM_HEADS * SSM_HEAD_DIM
    return pl.pallas_call(
        kern,
        grid=(b, s // L),
        in_specs=[
            pl.BlockSpec((1, L, D_SSM), lambda bi, c: (bi, c, col_z // D_SSM)),
            pl.BlockSpec((1, L, D_SSM), lambda bi, c: (bi, c, col_xs // D_SSM)),
            pl.BlockSpec((1, L, D_BC), lambda bi, c: (bi, c, col_b // D_BC)),
            pl.BlockSpec((1, L, D_BC), lambda bi, c: (bi, c, col_c // D_BC)),
            pl.BlockSpec((1, N_SSM_HEADS, L), lambda bi, c: (bi, 0, c)),
            pl.BlockSpec((1, SUBLANES, D_CONV), lambda bi, c: (bi, 0, 0)),
            pl.BlockSpec((1, hp, SSM_STATE), lambda bi, c: (bi, 0, 0)),
            const2((CONV_WIDTH, D_CONV)),
            const2((1, D_CONV)),
            const2((N_SSM_HEADS, 1)),
            const2((N_SSM_HEADS, 1)),
            const2((1, D_SSM)),
            const2((1, D_SSM)),
        ],
        out_specs=[
            pl.BlockSpec((1, L, D_SSM), lambda bi, c: (bi, c, 0)),
            pl.BlockSpec((1, hp, SSM_STATE), lambda bi, c: (bi, 0, 0)),
        ],
        out_shape=[
            jax.ShapeDtypeStruct((b, s, D_SSM), BF16),
            jax.ShapeDtypeStruct((b, hp, SSM_STATE), F32),
        ],
        scratch_shapes=[
            pltpu.VMEM((hp, SSM_STATE), F32),
            pltpu.VMEM((SUBLANES + L, D_CONV), F32),
            pltpu.VMEM((D_SSM, L), F32),
            pltpu.VMEM((D_SSM, L), F32),
        ],
        compiler_params=pltpu.CompilerParams(
            dimension_semantics=("parallel", "arbitrary"), vmem_limit_bytes=VMEM_LIMIT),
        name="ssd",
    )(src, src, src, src, dtT, hist8, h0, conv_w, conv_b, dt_bias, a_log, d_full, norm_w)


def _outproj_kernel(a_ref, y_ref, x_ref, w1_ref, w2_ref, fw_ref, o_ref):
    acc = (jnp.dot(a_ref[...], w1_ref[...], preferred_element_type=F32)
           + jnp.dot(y_ref[...], w2_ref[...], preferred_element_type=F32))
    h = x_ref[...] + acc
    ms = jnp.mean(h * h, axis=-1, keepdims=True)
    o_ref[...] = h * lax.rsqrt(ms + NORM_EPS) * fw_ref[...]


def _out_proj(att, y, x2d, w1, w2, final_w, *, tm):
    m = x2d.shape[0]
    const = pl.BlockSpec((D_ATT, D_MODEL), lambda i: (0, 0), pipeline_mode=pl.Buffered(1))
    return pl.pallas_call(
        _outproj_kernel,
        grid=(m // tm,),
        in_specs=[
            pl.BlockSpec((tm, D_ATT), lambda i: (i, 0)),
            pl.BlockSpec((tm, D_SSM), lambda i: (i, 0)),
            pl.BlockSpec((tm, D_MODEL), lambda i: (i, 0)),
            const, const,
            pl.BlockSpec((1, D_MODEL), lambda i: (0, 0)),
        ],
        out_specs=pl.BlockSpec((tm, D_MODEL), lambda i: (i, 0)),
        out_shape=jax.ShapeDtypeStruct((m, D_MODEL), F32),
        compiler_params=pltpu.CompilerParams(
            dimension_semantics=("parallel",), vmem_limit_bytes=VMEM_LIMIT),
        name="out_proj",
    )(att, y, x2d, w1, w2, final_w)


def _merge_cache_kernel(k_ref, v0_ref, v1_ref, ko_ref, vo_ref, *, nk, nv):
    tm = ko_ref.shape[0]
    for c in range(nk):
        ko_ref[:, c * LANES:(c + 1) * LANES] = k_ref[pl.ds(c, tm, stride=nk), :].astype(BF16)
    for half, v_ref in enumerate((v0_ref, v1_ref)):
        by_head = pltpu.einshape("mhd->hmd", v_ref[...])
        for h in range(nv):
            lo = (2 * h + half) * LANES
            vo_ref[:, lo:lo + LANES] = by_head[h].astype(BF16)


def _merge_cache(k3, v3, *, tm):
    rows, nk, _ = k3.shape
    nv = v3.shape[1]
    assert v3.shape[2] == 2 * LANES
    out = jax.ShapeDtypeStruct((rows, D_ATT), BF16)
    return pl.pallas_call(
        functools.partial(_merge_cache_kernel, nk=nk, nv=nv),
        grid=(rows // tm,),
        in_specs=[pl.BlockSpec((tm * nk, LANES), lambda i: (i, 0)),
                  pl.BlockSpec((tm, nv, LANES), lambda i: (i, 0, 0)),
                  pl.BlockSpec((tm, nv, LANES), lambda i: (i, 0, 1))],
        out_specs=[pl.BlockSpec((tm, D_ATT), lambda i: (i, 0)), pl.BlockSpec((tm, D_ATT), lambda i: (i, 0))],
        out_shape=[out, out],
        compiler_params=pltpu.CompilerParams(
            dimension_semantics=("parallel",), vmem_limit_bytes=VMEM_LIMIT),
        name="merge_cache",
    )(k3.reshape(rows * nk, LANES), v3, v3)


def _tiles(m):
    tm = min(m, 1024)
    assert m % tm == 0
    return tm


def _layer(h, k_past, v_past, conv_past, ssm_past, layer, rel_bias, norm_w, w_main, w_dt, lq1, lk1, lq2, lk2,
           subln_w, conv_w, conv_b, dt_bias, a_log, d_full, ssm_norm_w, w_out1, w_out2, out_norm_w):
    b, s, _ = h.shape
    m = b * s
    lam0 = _lambda_init(layer)
    x2d = h.reshape(m, D_MODEL)
    p32, k_new, v_new, qkv, dt_raw = _in_proj(x2d, norm_w, w_main, w_dt, tm=_tiles(m))
    k_new = k_new.reshape(b, s, N_ATT_HEADS, 2, ATT_HEAD_DIM)
    v_new = v_new.reshape(b, s, N_ATT_HEADS, 2 * ATT_HEAD_DIM)
    p32 = p32.reshape(b, s, D_P32)
    qkv = qkv.reshape(b, s, D_QKV)

    if k_past is None:
        tile = min(s, 512)
        vt = jnp.transpose(qkv[:, :, 2 * D_ATT:], (0, 2, 1))
        att = _prompt_attention(qkv, vt, p32, _prompt_bias_rows(rel_bias, tile), subln_w, lq1, lk1, lq2, lk2,
                                tile=tile, lam0=lam0)
    else:
        past = k_past.shape[1]
        kp, vp = _merge_cache(k_past.reshape(b * past, D_ATT // LANES, LANES),
                              v_past.reshape(b * past, N_ATT_HEADS, 2 * ATT_HEAD_DIM), tm=min(b * past, 512))
        att = _sample_attention(qkv, p32, kp.reshape(b, past, D_ATT), vp.reshape(b, past, D_ATT),
                                _sample_bias(rel_bias, past, s), subln_w, lq1, lk1, lq2, lk2, lam0=lam0)

    assert s >= CONV_WIDTH - 1
    conv_new = p32[:, s - (CONV_WIDTH - 1):, COL_XS:COL_XS + D_CONV]
    hist8 = jnp.pad(conv_past, ((0, 0), (SUBLANES - (CONV_WIDTH - 1), 0), (0, 0)))
    dtT = jnp.transpose(dt_raw[:, :N_SSM_HEADS].reshape(b, s, N_SSM_HEADS), (0, 2, 1))
    h0 = ssm_past.reshape(b, N_SSM_HEADS * SSM_HEAD_DIM, SSM_STATE)
    if s % SSD_L == 0:
        src, cols = p32, (COL_Z, COL_XS, COL_B, COL_C)
    else:
        assert s < SSD_L
        pad = SSD_L - s
        src = jnp.pad(p32[:, :, COL_Z:], ((0, 0), (0, pad), (0, 0)))
        cols = (0, D_SSM, 2 * D_SSM, 2 * D_SSM + D_BC)
        dtT = jnp.pad(dtT, ((0, 0), (0, 0), (0, pad)))
    y, ssm_new = _ssd(src, cols, dtT, hist8, h0, conv_w, conv_b, dt_bias, a_log, d_full, ssm_norm_w,
                      valid=min(s, SSD_L))
    y = y[:, :s].reshape(m, D_SSM)
    ssm_new = ssm_new.reshape(b, N_SSM_HEADS, SSM_HEAD_DIM, SSM_STATE)

    out = _out_proj(att.reshape(m, D_ATT), y, x2d, w_out1, w_out2, out_norm_w, tm=min(m, 512))
    return out.reshape(b, s, D_MODEL), k_new, v_new, conv_new, ssm_new


def kernel(x_prompt, x_sample, cache_k, cache_v, cache_conv, state_ssm, rel_bias, norm_w, w_in, lambda_q1,
           lambda_k1, lambda_q2, lambda_k2, subln_w, conv_w, conv_b, dt_bias, A_log, D_skip, ssm_norm_w, w_out,
           final_norm_w):
    depth = w_in.shape[0]
    assert depth == 1, "the final norm is fused into the (single) layer's output projection"
    bp = x_prompt.shape[0]
    l = 0
    col_q = 0
    col_dt = 4 * D_ATT + D_SSM + D_CONV
    w = w_in[l]
    w_main = jnp.transpose(w[:, col_q:col_dt].astype(BF16).reshape(D_MODEL, -1, IN_PROJ_TN), (1, 0, 2))
    w_dt = jnp.pad(w[:, col_dt:], ((0, 0), (0, LANES - N_SSM_HEADS))).astype(BF16)
    row = lambda t: t.reshape(1, -1).astype(F32)
    col = lambda t: t.reshape(-1, 1).astype(F32)
    params = (rel_bias, row(norm_w[l]), w_main, w_dt, row(lambda_q1[l]), row(lambda_k1[l]), row(lambda_q2[l]),
              row(lambda_k2[l]), row(subln_w[l]), conv_w[l].astype(F32), row(conv_b[l]), col(dt_bias[l]),
              col(A_log[l]), row(jnp.repeat(D_skip[l], SSM_HEAD_DIM)), row(ssm_norm_w[l]),
              w_out[l, :D_ATT].astype(BF16), w_out[l, D_ATT:].astype(BF16), row(final_norm_w))
    conv0 = jnp.zeros((bp, CONV_WIDTH - 1, D_CONV), x_prompt.dtype)
    ssm0 = jnp.zeros((bp, N_SSM_HEADS, SSM_HEAD_DIM, SSM_STATE), state_ssm.dtype)
    yp, k1, v1, c1, s1 = _layer(x_prompt, None, None, conv0, ssm0, l, *params)
    ys, k2, v2, c2, s2 = _layer(x_sample, cache_k[l], cache_v[l], cache_conv[l], state_ssm[l], l, *params)
    return (yp, ys, k1[None], v1[None], c1[None], s1[None], k2[None], v2[None], c2[None], s2[None])
```

```python
import functools
import math

import jax
import jax.numpy as jnp
from jax import lax
from jax.experimental import pallas as pl
from jax.experimental.pallas import tpu as pltpu

F32 = jnp.float32
BF16 = jnp.bfloat16

D_MODEL = 2048
CHUNK = 64
NORM_EPS = 1e-5
N_ATT_HEADS = 8
ATT_HEAD_DIM = 128
D_ATT = N_ATT_HEADS * 2 * ATT_HEAD_DIM
N_REL_BUCKETS = 32
REL_MAX_DIST = 128
D_SSM = 2048
SSM_HEAD_DIM = 64
N_SSM_HEADS = D_SSM // SSM_HEAD_DIM
N_SSM_GROUPS = 4
HEADS_PER_GROUP = N_SSM_HEADS // N_SSM_GROUPS
SSM_STATE = 128
CONV_WIDTH = 4
D_BC = N_SSM_GROUPS * SSM_STATE
D_CONV = D_SSM + 2 * D_BC
D_MIX = D_ATT + D_SSM
D_QKV = 3 * D_ATT
D_P32 = D_ATT + D_SSM + D_CONV
COL_G, COL_Z, COL_XS = 0, D_ATT, D_ATT + D_SSM
COL_B, COL_C = COL_XS + D_SSM, COL_XS + D_SSM + D_BC

LANES = 128
SUBLANES = 8
VMEM_LIMIT = 56 * 1024 * 1024
MASK_VALUE = -1e30
LOG2E = math.log2(math.e)
Q_SCALE = ATT_HEAD_DIM ** -0.5 * LOG2E
SSD_L = 128
IN_PROJ_TN = 1024


def _silu(x):
    return x * (1.0 / (1.0 + jnp.exp(-x)))


def _softplus(x):
    return jnp.maximum(x, 0.0) + jnp.log1p(jnp.exp(-jnp.abs(x)))


def _rms_to_scratch(x_ref, nw_ref, u_ref, slab):
    def body(r, carry):
        rows = pl.ds(pl.multiple_of(r * slab, slab), slab)
        x = x_ref[rows, :]
        ms = jnp.mean(x * x, axis=-1, keepdims=True)
        u = x * lax.rsqrt(ms + NORM_EPS) * nw_ref[...]
        u_ref[rows, :] = u.astype(BF16)
        return carry
    lax.fori_loop(0, x_ref.shape[0] // slab, body, 0)


def _inproj_qkv_kernel(x_ref, nw_ref, w_ref, k_ref, v_ref, qkv_ref, u_ref, *, n_q_tiles, slab):
    j = pl.program_id(1)

    @pl.when(j == 0)
    def _():
        _rms_to_scratch(x_ref, nw_ref, u_ref, slab)

    def project():
        return jnp.dot(u_ref[...], w_ref[0], preferred_element_type=F32)

    @pl.when(j < n_q_tiles)
    def _():
        qkv_ref[...] = (project() * Q_SCALE).astype(BF16)

    @pl.when(jnp.logical_and(j >= n_q_tiles, j < 2 * n_q_tiles))
    def _():
        res = project()
        k_ref[...] = pltpu.einshape("m(cd)->mcd", res, d=LANES)
        qkv_ref[...] = res.astype(BF16)

    @pl.when(j >= 2 * n_q_tiles)
    def _():
        res = project()
        v_ref[...] = res
        qkv_ref[...] = res.astype(BF16)


def _inproj_rest_kernel(x_ref, nw_ref, w_ref, wdt_ref, p_ref, dt_ref, u_ref, *, slab):
    @pl.when(pl.program_id(1) == 0)
    def _():
        _rms_to_scratch(x_ref, nw_ref, u_ref, slab)
        dt_ref[...] = jnp.dot(u_ref[...], wdt_ref[...], preferred_element_type=F32)

    p_ref[...] = jnp.dot(u_ref[...], w_ref[0], preferred_element_type=F32)


def _in_proj(x2d, norm_w, w_tiles, w_dt, *, tm):
    m = x2d.shape[0]
    n_tiles, _, tn = w_tiles.shape
    n_q_tiles = D_ATT // tn
    n_qkv_tiles = 3 * n_q_tiles
    slab = min(tm, 256)
    clamp = lambda j, lo: jnp.clip(j - lo * n_q_tiles, 0, n_q_tiles - 1)
    params = pltpu.CompilerParams(dimension_semantics=("parallel", "arbitrary"), vmem_limit_bytes=VMEM_LIMIT)
    x_spec = pl.BlockSpec((tm, D_MODEL), lambda i, j: (i, 0))
    nw_spec = pl.BlockSpec((1, D_MODEL), lambda i, j: (0, 0))
    k_new, v_new, qkv = pl.pallas_call(
        functools.partial(_inproj_qkv_kernel, n_q_tiles=n_q_tiles, slab=slab),
        grid=(m // tm, n_qkv_tiles),
        in_specs=[x_spec, nw_spec, pl.BlockSpec((1, D_MODEL, tn), lambda i, j: (j, 0, 0))],
        out_specs=[
            pl.BlockSpec((tm, tn // LANES, LANES), lambda i, j: (i, clamp(j, 1), 0)),
            pl.BlockSpec((tm, tn), lambda i, j: (i, clamp(j, 2))),
            pl.BlockSpec((tm, tn), lambda i, j: (i, j)),
        ],
        out_shape=[
            jax.ShapeDtypeStruct((m, D_ATT // LANES, LANES), F32),
            jax.ShapeDtypeStruct((m, D_ATT), F32),
            jax.ShapeDtypeStruct((m, D_QKV), BF16),
        ],
        scratch_shapes=[pltpu.VMEM((tm, D_MODEL), BF16)],
        compiler_params=params,
        name="in_proj_qkv",
    )(x2d, norm_w, w_tiles)
    p32, dt_raw = pl.pallas_call(
        functools.partial(_inproj_rest_kernel, slab=slab),
        grid=(m // tm, n_tiles - n_qkv_tiles),
        in_specs=[x_spec, nw_spec, pl.BlockSpec((1, D_MODEL, tn), lambda i, j: (n_qkv_tiles + j, 0, 0)),
                  pl.BlockSpec((D_MODEL, LANES), lambda i, j: (0, 0))],
        out_specs=[
            pl.BlockSpec((tm, tn), lambda i, j: (i, j)),
            pl.BlockSpec((tm, LANES), lambda i, j: (i, 0)),
        ],
        out_shape=[
            jax.ShapeDtypeStruct((m, D_P32), F32),
            jax.ShapeDtypeStruct((m, LANES), F32),
        ],
        scratch_shapes=[pltpu.VMEM((tm, D_MODEL), BF16)],
        compiler_params=params,
        name="in_proj_rest",
    )(x2d, norm_w, w_tiles, w_dt)
    return p32, k_new, v_new, qkv, dt_raw


def _rel_bucket(rel):
    half = N_REL_BUCKETS // 2
    max_exact = half // 2
    ret = jnp.where(rel > 0, half, 0)
    n = jnp.abs(rel)
    nf = jnp.maximum(n, 1).astype(F32)
    large = max_exact + (jnp.log(nf / max_exact) / math.log(REL_MAX_DIST / max_exact)
                         * (half - max_exact)).astype(jnp.int32)
    large = jnp.minimum(large, half - 1)
    return ret + jnp.where(n < max_exact, n, large)


def _lambda_init(layer):
    return 0.8 - 0.6 * math.exp(-0.3 * layer)


def _lam_from_refs(lq1, lk1, lq2, lk2, lam0):
    return (jnp.exp(jnp.sum(lq1[...] * lk1[...], axis=-1, keepdims=True))
            - jnp.exp(jnp.sum(lq2[...] * lk2[...], axis=-1, keepdims=True)) + lam0)


def _attn_epilogue(a1, a2, lam, g, subln_w, lam0):
    o = a1 - lam * a2
    ms = jnp.mean(o * o, axis=-1, keepdims=True)
    o = o * lax.rsqrt(ms + NORM_EPS) * subln_w
    o = o * (1.0 - lam0)
    return o * _silu(g)


def _prompt_attn_kernel(q_ref, k_ref, vt_ref, g_ref, brow_ref, sw_ref, lq1, lk1, lq2, lk2,
                        o_ref, m_ref, l_ref, acc_ref, qt_ref, on_ref, bias_ref, *, tile, qblk, lam0):
    qi = pl.program_id(2)
    d = ATT_HEAD_DIM
    units = [(c, mi) for c in range(tile // qblk) for mi in range(2)]
    qt_ref[...] = q_ref[0].astype(F32).T.astype(BF16)

    @pl.when(qi == 0)
    def _():
        for t in range(2):
            rows = jnp.broadcast_to(brow_ref[t, 0], (tile, 2 * tile))
            bias_ref[t] = pltpu.roll(rows, 0, 1, stride=1, stride_axis=0)[:, :tile]
        kc = lax.broadcasted_iota(jnp.int32, (tile, tile), 0) // CHUNK
        qc = lax.broadcasted_iota(jnp.int32, (tile, tile), 1) // CHUNK
        bias_ref[1] = jnp.where(kc <= qc, bias_ref[1], MASK_VALUE)

    def reset(m_init):
        m_ref[...] = jnp.full(m_ref.shape, m_init, F32)
        l_ref[...] = jnp.zeros(l_ref.shape, F32)
        acc_ref[...] = jnp.zeros(acc_ref.shape, F32)

    def kv_span(j, n, bias_idx, lagged, diagonal=False):
        keys = pl.ds(pl.multiple_of(j * tile, tile), n * tile)
        k = k_ref[0, keys, :]
        vt_all = vt_ref[0, :, keys]
        n_keys = [(c + 1) * qblk if diagonal else n * tile for c, _ in units]
        ss = [jnp.dot(k[:nk, mi * d:(mi + 1) * d], qt_ref[mi * d:(mi + 1) * d, c * qblk:(c + 1) * qblk],
                      preferred_element_type=F32) for (c, mi), nk in zip(units, n_keys)]
        for (c, mi), nk, s in zip(units, n_keys, ss):
            cols = slice(c * qblk, (c + 1) * qblk)
            vt = vt_all[:, :nk]
            if bias_idx is not None:
                s = s + bias_ref[bias_idx, :nk, cols]
            m_old = m_ref[mi, :, cols]
            if lagged:
                p = jnp.exp2(s - m_old)
                l_new = l_ref[mi, :, cols] + jnp.sum(p, axis=0, keepdims=True)
                acc_new = acc_ref[mi, :, cols] + jnp.dot(vt, p.astype(BF16), preferred_element_type=F32)
                m_new = jnp.maximum(m_old, jnp.max(s, axis=0, keepdims=True))
                alpha = jnp.exp2(m_old - m_new)
                l_ref[mi, :, cols] = l_new * alpha
                acc_ref[mi, :, cols] = acc_new * alpha
            else:
                m_new = jnp.maximum(m_old, jnp.max(s, axis=0, keepdims=True))
                alpha = jnp.exp2(m_old - m_new)
                p = jnp.exp2(s - m_new)
                l_ref[mi, :, cols] = alpha * l_ref[mi, :, cols] + jnp.sum(p, axis=0, keepdims=True)
                pv = jnp.dot(vt, p.astype(BF16), preferred_element_type=F32)
                acc_ref[mi, :, cols] = alpha * acc_ref[mi, :, cols] + pv
            m_ref[mi, :, cols] = m_new

    n_far = jnp.maximum(qi - 1, 0)
    n_quads = n_far // 4
    n_pairs = n_far // 2

    def all_tiles(lagged):
        reset(0.0 if lagged else MASK_VALUE)

        kv_span(qi, 1, 1, lagged, diagonal=True)

        def off_body(j, carry):
            kv_span(j, 1, 0, lagged)
            return carry
        lax.fori_loop(n_far, qi, off_body, 0)

        def far_body(width):
            def body(i, carry):
                kv_span(width * i, width, None, lagged)
                return carry
            return body
        lax.fori_loop(0, n_quads, far_body(4), 0)
        lax.fori_loop(2 * n_quads, n_pairs, far_body(2), 0)
        lax.fori_loop(2 * n_pairs, n_far, far_body(1), 0)

    lam = _lam_from_refs(lq1, lk1, lq2, lk2, lam0)

    def combine():
        o = acc_ref[0] * (1.0 / l_ref[0]) - lam * (acc_ref[1] * (1.0 / l_ref[1]))
        ms = jnp.mean(o * o, axis=0, keepdims=True)
        on_ref[...] = o * lax.rsqrt(ms + NORM_EPS)
        return ms

    all_tiles(True)
    ms = combine()
    finite = jnp.logical_and(jnp.all(jnp.isfinite(l_ref[...])), jnp.all(jnp.isfinite(ms)))

    @pl.when(jnp.logical_not(finite))
    def _():
        all_tiles(False)
        combine()

    o_ref[0] = (on_ref[...].T * sw_ref[...] * (1.0 - lam0) * _silu(g_ref[0])).astype(BF16)


def _toeplitz(fn, rows, cols):
    period = rows + cols
    d = jnp.arange(period, dtype=jnp.int32)
    g = jnp.moveaxis(fn(jnp.where(d < cols, d, d - period)), 0, -1)
    x = jnp.tile(g, rows)[..., :rows * (period - 1)].reshape(g.shape[:-1] + (rows, period - 1))
    return x[..., :cols]


def _prompt_bias_rows(rel_table, tile):
    assert tile >= REL_MAX_DIST and tile % CHUNK == 0
    table = rel_table.astype(F32) * LOG2E
    far = table[_rel_bucket(jnp.full((1,), -(tile + 1), jnp.int32))]
    d = jnp.arange(2 * tile, dtype=jnp.int32)
    dd = jnp.where(d < tile, d, d - 2 * tile)
    diag = table[_rel_bucket(-dd)] - far
    off = table[_rel_bucket(-dd - tile)] - far
    return jnp.transpose(jnp.stack([off, diag]), (0, 2, 1))[:, :, None, :]


def _prompt_attention(qkv, vt, p32, bias_rows, subln_w, lq1, lk1, lq2, lk2, *, tile, lam0):
    b, s, _ = qkv.shape
    hw = 2 * ATT_HEAD_DIM
    nh = N_ATT_HEADS
    kern = functools.partial(_prompt_attn_kernel, tile=tile, qblk=min(tile, 256), lam0=lam0)
    vec = pl.BlockSpec((1, ATT_HEAD_DIM), lambda bi, h, qi: (0, 0))
    return pl.pallas_call(
        kern,
        grid=(b, nh, s // tile),
        in_specs=[
            pl.BlockSpec((1, tile, hw), lambda bi, h, qi: (bi, qi, h)),
            pl.BlockSpec((1, s, hw), lambda bi, h, qi: (bi, 0, nh + h)),
            pl.BlockSpec((1, hw, s), lambda bi, h, qi: (bi, h, 0)),
            pl.BlockSpec((1, tile, hw), lambda bi, h, qi: (bi, qi, COL_G // hw + h)),
            pl.BlockSpec((2, 1, 1, 2 * tile), lambda bi, h, qi: (0, h, 0, 0)),
            pl.BlockSpec((1, hw), lambda bi, h, qi: (0, 0)),
            vec, vec, vec, vec,
        ],
        out_specs=pl.BlockSpec((1, tile, hw), lambda bi, h, qi: (bi, qi, h)),
        out_shape=jax.ShapeDtypeStruct((b, s, D_ATT), BF16),
        scratch_shapes=[
            pltpu.VMEM((2, 1, tile), F32),
            pltpu.VMEM((2, 1, tile), F32),
            pltpu.VMEM((2, hw, tile), F32),
            pltpu.VMEM((hw, tile), BF16),
            pltpu.VMEM((hw, tile), F32),
            pltpu.VMEM((2, tile, tile), F32),
        ],
        compiler_params=pltpu.CompilerParams(
            dimension_semantics=("parallel", "parallel", "arbitrary"), vmem_limit_bytes=VMEM_LIMIT),
        name="prompt_attention",
    )(qkv, qkv, vt, p32, bias_rows, subln_w, lq1, lk1, lq2, lk2)


def _sample_attn_kernel(q_ref, kn_ref, vn_ref, kp_ref, vp_ref, g_ref, bp_ref, bn_ref, sw_ref,
                        lq1, lk1, lq2, lk2, o_ref, *, lam0):
    d = ATT_HEAD_DIM
    q = q_ref[0]
    kn = kn_ref[0]
    vn = vn_ref[0]
    kp = kp_ref[0]
    vp = vp_ref[0]
    nt = (((1,), (1,)), ((), ()))
    outs = []
    for mi in range(2):
        qm = q[:, mi * d:(mi + 1) * d]
        sp = lax.dot_general(qm, kp[:, mi * d:(mi + 1) * d], nt, preferred_element_type=F32)
        sn = lax.dot_general(qm, kn[:, mi * d:(mi + 1) * d], nt, preferred_element_type=F32)
        sp = sp + bp_ref[0]
        sn = sn + bn_ref[0]
        m = jnp.maximum(jnp.max(sp, axis=-1, keepdims=True), jnp.max(sn, axis=-1, keepdims=True))
        pp = jnp.exp2(sp - m)
        pn = jnp.exp2(sn - m)
        l = jnp.sum(pp, axis=-1, keepdims=True) + jnp.sum(pn, axis=-1, keepdims=True)
        acc = (jnp.dot(pp.astype(BF16), vp, preferred_element_type=F32)
               + jnp.dot(pn.astype(BF16), vn, preferred_element_type=F32))
        outs.append(acc / l)
    lam = _lam_from_refs(lq1, lk1, lq2, lk2, lam0)
    o_ref[0] = _attn_epilogue(outs[0], outs[1], lam, g_ref[0], sw_ref[...], lam0).astype(BF16)


def _sample_bias(rel_table, past_len, s):
    qpos = past_len + jnp.arange(s, dtype=jnp.int32)[:, None]
    kpos = jnp.arange(past_len + s, dtype=jnp.int32)[None, :]
    table = rel_table.astype(F32) * LOG2E
    bias = _toeplitz(lambda dd: table[_rel_bucket(dd - past_len)], s, past_len + s)
    visible = (kpos // CHUNK) <= (qpos // CHUNK)
    return jnp.where(visible[None], bias, MASK_VALUE)


def _sample_attention(qkv, p32, k_past, v_past, bias, subln_w, lq1, lk1, lq2, lk2, *, lam0):
    b, s, _ = qkv.shape
    past = k_past.shape[1]
    hw = 2 * ATT_HEAD_DIM
    nh = N_ATT_HEADS
    bias_p = bias[:, :, :past]
    bias_n = bias[:, :, past:]
    kern = functools.partial(_sample_attn_kernel, lam0=lam0)
    vec = pl.BlockSpec((1, ATT_HEAD_DIM), lambda bi, h: (0, 0))
    return pl.pallas_call(
        kern,
        grid=(b, nh),
        in_specs=[
            pl.BlockSpec((1, s, hw), lambda bi, h: (bi, 0, h)),
            pl.BlockSpec((1, s, hw), lambda bi, h: (bi, 0, nh + h)),
            pl.BlockSpec((1, s, hw), lambda bi, h: (bi, 0, 2 * nh + h)),
            pl.BlockSpec((1, past, hw), lambda bi, h: (bi, 0, h)),
            pl.BlockSpec((1, past, hw), lambda bi, h: (bi, 0, h)),
            pl.BlockSpec((1, s, hw), lambda bi, h: (bi, 0, COL_G // hw + h)),
            pl.BlockSpec((1, s, past), lambda bi, h: (h, 0, 0)),
            pl.BlockSpec((1, s, s), lambda bi, h: (h, 0, 0)),
            pl.BlockSpec((1, hw), lambda bi, h: (0, 0)),
            vec, vec, vec, vec,
        ],
        out_specs=pl.BlockSpec((1, s, hw), lambda bi, h: (bi, 0, h)),
        out_shape=jax.ShapeDtypeStruct((b, s, D_ATT), BF16),
        compiler_params=pltpu.CompilerParams(
            dimension_semantics=("parallel", "parallel"), vmem_limit_bytes=VMEM_LIMIT),
        name="sample_attention",
    )(qkv, qkv, qkv, k_past, v_past, p32, bias_p, bias_n, subln_w, lq1, lk1, lq2, lk2)


def _ssd_kernel(z_ref, xs_ref, b_ref, c_ref, dtT_ref, hist_ref, h0_ref, cw_ref, cb_ref, dtb_ref,
                alog_ref, dskip_ref, nw_ref, y_ref, hout_ref, h_s, xpad_s, xT_s, yT_s, *, valid):
    c = pl.program_id(1)
    L = xs_ref.shape[1]
    P, N, R = SSM_HEAD_DIM, SSM_STATE, HEADS_PER_GROUP

    @pl.when(c == 0)
    def _():
        h_s[...] = h0_ref[0]
        xpad_s[:SUBLANES, :] = hist_ref[0]

    def conv_silu(x_ref, lo, hi):
        xpad_s[SUBLANES:, lo:hi] = x_ref[0]
        acc = cb_ref[:, lo:hi]
        for k in range(CONV_WIDTH):
            acc = acc + (xpad_s[SUBLANES - k:SUBLANES - k + L, lo:hi]
                         * cw_ref[CONV_WIDTH - 1 - k:CONV_WIDTH - k, lo:hi])
        return _silu(acc)

    xs = conv_silu(xs_ref, 0, D_SSM)
    bm = conv_silu(b_ref, D_SSM, D_SSM + D_BC)
    cm = conv_silu(c_ref, D_SSM + D_BC, D_CONV)
    xpad_s[:SUBLANES, :] = xpad_s[L:, :]

    dt = _softplus(dtT_ref[0] + dtb_ref[...])
    if valid < L:
        dt = jnp.where(lax.broadcasted_iota(jnp.int32, dt.shape, 1) < valid, dt, 0.0)
    a = dt * (-jnp.exp(alog_ref[...]))
    s_idx = lax.broadcasted_iota(jnp.int32, (L, L), 0)
    t_idx = lax.broadcasted_iota(jnp.int32, (L, L), 1)
    causal = s_idx <= t_idx
    upper = jnp.where(causal, 1.0, 0.0).astype(BF16)
    a1 = a.astype(BF16)
    r1 = a - a1.astype(F32)
    a2 = r1.astype(BF16)
    a3 = (r1 - a2.astype(F32)).astype(BF16)
    acs = (jnp.dot(a1, upper, preferred_element_type=F32)
           + jnp.dot(a2, upper, preferred_element_type=F32)
           + jnp.dot(a3, upper, preferred_element_type=F32))
    tot = acs[:, L - 1:L]
    e_row = jnp.exp(acs)
    w_row = dt * jnp.exp(tot - acs)
    d_row = jnp.broadcast_to(jnp.exp(tot), (N_SSM_HEADS, N))
    acs_col = jnp.concatenate([acs, jnp.zeros((L - N_SSM_HEADS, L), F32)], axis=0).T

    xT_s[...] = xs.T

    for g in range(N_SSM_GROUPS):
        bg = bm[:, g * N:(g + 1) * N].astype(BF16)
        cgT = cm[:, g * N:(g + 1) * N].T.astype(BF16)
        cbT = jnp.dot(bg, cgT, preferred_element_type=F32)
        for r8 in range(R):
            r = g * R + r8
            rows = slice(r * P, (r + 1) * P)
            seg = acs[r:r + 1, :] - acs_col[:, r:r + 1]
            decay = jnp.exp(jnp.where(causal, seg, MASK_VALUE))
            mT = (cbT * decay).astype(BF16)
            xr = xT_s[rows, :]
            hr = h_s[rows, :]
            y_intra = jnp.dot((xr * dt[r:r + 1, :]).astype(BF16), mT, preferred_element_type=F32)
            y_inter = jnp.dot(hr.astype(BF16), cgT, preferred_element_type=F32)
            yT_s[rows, :] = y_intra + y_inter * e_row[r:r + 1, :]
            upd = jnp.dot((xr * w_row[r:r + 1, :]).astype(BF16), bg, preferred_element_type=F32)
            h_s[rows, :] = hr * d_row[r:r + 1, :] + upd

    y = yT_s[...].T
    y = y + dskip_ref[...] * xs
    y = y * _silu(z_ref[0])
    gs = D_SSM // N_SSM_GROUPS
    outs = []
    for g in range(N_SSM_GROUPS):
        yg = y[:, g * gs:(g + 1) * gs]
        ms = jnp.mean(yg * yg, axis=-1, keepdims=True)
        outs.append(yg * lax.rsqrt(ms + NORM_EPS) * nw_ref[:, g * gs:(g + 1) * gs])
    y_ref[0] = jnp.concatenate(outs, axis=-1).astype(BF16)

    @pl.when(c == pl.num_programs(1) - 1)
    def _():
        hout_ref[0] = h_s[...]


def _ssd(src, cols, dtT, hist8, h0, conv_w, conv_b, dt_bias, a_log, d_full, norm_w, *, valid):
    b, s, _ = src.shape
    L = SSD_L
    col_z, col_xs, col_b, col_c = cols
    kern = functools.partial(_ssd_kernel, valid=valid)
    const2 = lambda shape: pl.BlockSpec(shape, lambda bi, c: (0, 0))
    hp = N_SSM_HEADS * SSM_HEAD_DIM
    return pl.pallas_call(
        kern,
        grid=(b, s // L),
        in_specs=[
            pl.BlockSpec((1, L, D_SSM), lambda bi, c: (bi, c, col_z // D_SSM)),
            pl.BlockSpec((1, L, D_SSM), lambda bi, c: (bi, c, col_xs // D_SSM)),
            pl.BlockSpec((1, L, D_BC), lambda bi, c: (bi, c, col_b // D_BC)),
            pl.BlockSpec((1, L, D_BC), lambda bi, c: (bi, c, col_c // D_BC)),
            pl.BlockSpec((1, N_SSM_HEADS, L), lambda bi, c: (bi, 0, c)),
            pl.BlockSpec((1, SUBLANES, D_CONV), lambda bi, c: (bi, 0, 0)),
            pl.BlockSpec((1, hp, SSM_STATE), lambda bi, c: (bi, 0, 0)),
            const2((CONV_WIDTH, D_CONV)),
            const2((1, D_CONV)),
            const2((N_SSM_HEADS, 1)),
            const2((N_SSM_HEADS, 1)),
            const2((1, D_SSM)),
            const2((1, D_SSM)),
        ],
        out_specs=[
            pl.BlockSpec((1, L, D_SSM), lambda bi, c: (bi, c, 0)),
            pl.BlockSpec((1, hp, SSM_STATE), lambda bi, c: (bi, 0, 0)),
        ],
        out_shape=[
            jax.ShapeDtypeStruct((b, s, D_SSM), BF16),
            jax.ShapeDtypeStruct((b, hp, SSM_STATE), F32),
        ],
        scratch_shapes=[
            pltpu.VMEM((hp, SSM_STATE), F32),
            pltpu.VMEM((SUBLANES + L, D_CONV), F32),
            pltpu.VMEM((D_SSM, L), F32),
            pltpu.VMEM((D_SSM, L), F32),
        ],
        compiler_params=pltpu.CompilerParams(
            dimension_semantics=("parallel", "arbitrary"), vmem_limit_bytes=VMEM_LIMIT),
        name="ssd",
    )(src, src, src, src, dtT, hist8, h0, conv_w, conv_b, dt_bias, a_log, d_full, norm_w)


def _outproj_kernel(a_ref, y_ref, x_ref, w1_ref, w2_ref, fw_ref, o_ref):
    acc = (jnp.dot(a_ref[...], w1_ref[...], preferred_element_type=F32)
           + jnp.dot(y_ref[...], w2_ref[...], preferred_element_type=F32))
    h = x_ref[...] + acc
    ms = jnp.mean(h * h, axis=-1, keepdims=True)
    o_ref[...] = h * lax.rsqrt(ms + NORM_EPS) * fw_ref[...]


def _out_proj(att, y, x2d, w1, w2, final_w, *, tm):
    m = x2d.shape[0]
    const = pl.BlockSpec((D_ATT, D_MODEL), lambda i: (0, 0), pipeline_mode=pl.Buffered(1))
    return pl.pallas_call(
        _outproj_kernel,
        grid=(m // tm,),
        in_specs=[
            pl.BlockSpec((tm, D_ATT), lambda i: (i, 0)),
            pl.BlockSpec((tm, D_SSM), lambda i: (i, 0)),
            pl.BlockSpec((tm, D_MODEL), lambda i: (i, 0)),
            const, const,
            pl.BlockSpec((1, D_MODEL), lambda i: (0, 0)),
        ],
        out_specs=pl.BlockSpec((tm, D_MODEL), lambda i: (i, 0)),
        out_shape=jax.ShapeDtypeStruct((m, D_MODEL), F32),
        compiler_params=pltpu.CompilerParams(
            dimension_semantics=("parallel",), vmem_limit_bytes=VMEM_LIMIT),
        name="out_proj",
    )(att, y, x2d, w1, w2, final_w)


def _merge_cache_kernel(k_ref, v_ref, ko_ref, vo_ref, *, nk, nv):
    tm = ko_ref.shape[0]
    for c in range(nk):
        ko_ref[:, c * LANES:(c + 1) * LANES] = k_ref[pl.ds(c, tm, stride=nk), :].astype(BF16)
    for half in range(v_ref.shape[2] // LANES):
        by_head = pltpu.einshape("mhd->hmd", v_ref[:, :, half * LANES:(half + 1) * LANES])
        for h in range(nv):
            lo = (2 * h + half) * LANES
            vo_ref[:, lo:lo + LANES] = by_head[h].astype(BF16)


def _merge_cache(k3, v3, *, tm):
    rows, nk, _ = k3.shape
    nv = v3.shape[1]
    assert v3.shape[2] == 2 * LANES
    out = jax.ShapeDtypeStruct((rows, D_ATT), BF16)
    return pl.pallas_call(
        functools.partial(_merge_cache_kernel, nk=nk, nv=nv),
        grid=(rows // tm,),
        in_specs=[pl.BlockSpec((tm * nk, LANES), lambda i: (i, 0)),
                  pl.BlockSpec((tm, nv, 2 * LANES), lambda i: (i, 0, 0))],
        out_specs=[pl.BlockSpec((tm, D_ATT), lambda i: (i, 0)), pl.BlockSpec((tm, D_ATT), lambda i: (i, 0))],
        out_shape=[out, out],
        compiler_params=pltpu.CompilerParams(
            dimension_semantics=("parallel",), vmem_limit_bytes=VMEM_LIMIT),
        name="merge_cache",
    )(k3.reshape(rows * nk, LANES), v3)


def _tiles(m):
    tm = min(m, 1024)
    assert m % tm == 0
    return tm


def _layer(h, k_past, v_past, conv_past, ssm_past, layer, rel_bias, norm_w, w_main, w_dt, lq1, lk1, lq2, lk2,
           subln_w, conv_w, conv_b, dt_bias, a_log, d_full, ssm_norm_w, w_out1, w_out2, out_norm_w):
    b, s, _ = h.shape
    m = b * s
    lam0 = _lambda_init(layer)
    x2d = h.reshape(m, D_MODEL)
    p32, k_new, v_new, qkv, dt_raw = _in_proj(x2d, norm_w, w_main, w_dt, tm=_tiles(m))
    k_new = k_new.reshape(b, s, N_ATT_HEADS, 2, ATT_HEAD_DIM)
    v_new = v_new.reshape(b, s, N_ATT_HEADS, 2 * ATT_HEAD_DIM)
    p32 = p32.reshape(b, s, D_P32)
    qkv = qkv.reshape(b, s, D_QKV)

    if k_past is None:
        tile = min(s, 512)
        vt = jnp.transpose(qkv[:, :, 2 * D_ATT:], (0, 2, 1))
        att = _prompt_attention(qkv, vt, p32, _prompt_bias_rows(rel_bias, tile), subln_w, lq1, lk1, lq2, lk2,
                                tile=tile, lam0=lam0)
    else:
        past = k_past.shape[1]
        kp, vp = _merge_cache(k_past.reshape(b * past, D_ATT // LANES, LANES),
                              v_past.reshape(b * past, N_ATT_HEADS, 2 * ATT_HEAD_DIM), tm=min(b * past, 512))
        att = _sample_attention(qkv, p32, kp.reshape(b, past, D_ATT), vp.reshape(b, past, D_ATT),
                                _sample_bias(rel_bias, past, s), subln_w, lq1, lk1, lq2, lk2, lam0=lam0)

    assert s >= CONV_WIDTH - 1
    conv_new = p32[:, s - (CONV_WIDTH - 1):, COL_XS:COL_XS + D_CONV]
    hist8 = jnp.pad(conv_past, ((0, 0), (SUBLANES - (CONV_WIDTH - 1), 0), (0, 0)))
    dtT = jnp.transpose(dt_raw[:, :N_SSM_HEADS].reshape(b, s, N_SSM_HEADS), (0, 2, 1))
    h0 = ssm_past.reshape(b, N_SSM_HEADS * SSM_HEAD_DIM, SSM_STATE)
    if s % SSD_L == 0:
        src, cols = p32, (COL_Z, COL_XS, COL_B, COL_C)
    else:
        assert s < SSD_L
        pad = SSD_L - s
        src = jnp.pad(p32[:, :, COL_Z:], ((0, 0), (0, pad), (0, 0)))
        cols = (0, D_SSM, 2 * D_SSM, 2 * D_SSM + D_BC)
        dtT = jnp.pad(dtT, ((0, 0), (0, 0), (0, pad)))
    y, ssm_new = _ssd(src, cols, dtT, hist8, h0, conv_w, conv_b, dt_bias, a_log, d_full, ssm_norm_w,
                      valid=min(s, SSD_L))
    y = y[:, :s].reshape(m, D_SSM)
    ssm_new = ssm_new.reshape(b, N_SSM_HEADS, SSM_HEAD_DIM, SSM_STATE)

    out = _out_proj(att.reshape(m, D_ATT), y, x2d, w_out1, w_out2, out_norm_w, tm=min(m, 512))
    return out.reshape(b, s, D_MODEL), k_new, v_new, conv_new, ssm_new


def kernel(x_prompt, x_sample, cache_k, cache_v, cache_conv, state_ssm, rel_bias, norm_w, w_in, lambda_q1,
           lambda_k1, lambda_q2, lambda_k2, subln_w, conv_w, conv_b, dt_bias, A_log, D_skip, ssm_norm_w, w_out,
           final_norm_w):
    depth = w_in.shape[0]
    assert depth == 1, "the final norm is fused into the (single) layer's output projection"
    bp = x_prompt.shape[0]
    l = 0
    col_q = 0
    col_dt = 4 * D_ATT + D_SSM + D_CONV
    w = w_in[l]
    w_main = jnp.transpose(w[:, col_q:col_dt].astype(BF16).reshape(D_MODEL, -1, IN_PROJ_TN), (1, 0, 2))
    w_dt = jnp.pad(w[:, col_dt:], ((0, 0), (0, LANES - N_SSM_HEADS))).astype(BF16)
    row = lambda t: t.reshape(1, -1).astype(F32)
    col = lambda t: t.reshape(-1, 1).astype(F32)
    params = (rel_bias, row(norm_w[l]), w_main, w_dt, row(lambda_q1[l]), row(lambda_k1[l]), row(lambda_q2[l]),
              row(lambda_k2[l]), row(subln_w[l]), conv_w[l].astype(F32), row(conv_b[l]), col(dt_bias[l]),
              col(A_log[l]), row(jnp.repeat(D_skip[l], SSM_HEAD_DIM)), row(ssm_norm_w[l]),
              w_out[l, :D_ATT].astype(BF16), w_out[l, D_ATT:].astype(BF16), row(final_norm_w))
    conv0 = jnp.zeros((bp, CONV_WIDTH - 1, D_CONV), x_prompt.dtype)
    ssm0 = jnp.zeros((bp, N_SSM_HEADS, SSM_HEAD_DIM, SSM_STATE), state_ssm.dtype)
    yp, k1, v1, c1, s1 = _layer(x_prompt, None, None, conv0, ssm0, l, *params)
    ys, k2, v2, c2, s2 = _layer(x_sample, cache_k[l], cache_v[l], cache_conv[l], state_ssm[l], l, *params)
    return (yp, ys, k1[None], v1[None], c1[None], s1[None], k2[None], v2[None], c2[None], s2[None])
```

```python
import functools
import math

import jax
import jax.numpy as jnp
from jax import lax
from jax.experimental import pallas as pl
from jax.experimental.pallas import tpu as pltpu

F32 = jnp.float32
BF16 = jnp.bfloat16

D_MODEL = 2048
CHUNK = 64
NORM_EPS = 1e-5
N_ATT_HEADS = 8
ATT_HEAD_DIM = 128
D_ATT = N_ATT_HEADS * 2 * ATT_HEAD_DIM
N_REL_BUCKETS = 32
REL_MAX_DIST = 128
D_SSM = 2048
SSM_HEAD_DIM = 64
N_SSM_HEADS = D_SSM // SSM_HEAD_DIM
N_SSM_GROUPS = 4
HEADS_PER_GROUP = N_SSM_HEADS // N_SSM_GROUPS
SSM_STATE = 128
CONV_WIDTH = 4
D_BC = N_SSM_GROUPS * SSM_STATE
D_CONV = D_SSM + 2 * D_BC
D_MIX = D_ATT + D_SSM
D_QKV = 3 * D_ATT
D_P32 = D_ATT + D_SSM + D_CONV
COL_G, COL_Z, COL_XS = 0, D_ATT, D_ATT + D_SSM
COL_B, COL_C = COL_XS + D_SSM, COL_XS + D_SSM + D_BC

LANES = 128
SUBLANES = 8
VMEM_LIMIT = 56 * 1024 * 1024
MASK_VALUE = -1e30
LOG2E = math.log2(math.e)
Q_SCALE = ATT_HEAD_DIM ** -0.5 * LOG2E
SSD_L = 128
IN_PROJ_TN = 1024


def _silu(x):
    return x * (1.0 / (1.0 + jnp.exp(-x)))


def _softplus(x):
    return jnp.maximum(x, 0.0) + jnp.log1p(jnp.exp(-jnp.abs(x)))


def _rms_to_scratch(x_ref, nw_ref, u_ref, slab):
    def body(r, carry):
        rows = pl.ds(pl.multiple_of(r * slab, slab), slab)
        x = x_ref[rows, :]
        ms = jnp.mean(x * x, axis=-1, keepdims=True)
        u = x * lax.rsqrt(ms + NORM_EPS) * nw_ref[...]
        u_ref[rows, :] = u.astype(BF16)
        return carry
    lax.fori_loop(0, x_ref.shape[0] // slab, body, 0)


def _inproj_qkv_kernel(x_ref, nw_ref, w_ref, k_ref, v_ref, qkv_ref, vt_ref, u_ref, *, n_q_tiles, slab):
    j = pl.program_id(1)

    @pl.when(j == 0)
    def _():
        _rms_to_scratch(x_ref, nw_ref, u_ref, slab)

    def project():
        return jnp.dot(u_ref[...], w_ref[0], preferred_element_type=F32)

    @pl.when(j < n_q_tiles)
    def _():
        qkv_ref[...] = (project() * Q_SCALE).astype(BF16)

    @pl.when(jnp.logical_and(j >= n_q_tiles, j < 2 * n_q_tiles))
    def _():
        res = project()
        k_ref[...] = pltpu.einshape("m(cd)->mcd", res, d=LANES)
        qkv_ref[...] = res.astype(BF16)

    @pl.when(j >= 2 * n_q_tiles)
    def _():
        res = project()
        v_ref[...] = res
        qkv_ref[...] = res.astype(BF16)
        vt_ref[...] = res.T.astype(BF16)


def _inproj_rest_kernel(x_ref, nw_ref, w_ref, wdt_ref, p_ref, dt_ref, u_ref, *, slab):
    @pl.when(pl.program_id(1) == 0)
    def _():
        _rms_to_scratch(x_ref, nw_ref, u_ref, slab)
        dt_ref[...] = jnp.dot(u_ref[...], wdt_ref[...], preferred_element_type=F32)

    p_ref[...] = jnp.dot(u_ref[...], w_ref[0], preferred_element_type=F32)


def _in_proj(x2d, norm_w, w_tiles, w_dt, *, tm):
    m = x2d.shape[0]
    n_tiles, _, tn = w_tiles.shape
    n_q_tiles = D_ATT // tn
    n_qkv_tiles = 3 * n_q_tiles
    slab = min(tm, 256)
    clamp = lambda j, lo: jnp.clip(j - lo * n_q_tiles, 0, n_q_tiles - 1)
    params = pltpu.CompilerParams(dimension_semantics=("parallel", "arbitrary"), vmem_limit_bytes=VMEM_LIMIT)
    x_spec = pl.BlockSpec((tm, D_MODEL), lambda i, j: (i, 0))
    nw_spec = pl.BlockSpec((1, D_MODEL), lambda i, j: (0, 0))
    k_new, v_new, qkv, vt = pl.pallas_call(
        functools.partial(_inproj_qkv_kernel, n_q_tiles=n_q_tiles, slab=slab),
        grid=(m // tm, n_qkv_tiles),
        in_specs=[x_spec, nw_spec, pl.BlockSpec((1, D_MODEL, tn), lambda i, j: (j, 0, 0))],
        out_specs=[
            pl.BlockSpec((tm, tn // LANES, LANES), lambda i, j: (i, clamp(j, 1), 0)),
            pl.BlockSpec((tm, tn), lambda i, j: (i, clamp(j, 2))),
            pl.BlockSpec((tm, tn), lambda i, j: (i, j)),
            pl.BlockSpec((tn, tm), lambda i, j: (clamp(j, 2), i)),
        ],
        out_shape=[
            jax.ShapeDtypeStruct((m, D_ATT // LANES, LANES), F32),
            jax.ShapeDtypeStruct((m, D_ATT), F32),
            jax.ShapeDtypeStruct((m, D_QKV), BF16),
            jax.ShapeDtypeStruct((D_ATT, m), BF16),
        ],
        scratch_shapes=[pltpu.VMEM((tm, D_MODEL), BF16)],
        compiler_params=params,
        name="in_proj_qkv",
    )(x2d, norm_w, w_tiles)
    p32, dt_raw = pl.pallas_call(
        functools.partial(_inproj_rest_kernel, slab=slab),
        grid=(m // tm, n_tiles - n_qkv_tiles),
        in_specs=[x_spec, nw_spec, pl.BlockSpec((1, D_MODEL, tn), lambda i, j: (n_qkv_tiles + j, 0, 0)),
                  pl.BlockSpec((D_MODEL, LANES), lambda i, j: (0, 0))],
        out_specs=[
            pl.BlockSpec((tm, tn), lambda i, j: (i, j)),
            pl.BlockSpec((tm, LANES), lambda i, j: (i, 0)),
        ],
        out_shape=[
            jax.ShapeDtypeStruct((m, D_P32), F32),
            jax.ShapeDtypeStruct((m, LANES), F32),
        ],
        scratch_shapes=[pltpu.VMEM((tm, D_MODEL), BF16)],
        compiler_params=params,
        name="in_proj_rest",
    )(x2d, norm_w, w_tiles, w_dt)
    return p32, k_new, v_new, qkv, vt, dt_raw


def _rel_bucket(rel):
    half = N_REL_BUCKETS // 2
    max_exact = half // 2
    ret = jnp.where(rel > 0, half, 0)
    n = jnp.abs(rel)
    nf = jnp.maximum(n, 1).astype(F32)
    large = max_exact + (jnp.log(nf / max_exact) / math.log(REL_MAX_DIST / max_exact)
                         * (half - max_exact)).astype(jnp.int32)
    large = jnp.minimum(large, half - 1)
    return ret + jnp.where(n < max_exact, n, large)


def _lambda_init(layer):
    return 0.8 - 0.6 * math.exp(-0.3 * layer)


def _lam_from_refs(lq1, lk1, lq2, lk2, lam0):
    return (jnp.exp(jnp.sum(lq1[...] * lk1[...], axis=-1, keepdims=True))
            - jnp.exp(jnp.sum(lq2[...] * lk2[...], axis=-1, keepdims=True)) + lam0)


def _attn_epilogue(a1, a2, lam, g, subln_w, lam0):
    o = a1 - lam * a2
    ms = jnp.mean(o * o, axis=-1, keepdims=True)
    o = o * lax.rsqrt(ms + NORM_EPS) * subln_w
    o = o * (1.0 - lam0)
    return o * _silu(g)


def _prompt_attn_kernel(q_ref, k_ref, vt_ref, g_ref, brow_ref, sw_ref, lq1, lk1, lq2, lk2,
                        o_ref, m_ref, l_ref, acc_ref, qt_ref, on_ref, bias_ref, *, tile, qblk, lam0):
    qi = pl.program_id(2)
    d = ATT_HEAD_DIM
    units = [(c, mi) for c in range(tile // qblk) for mi in range(2)]
    qt_ref[...] = q_ref[0].astype(F32).T.astype(BF16)

    @pl.when(qi == 0)
    def _():
        for t in range(2):
            rows = jnp.broadcast_to(brow_ref[t, 0], (tile, 2 * tile))
            bias_ref[t] = pltpu.roll(rows, 0, 1, stride=1, stride_axis=0)[:, :tile]
        kc = lax.broadcasted_iota(jnp.int32, (tile, tile), 0) // CHUNK
        qc = lax.broadcasted_iota(jnp.int32, (tile, tile), 1) // CHUNK
        bias_ref[1] = jnp.where(kc <= qc, bias_ref[1], MASK_VALUE)

    def reset(m_init):
        m_ref[...] = jnp.full(m_ref.shape, m_init, F32)
        l_ref[...] = jnp.zeros(l_ref.shape, F32)
        acc_ref[...] = jnp.zeros(acc_ref.shape, F32)

    def kv_span(j, n, bias_idx, lagged, diagonal=False):
        keys = pl.ds(pl.multiple_of(j * tile, tile), n * tile)
        k = k_ref[0, keys, :]
        vt_all = vt_ref[:, keys]
        n_keys = [(c + 1) * qblk if diagonal else n * tile for c, _ in units]
        ss = [jnp.dot(k[:nk, mi * d:(mi + 1) * d], qt_ref[mi * d:(mi + 1) * d, c * qblk:(c + 1) * qblk],
                      preferred_element_type=F32) for (c, mi), nk in zip(units, n_keys)]
        for (c, mi), nk, s in zip(units, n_keys, ss):
            cols = slice(c * qblk, (c + 1) * qblk)
            vt = vt_all[:, :nk]
            if bias_idx is not None:
                s = s + bias_ref[bias_idx, :nk, cols]
            m_old = m_ref[mi, :, cols]
            if lagged:
                p = jnp.exp2(s - m_old)
                l_new = l_ref[mi, :, cols] + jnp.sum(p, axis=0, keepdims=True)
                acc_new = acc_ref[mi, :, cols] + jnp.dot(vt, p.astype(BF16), preferred_element_type=F32)
                m_new = jnp.maximum(m_old, jnp.max(s, axis=0, keepdims=True))
                alpha = jnp.exp2(m_old - m_new)
                l_ref[mi, :, cols] = l_new * alpha
                acc_ref[mi, :, cols] = acc_new * alpha
            else:
                m_new = jnp.maximum(m_old, jnp.max(s, axis=0, keepdims=True))
                alpha = jnp.exp2(m_old - m_new)
                p = jnp.exp2(s - m_new)
                l_ref[mi, :, cols] = alpha * l_ref[mi, :, cols] + jnp.sum(p, axis=0, keepdims=True)
                pv = jnp.dot(vt, p.astype(BF16), preferred_element_type=F32)
                acc_ref[mi, :, cols] = alpha * acc_ref[mi, :, cols] + pv
            m_ref[mi, :, cols] = m_new

    n_far = jnp.maximum(qi - 1, 0)
    n_quads = n_far // 4
    n_pairs = n_far // 2

    def all_tiles(lagged):
        reset(0.0 if lagged else MASK_VALUE)

        kv_span(qi, 1, 1, lagged, diagonal=True)

        def off_body(j, carry):
            kv_span(j, 1, 0, lagged)
            return carry
        lax.fori_loop(n_far, qi, off_body, 0)

        def far_body(width):
            def body(i, carry):
                kv_span(width * i, width, None, lagged)
                return carry
            return body
        lax.fori_loop(0, n_quads, far_body(4), 0)
        lax.fori_loop(2 * n_quads, n_pairs, far_body(2), 0)
        lax.fori_loop(2 * n_pairs, n_far, far_body(1), 0)

    lam = _lam_from_refs(lq1, lk1, lq2, lk2, lam0)

    def combine():
        o = acc_ref[0] * (1.0 / l_ref[0]) - lam * (acc_ref[1] * (1.0 / l_ref[1]))
        ms = jnp.mean(o * o, axis=0, keepdims=True)
        on_ref[...] = o * lax.rsqrt(ms + NORM_EPS)
        return ms

    all_tiles(True)
    ms = combine()
    finite = jnp.logical_and(jnp.all(jnp.isfinite(l_ref[...])), jnp.all(jnp.isfinite(ms)))

    @pl.when(jnp.logical_not(finite))
    def _():
        all_tiles(False)
        combine()

    o_ref[0] = (on_ref[...].T * sw_ref[...] * (1.0 - lam0) * _silu(g_ref[0])).astype(BF16)


def _toeplitz(fn, rows, cols):
    period = rows + cols
    d = jnp.arange(period, dtype=jnp.int32)
    g = jnp.moveaxis(fn(jnp.where(d < cols, d, d - period)), 0, -1)
    x = jnp.tile(g, rows)[..., :rows * (period - 1)].reshape(g.shape[:-1] + (rows, period - 1))
    return x[..., :cols]


def _prompt_bias_rows(rel_table, tile):
    assert tile >= REL_MAX_DIST and tile % CHUNK == 0
    table = rel_table.astype(F32) * LOG2E
    far = table[_rel_bucket(jnp.full((1,), -(tile + 1), jnp.int32))]
    d = jnp.arange(2 * tile, dtype=jnp.int32)
    dd = jnp.where(d < tile, d, d - 2 * tile)
    diag = table[_rel_bucket(-dd)] - far
    off = table[_rel_bucket(-dd - tile)] - far
    return jnp.transpose(jnp.stack([off, diag]), (0, 2, 1))[:, :, None, :]


def _prompt_attention(qkv, vt, p32, bias_rows, subln_w, lq1, lk1, lq2, lk2, *, tile, lam0):
    b, s, _ = qkv.shape
    hw = 2 * ATT_HEAD_DIM
    nh = N_ATT_HEADS
    kern = functools.partial(_prompt_attn_kernel, tile=tile, qblk=min(tile, 256), lam0=lam0)
    vec = pl.BlockSpec((1, ATT_HEAD_DIM), lambda bi, h, qi: (0, 0))
    return pl.pallas_call(
        kern,
        grid=(b, nh, s // tile),
        in_specs=[
            pl.BlockSpec((1, tile, hw), lambda bi, h, qi: (bi, qi, h)),
            pl.BlockSpec((1, s, hw), lambda bi, h, qi: (bi, 0, nh + h)),
            pl.BlockSpec((hw, s), lambda bi, h, qi: (h, bi)),
            pl.BlockSpec((1, tile, hw), lambda bi, h, qi: (bi, qi, COL_G // hw + h)),
            pl.BlockSpec((2, 1, 1, 2 * tile), lambda bi, h, qi: (0, h, 0, 0)),
            pl.BlockSpec((1, hw), lambda bi, h, qi: (0, 0)),
            vec, vec, vec, vec,
        ],
        out_specs=pl.BlockSpec((1, tile, hw), lambda bi, h, qi: (bi, qi, h)),
        out_shape=jax.ShapeDtypeStruct((b, s, D_ATT), BF16),
        scratch_shapes=[
            pltpu.VMEM((2, 1, tile), F32),
            pltpu.VMEM((2, 1, tile), F32),
            pltpu.VMEM((2, hw, tile), F32),
            pltpu.VMEM((hw, tile), BF16),
            pltpu.VMEM((hw, tile), F32),
            pltpu.VMEM((2, tile, tile), F32),
        ],
        compiler_params=pltpu.CompilerParams(
            dimension_semantics=("parallel", "parallel", "arbitrary"), vmem_limit_bytes=VMEM_LIMIT),
        name="prompt_attention",
    )(qkv, qkv, vt, p32, bias_rows, subln_w, lq1, lk1, lq2, lk2)


def _sample_attn_kernel(q_ref, kn_ref, vn_ref, kp_ref, vp_ref, g_ref, bp_ref, bn_ref, sw_ref,
                        lq1, lk1, lq2, lk2, o_ref, *, lam0):
    d = ATT_HEAD_DIM
    q = q_ref[0]
    kn = kn_ref[0]
    vn = vn_ref[0]
    kp = kp_ref[0]
    vp = vp_ref[0]
    nt = (((1,), (1,)), ((), ()))
    outs = []
    for mi in range(2):
        qm = q[:, mi * d:(mi + 1) * d]
        sp = lax.dot_general(qm, kp[:, mi * d:(mi + 1) * d], nt, preferred_element_type=F32)
        sn = lax.dot_general(qm, kn[:, mi * d:(mi + 1) * d], nt, preferred_element_type=F32)
        sp = sp + bp_ref[0]
        sn = sn + bn_ref[0]
        m = jnp.maximum(jnp.max(sp, axis=-1, keepdims=True), jnp.max(sn, axis=-1, keepdims=True))
        pp = jnp.exp2(sp - m)
        pn = jnp.exp2(sn - m)
        l = jnp.sum(pp, axis=-1, keepdims=True) + jnp.sum(pn, axis=-1, keepdims=True)
        acc = (jnp.dot(pp.astype(BF16), vp, preferred_element_type=F32)
               + jnp.dot(pn.astype(BF16), vn, preferred_element_type=F32))
        outs.append(acc / l)
    lam = _lam_from_refs(lq1, lk1, lq2, lk2, lam0)
    o_ref[0] = _attn_epilogue(outs[0], outs[1], lam, g_ref[0], sw_ref[...], lam0).astype(BF16)


def _sample_bias(rel_table, past_len, s):
    qpos = past_len + jnp.arange(s, dtype=jnp.int32)[:, None]
    kpos = jnp.arange(past_len + s, dtype=jnp.int32)[None, :]
    table = rel_table.astype(F32) * LOG2E
    bias = _toeplitz(lambda dd: table[_rel_bucket(dd - past_len)], s, past_len + s)
    visible = (kpos // CHUNK) <= (qpos // CHUNK)
    return jnp.where(visible[None], bias, MASK_VALUE)


def _sample_attention(qkv, p32, k_past, v_past, bias, subln_w, lq1, lk1, lq2, lk2, *, lam0):
    b, s, _ = qkv.shape
    past = k_past.shape[1]
    hw = 2 * ATT_HEAD_DIM
    nh = N_ATT_HEADS
    bias_p = bias[:, :, :past]
    bias_n = bias[:, :, past:]
    kern = functools.partial(_sample_attn_kernel, lam0=lam0)
    vec = pl.BlockSpec((1, ATT_HEAD_DIM), lambda bi, h: (0, 0))
    return pl.pallas_call(
        kern,
        grid=(b, nh),
        in_specs=[
            pl.BlockSpec((1, s, hw), lambda bi, h: (bi, 0, h)),
            pl.BlockSpec((1, s, hw), lambda bi, h: (bi, 0, nh + h)),
            pl.BlockSpec((1, s, hw), lambda bi, h: (bi, 0, 2 * nh + h)),
            pl.BlockSpec((1, past, hw), lambda bi, h: (bi, 0, h)),
            pl.BlockSpec((1, past, hw), lambda bi, h: (bi, 0, h)),
            pl.BlockSpec((1, s, hw), lambda bi, h: (bi, 0, COL_G // hw + h)),
            pl.BlockSpec((1, s, past), lambda bi, h: (h, 0, 0)),
            pl.BlockSpec((1, s, s), lambda bi, h: (h, 0, 0)),
            pl.BlockSpec((1, hw), lambda bi, h: (0, 0)),
            vec, vec, vec, vec,
        ],
        out_specs=pl.BlockSpec((1, s, hw), lambda bi, h: (bi, 0, h)),
        out_shape=jax.ShapeDtypeStruct((b, s, D_ATT), BF16),
        compiler_params=pltpu.CompilerParams(
            dimension_semantics=("parallel", "parallel"), vmem_limit_bytes=VMEM_LIMIT),
        name="sample_attention",
    )(qkv, qkv, qkv, k_past, v_past, p32, bias_p, bias_n, subln_w, lq1, lk1, lq2, lk2)


def _ssd_kernel(z_ref, xs_ref, b_ref, c_ref, dtT_ref, hist_ref, h0_ref, cw_ref, cb_ref, dtb_ref,
                alog_ref, dskip_ref, nw_ref, y_ref, hout_ref, h_s, xpad_s, xT_s, yT_s, *, valid):
    c = pl.program_id(1)
    L = xs_ref.shape[1]
    P, N, R = SSM_HEAD_DIM, SSM_STATE, HEADS_PER_GROUP

    @pl.when(c == 0)
    def _():
        h_s[...] = h0_ref[0]
        xpad_s[:SUBLANES, :] = hist_ref[0]

    def conv_silu(x_ref, lo, hi):
        xpad_s[SUBLANES:, lo:hi] = x_ref[0]
        acc = cb_ref[:, lo:hi]
        for k in range(CONV_WIDTH):
            acc = acc + (xpad_s[SUBLANES - k:SUBLANES - k + L, lo:hi]
                         * cw_ref[CONV_WIDTH - 1 - k:CONV_WIDTH - k, lo:hi])
        return _silu(acc)

    xs = conv_silu(xs_ref, 0, D_SSM)
    bm = conv_silu(b_ref, D_SSM, D_SSM + D_BC)
    cm = conv_silu(c_ref, D_SSM + D_BC, D_CONV)
    xpad_s[:SUBLANES, :] = xpad_s[L:, :]

    dt = _softplus(dtT_ref[0] + dtb_ref[...])
    if valid < L:
        dt = jnp.where(lax.broadcasted_iota(jnp.int32, dt.shape, 1) < valid, dt, 0.0)
    a = dt * (-jnp.exp(alog_ref[...]))
    s_idx = lax.broadcasted_iota(jnp.int32, (L, L), 0)
    t_idx = lax.broadcasted_iota(jnp.int32, (L, L), 1)
    causal = s_idx <= t_idx
    upper = jnp.where(causal, 1.0, 0.0).astype(BF16)
    a1 = a.astype(BF16)
    r1 = a - a1.astype(F32)
    a2 = r1.astype(BF16)
    a3 = (r1 - a2.astype(F32)).astype(BF16)
    acs = (jnp.dot(a1, upper, preferred_element_type=F32)
           + jnp.dot(a2, upper, preferred_element_type=F32)
           + jnp.dot(a3, upper, preferred_element_type=F32))
    tot = acs[:, L - 1:L]
    e_row = jnp.exp(acs)
    w_row = dt * jnp.exp(tot - acs)
    d_row = jnp.broadcast_to(jnp.exp(tot), (N_SSM_HEADS, N))
    acs_col = jnp.concatenate([acs, jnp.zeros((L - N_SSM_HEADS, L), F32)], axis=0).T

    xT_s[...] = xs.T

    for g in range(N_SSM_GROUPS):
        bg = bm[:, g * N:(g + 1) * N].astype(BF16)
        cgT = cm[:, g * N:(g + 1) * N].T.astype(BF16)
        cbT = jnp.dot(bg, cgT, preferred_element_type=F32)
        for r8 in range(R):
            r = g * R + r8
            rows = slice(r * P, (r + 1) * P)
            seg = acs[r:r + 1, :] - acs_col[:, r:r + 1]
            decay = jnp.exp(jnp.where(causal, seg, MASK_VALUE))
            mT = (cbT * decay).astype(BF16)
            xr = xT_s[rows, :]
            hr = h_s[rows, :]
            y_intra = jnp.dot((xr * dt[r:r + 1, :]).astype(BF16), mT, preferred_element_type=F32)
            y_inter = jnp.dot(hr.astype(BF16), cgT, preferred_element_type=F32)
            yT_s[rows, :] = y_intra + y_inter * e_row[r:r + 1, :]
            upd = jnp.dot((xr * w_row[r:r + 1, :]).astype(BF16), bg, preferred_element_type=F32)
            h_s[rows, :] = hr * d_row[r:r + 1, :] + upd

    y = yT_s[...].T
    y = y + dskip_ref[...] * xs
    y = y * _silu(z_ref[0])
    gs = D_SSM // N_SSM_GROUPS
    outs = []
    for g in range(N_SSM_GROUPS):
        yg = y[:, g * gs:(g + 1) * gs]
        ms = jnp.mean(yg * yg, axis=-1, keepdims=True)
        outs.append(yg * lax.rsqrt(ms + NORM_EPS) * nw_ref[:, g * gs:(g + 1) * gs])
    y_ref[0] = jnp.concatenate(outs, axis=-1).astype(BF16)

    @pl.when(c == pl.num_programs(1) - 1)
    def _():
        hout_ref[0] = h_s[...]


def _ssd(src, cols, dtT, hist8, h0, conv_w, conv_b, dt_bias, a_log, d_full, norm_w, *, valid):
    b, s, _ = src.shape
    L = SSD_L
    col_z, col_xs, col_b, col_c = cols
    kern = functools.partial(_ssd_kernel, valid=valid)
    const2 = lambda shape: pl.BlockSpec(shape, lambda bi, c: (0, 0))
    hp = N_SSM_HEADS * SSM_HEAD_DIM
    return pl.pallas_call(
        kern,
        grid=(b, s // L),
        in_specs=[
            pl.BlockSpec((1, L, D_SSM), lambda bi, c: (bi, c, col_z // D_SSM)),
            pl.BlockSpec((1, L, D_SSM), lambda bi, c: (bi, c, col_xs // D_SSM)),
            pl.BlockSpec((1, L, D_BC), lambda bi, c: (bi, c, col_b // D_BC)),
            pl.BlockSpec((1, L, D_BC), lambda bi, c: (bi, c, col_c // D_BC)),
            pl.BlockSpec((1, N_SSM_HEADS, L), lambda bi, c: (bi, 0, c)),
            pl.BlockSpec((1, SUBLANES, D_CONV), lambda bi, c: (bi, 0, 0)),
            pl.BlockSpec((1, hp, SSM_STATE), lambda bi, c: (bi, 0, 0)),
            const2((CONV_WIDTH, D_CONV)),
            const2((1, D_CONV)),
            const2((N_SSM_HEADS, 1)),
            const2((N_SSM_HEADS, 1)),
            const2((1, D_SSM)),
            const2((1, D_SSM)),
        ],
        out_specs=[
            pl.BlockSpec((1, L, D_SSM), lambda bi, c: (bi, c, 0)),
            pl.BlockSpec((1, hp, SSM_STATE), lambda bi, c: (bi, 0, 0)),
        ],
        out_shape=[
            jax.ShapeDtypeStruct((b, s, D_SSM), BF16),
            jax.ShapeDtypeStruct((b, hp, SSM_STATE), F32),
        ],
        scratch_shapes=[
            pltpu.VMEM((hp, SSM_STATE), F32),
            pltpu.VMEM((SUBLANES + L, D_CONV), F32),
            pltpu.VMEM((D_SSM, L), F32),
            pltpu.VMEM((D_SSM, L), F32),
        ],
        compiler_params=pltpu.CompilerParams(
            dimension_semantics=("parallel", "arbitrary"), vmem_limit_bytes=VMEM_LIMIT),
        name="ssd",
    )(src, src, src, src, dtT, hist8, h0, conv_w, conv_b, dt_bias, a_log, d_full, norm_w)


def _outproj_kernel(a_ref, y_ref, x_ref, w1_ref, w2_ref, fw_ref, o_ref):
    acc = (jnp.dot(a_ref[...], w1_ref[...], preferred_element_type=F32)
           + jnp.dot(y_ref[...], w2_ref[...], preferred_element_type=F32))
    h = x_ref[...] + acc
    ms = jnp.mean(h * h, axis=-1, keepdims=True)
    o_ref[...] = h * lax.rsqrt(ms + NORM_EPS) * fw_ref[...]


def _out_proj(att, y, x2d, w1, w2, final_w, *, tm):
    m = x2d.shape[0]
    const = pl.BlockSpec((D_ATT, D_MODEL), lambda i: (0, 0), pipeline_mode=pl.Buffered(1))
    return pl.pallas_call(
        _outproj_kernel,
        grid=(m // tm,),
        in_specs=[
            pl.BlockSpec((tm, D_ATT), lambda i: (i, 0)),
            pl.BlockSpec((tm, D_SSM), lambda i: (i, 0)),
            pl.BlockSpec((tm, D_MODEL), lambda i: (i, 0)),
            const, const,
            pl.BlockSpec((1, D_MODEL), lambda i: (0, 0)),
        ],
        out_specs=pl.BlockSpec((tm, D_MODEL), lambda i: (i, 0)),
        out_shape=jax.ShapeDtypeStruct((m, D_MODEL), F32),
        compiler_params=pltpu.CompilerParams(
            dimension_semantics=("parallel",), vmem_limit_bytes=VMEM_LIMIT),
        name="out_proj",
    )(att, y, x2d, w1, w2, final_w)


def _merge_cache_kernel(k_ref, v_ref, ko_ref, vo_ref, *, nk, nv):
    tm = ko_ref.shape[0]
    for c in range(nk):
        ko_ref[:, c * LANES:(c + 1) * LANES] = k_ref[pl.ds(c, tm, stride=nk), :].astype(BF16)
    for half in range(v_ref.shape[2] // LANES):
        by_head = pltpu.einshape("mhd->hmd", v_ref[:, :, half * LANES:(half + 1) * LANES])
        for h in range(nv):
            lo = (2 * h + half) * LANES
            vo_ref[:, lo:lo + LANES] = by_head[h].astype(BF16)


def _merge_cache(k3, v3, *, tm):
    rows, nk, _ = k3.shape
    nv = v3.shape[1]
    assert v3.shape[2] == 2 * LANES
    out = jax.ShapeDtypeStruct((rows, D_ATT), BF16)
    return pl.pallas_call(
        functools.partial(_merge_cache_kernel, nk=nk, nv=nv),
        grid=(rows // tm,),
        in_specs=[pl.BlockSpec((tm * nk, LANES), lambda i: (i, 0)),
                  pl.BlockSpec((tm, nv, 2 * LANES), lambda i: (i, 0, 0))],
        out_specs=[pl.BlockSpec((tm, D_ATT), lambda i: (i, 0)), pl.BlockSpec((tm, D_ATT), lambda i: (i, 0))],
        out_shape=[out, out],
        compiler_params=pltpu.CompilerParams(
            dimension_semantics=("parallel",), vmem_limit_bytes=VMEM_LIMIT),
        name="merge_cache",
    )(k3.reshape(rows * nk, LANES), v3)


def _tiles(m):
    tm = min(m, 1024)
    assert m % tm == 0
    return tm


def _layer(h, k_past, v_past, conv_past, ssm_past, layer, rel_bias, norm_w, w_main, w_dt, lq1, lk1, lq2, lk2,
           subln_w, conv_w, conv_b, dt_bias, a_log, d_full, ssm_norm_w, w_out1, w_out2, out_norm_w):
    b, s, _ = h.shape
    m = b * s
    lam0 = _lambda_init(layer)
    x2d = h.reshape(m, D_MODEL)
    p32, k_new, v_new, qkv, vt, dt_raw = _in_proj(x2d, norm_w, w_main, w_dt, tm=_tiles(m))
    k_new = k_new.reshape(b, s, N_ATT_HEADS, 2, ATT_HEAD_DIM)
    v_new = v_new.reshape(b, s, N_ATT_HEADS, 2 * ATT_HEAD_DIM)
    p32 = p32.reshape(b, s, D_P32)
    qkv = qkv.reshape(b, s, D_QKV)

    if k_past is None:
        tile = min(s, 512)
        att = _prompt_attention(qkv, vt, p32, _prompt_bias_rows(rel_bias, tile), subln_w, lq1, lk1, lq2, lk2,
                                tile=tile, lam0=lam0)
    else:
        past = k_past.shape[1]
        kp, vp = _merge_cache(k_past.reshape(b * past, D_ATT // LANES, LANES),
                              v_past.reshape(b * past, N_ATT_HEADS, 2 * ATT_HEAD_DIM), tm=min(b * past, 512))
        att = _sample_attention(qkv, p32, kp.reshape(b, past, D_ATT), vp.reshape(b, past, D_ATT),
                                _sample_bias(rel_bias, past, s), subln_w, lq1, lk1, lq2, lk2, lam0=lam0)

    assert s >= CONV_WIDTH - 1
    conv_new = p32[:, s - (CONV_WIDTH - 1):, COL_XS:COL_XS + D_CONV]
    hist8 = jnp.pad(conv_past, ((0, 0), (SUBLANES - (CONV_WIDTH - 1), 0), (0, 0)))
    dtT = jnp.transpose(dt_raw[:, :N_SSM_HEADS].reshape(b, s, N_SSM_HEADS), (0, 2, 1))
    h0 = ssm_past.reshape(b, N_SSM_HEADS * SSM_HEAD_DIM, SSM_STATE)
    if s % SSD_L == 0:
        src, cols = p32, (COL_Z, COL_XS, COL_B, COL_C)
    else:
        assert s < SSD_L
        pad = SSD_L - s
        src = jnp.pad(p32[:, :, COL_Z:], ((0, 0), (0, pad), (0, 0)))
        cols = (0, D_SSM, 2 * D_SSM, 2 * D_SSM + D_BC)
        dtT = jnp.pad(dtT, ((0, 0), (0, 0), (0, pad)))
    y, ssm_new = _ssd(src, cols, dtT, hist8, h0, conv_w, conv_b, dt_bias, a_log, d_full, ssm_norm_w,
                      valid=min(s, SSD_L))
    y = y[:, :s].reshape(m, D_SSM)
    ssm_new = ssm_new.reshape(b, N_SSM_HEADS, SSM_HEAD_DIM, SSM_STATE)

    out = _out_proj(att.reshape(m, D_ATT), y, x2d, w_out1, w_out2, out_norm_w, tm=min(m, 512))
    return out.reshape(b, s, D_MODEL), k_new, v_new, conv_new, ssm_new


def kernel(x_prompt, x_sample, cache_k, cache_v, cache_conv, state_ssm, rel_bias, norm_w, w_in, lambda_q1,
           lambda_k1, lambda_q2, lambda_k2, subln_w, conv_w, conv_b, dt_bias, A_log, D_skip, ssm_norm_w, w_out,
           final_norm_w):
    depth = w_in.shape[0]
    assert depth == 1, "the final norm is fused into the (single) layer's output projection"
    bp = x_prompt.shape[0]
    l = 0
    col_q = 0
    col_dt = 4 * D_ATT + D_SSM + D_CONV
    w = w_in[l]
    w_main = jnp.transpose(w[:, col_q:col_dt].astype(BF16).reshape(D_MODEL, -1, IN_PROJ_TN), (1, 0, 2))
    w_dt = jnp.pad(w[:, col_dt:], ((0, 0), (0, LANES - N_SSM_HEADS))).astype(BF16)
    row = lambda t: t.reshape(1, -1).astype(F32)
    col = lambda t: t.reshape(-1, 1).astype(F32)
    params = (rel_bias, row(norm_w[l]), w_main, w_dt, row(lambda_q1[l]), row(lambda_k1[l]), row(lambda_q2[l]),
              row(lambda_k2[l]), row(subln_w[l]), conv_w[l].astype(F32), row(conv_b[l]), col(dt_bias[l]),
              col(A_log[l]), row(jnp.repeat(D_skip[l], SSM_HEAD_DIM)), row(ssm_norm_w[l]),
              w_out[l, :D_ATT].astype(BF16), w_out[l, D_ATT:].astype(BF16), row(final_norm_w))
    conv0 = jnp.zeros((bp, CONV_WIDTH - 1, D_CONV), x_prompt.dtype)
    ssm0 = jnp.zeros((bp, N_SSM_HEADS, SSM_HEAD_DIM, SSM_STATE), state_ssm.dtype)
    yp, k1, v1, c1, s1 = _layer(x_prompt, None, None, conv0, ssm0, l, *params)
    ys, k2, v2, c2, s2 = _layer(x_sample, cache_k[l], cache_v[l], cache_conv[l], state_ssm[l], l, *params)
    return (yp, ys, k1[None], v1[None], c1[None], s1[None], k2[None], v2[None], c2[None], s2[None])
```

```python
import functools
import math

import jax
import jax.numpy as jnp
from jax import lax
from jax.experimental import pallas as pl
from jax.experimental.pallas import tpu as pltpu

F32 = jnp.float32
BF16 = jnp.bfloat16

D_MODEL = 2048
CHUNK = 64
NORM_EPS = 1e-5
N_ATT_HEADS = 8
ATT_HEAD_DIM = 128
D_ATT = N_ATT_HEADS * 2 * ATT_HEAD_DIM
N_REL_BUCKETS = 32
REL_MAX_DIST = 128
D_SSM = 2048
SSM_HEAD_DIM = 64
N_SSM_HEADS = D_SSM // SSM_HEAD_DIM
N_SSM_GROUPS = 4
HEADS_PER_GROUP = N_SSM_HEADS // N_SSM_GROUPS
SSM_STATE = 128
CONV_WIDTH = 4
D_BC = N_SSM_GROUPS * SSM_STATE
D_CONV = D_SSM + 2 * D_BC
D_MIX = D_ATT + D_SSM
D_QKV = 3 * D_ATT
D_P32 = D_ATT + D_SSM + D_CONV
COL_G, COL_Z, COL_XS = 0, D_ATT, D_ATT + D_SSM
COL_B, COL_C = COL_XS + D_SSM, COL_XS + D_SSM + D_BC

LANES = 128
SUBLANES = 8
VMEM_LIMIT = 56 * 1024 * 1024
MASK_VALUE = -1e30
LOG2E = math.log2(math.e)
Q_SCALE = ATT_HEAD_DIM ** -0.5 * LOG2E
SSD_L = 128
IN_PROJ_TN = 1024


def _silu(x):
    return x * (1.0 / (1.0 + jnp.exp(-x)))


def _softplus(x):
    return jnp.maximum(x, 0.0) + jnp.log1p(jnp.exp(-jnp.abs(x)))


def _rms_to_scratch(x_ref, nw_ref, u_ref, slab):
    def body(r, carry):
        rows = pl.ds(pl.multiple_of(r * slab, slab), slab)
        x = x_ref[rows, :]
        ms = jnp.mean(x * x, axis=-1, keepdims=True)
        u = x * lax.rsqrt(ms + NORM_EPS) * nw_ref[...]
        u_ref[rows, :] = u.astype(BF16)
        return carry
    lax.fori_loop(0, x_ref.shape[0] // slab, body, 0)


def _inproj_qkv_kernel(x_ref, nw_ref, w_ref, k_ref, v_ref, qkv_ref, vt_ref, u_ref, *, n_q_tiles, slab):
    j = pl.program_id(1)

    @pl.when(j == 0)
    def _():
        _rms_to_scratch(x_ref, nw_ref, u_ref, slab)

    def project():
        return jnp.dot(u_ref[...], w_ref[0], preferred_element_type=F32)

    @pl.when(j < n_q_tiles)
    def _():
        qkv_ref[...] = (project() * Q_SCALE).astype(BF16)

    @pl.when(jnp.logical_and(j >= n_q_tiles, j < 2 * n_q_tiles))
    def _():
        res = project()
        k_ref[...] = pltpu.einshape("m(cd)->mcd", res, d=LANES)
        qkv_ref[...] = res.astype(BF16)

    @pl.when(j >= 2 * n_q_tiles)
    def _():
        res = project()
        v_ref[...] = res
        qkv_ref[...] = res.astype(BF16)
        vt_ref[...] = res.T.astype(BF16)


def _inproj_rest_kernel(x_ref, nw_ref, w_ref, wdt_ref, p_ref, dt_ref, u_ref, *, slab):
    @pl.when(pl.program_id(1) == 0)
    def _():
        _rms_to_scratch(x_ref, nw_ref, u_ref, slab)
        dt_ref[...] = jnp.dot(u_ref[...], wdt_ref[...], preferred_element_type=F32)

    p_ref[...] = jnp.dot(u_ref[...], w_ref[0], preferred_element_type=F32)


def _in_proj(x2d, norm_w, w_tiles, w_dt, *, tm):
    m = x2d.shape[0]
    n_tiles, _, tn = w_tiles.shape
    n_q_tiles = D_ATT // tn
    n_qkv_tiles = 3 * n_q_tiles
    slab = min(tm, 256)
    clamp = lambda j, lo: jnp.clip(j - lo * n_q_tiles, 0, n_q_tiles - 1)
    params = pltpu.CompilerParams(dimension_semantics=("parallel", "arbitrary"), vmem_limit_bytes=VMEM_LIMIT)
    x_spec = pl.BlockSpec((tm, D_MODEL), lambda i, j: (i, 0))
    nw_spec = pl.BlockSpec((1, D_MODEL), lambda i, j: (0, 0))
    k_new, v_new, qkv, vt = pl.pallas_call(
        functools.partial(_inproj_qkv_kernel, n_q_tiles=n_q_tiles, slab=slab),
        grid=(m // tm, n_qkv_tiles),
        in_specs=[x_spec, nw_spec, pl.BlockSpec((1, D_MODEL, tn), lambda i, j: (j, 0, 0))],
        out_specs=[
            pl.BlockSpec((tm, tn // LANES, LANES), lambda i, j: (i, clamp(j, 1), 0)),
            pl.BlockSpec((tm, tn), lambda i, j: (i, clamp(j, 2))),
            pl.BlockSpec((tm, tn), lambda i, j: (i, j)),
            pl.BlockSpec((tn, tm), lambda i, j: (clamp(j, 2), i)),
        ],
        out_shape=[
            jax.ShapeDtypeStruct((m, D_ATT // LANES, LANES), F32),
            jax.ShapeDtypeStruct((m, D_ATT), F32),
            jax.ShapeDtypeStruct((m, D_QKV), BF16),
            jax.ShapeDtypeStruct((D_ATT, m), BF16),
        ],
        scratch_shapes=[pltpu.VMEM((tm, D_MODEL), BF16)],
        compiler_params=params,
        name="in_proj_qkv",
    )(x2d, norm_w, w_tiles)
    p32, dt_raw = pl.pallas_call(
        functools.partial(_inproj_rest_kernel, slab=slab),
        grid=(m // tm, n_tiles - n_qkv_tiles),
        in_specs=[x_spec, nw_spec, pl.BlockSpec((1, D_MODEL, tn), lambda i, j: (n_qkv_tiles + j, 0, 0)),
                  pl.BlockSpec((D_MODEL, LANES), lambda i, j: (0, 0))],
        out_specs=[
            pl.BlockSpec((tm, tn), lambda i, j: (i, j)),
            pl.BlockSpec((tm, LANES), lambda i, j: (i, 0)),
        ],
        out_shape=[
            jax.ShapeDtypeStruct((m, D_P32), F32),
            jax.ShapeDtypeStruct((m, LANES), F32),
        ],
        scratch_shapes=[pltpu.VMEM((tm, D_MODEL), BF16)],
        compiler_params=params,
        name="in_proj_rest",
    )(x2d, norm_w, w_tiles, w_dt)
    return p32, k_new, v_new, qkv, vt, dt_raw


def _rel_bucket(rel):
    half = N_REL_BUCKETS // 2
    max_exact = half // 2
    ret = jnp.where(rel > 0, half, 0)
    n = jnp.abs(rel)
    nf = jnp.maximum(n, 1).astype(F32)
    large = max_exact + (jnp.log(nf / max_exact) / math.log(REL_MAX_DIST / max_exact)
                         * (half - max_exact)).astype(jnp.int32)
    large = jnp.minimum(large, half - 1)
    return ret + jnp.where(n < max_exact, n, large)


def _lambda_init(layer):
    return 0.8 - 0.6 * math.exp(-0.3 * layer)


def _lam_from_refs(lq1, lk1, lq2, lk2, lam0):
    return (jnp.exp(jnp.sum(lq1[...] * lk1[...], axis=-1, keepdims=True))
            - jnp.exp(jnp.sum(lq2[...] * lk2[...], axis=-1, keepdims=True)) + lam0)


def _attn_epilogue(a1, a2, lam, g, subln_w, lam0):
    o = a1 - lam * a2
    ms = jnp.mean(o * o, axis=-1, keepdims=True)
    o = o * lax.rsqrt(ms + NORM_EPS) * subln_w
    o = o * (1.0 - lam0)
    return o * _silu(g)


def _prompt_attn_kernel(q_ref, k_ref, vt_ref, g_ref, brow_ref, sw_ref, lq1, lk1, lq2, lk2,
                        o_ref, m_ref, l_ref, acc_ref, qt_ref, on_ref, bias_ref, *, tile, qblk, lam0):
    qi = pl.program_id(2)
    d = ATT_HEAD_DIM
    units = [(c, mi) for c in range(tile // qblk) for mi in range(2)]
    qt_ref[...] = q_ref[0].astype(F32).T.astype(BF16)

    @pl.when(qi == 0)
    def _():
        for t in range(2):
            rows = jnp.broadcast_to(brow_ref[t, 0], (tile, 2 * tile))
            bias_ref[t] = pltpu.roll(rows, 0, 1, stride=1, stride_axis=0)[:, :tile]
        kc = lax.broadcasted_iota(jnp.int32, (tile, tile), 0) // CHUNK
        qc = lax.broadcasted_iota(jnp.int32, (tile, tile), 1) // CHUNK
        bias_ref[1] = jnp.where(kc <= qc, bias_ref[1], MASK_VALUE)

    def reset(m_init):
        m_ref[...] = jnp.full(m_ref.shape, m_init, F32)
        l_ref[...] = jnp.zeros(l_ref.shape, F32)
        acc_ref[...] = jnp.zeros(acc_ref.shape, F32)

    def kv_span(j, n, bias_idx, lagged, diagonal=False):
        keys = pl.ds(pl.multiple_of(j * tile, tile), n * tile)
        k = k_ref[0, keys, :]
        vt_all = vt_ref[:, keys]
        n_keys = [(c + 1) * qblk if diagonal else n * tile for c, _ in units]
        ss = [jnp.dot(k[:nk, mi * d:(mi + 1) * d], qt_ref[mi * d:(mi + 1) * d, c * qblk:(c + 1) * qblk],
                      preferred_element_type=F32) for (c, mi), nk in zip(units, n_keys)]
        for (c, mi), nk, s in zip(units, n_keys, ss):
            cols = slice(c * qblk, (c + 1) * qblk)
            vt = vt_all[:, :nk]
            if bias_idx is not None:
                s = s + bias_ref[bias_idx, :nk, cols]
            m_old = m_ref[mi, :, cols]
            if lagged:
                p = jnp.exp2(s - m_old)
                l_new = l_ref[mi, :, cols] + jnp.sum(p, axis=0, keepdims=True)
                acc_new = acc_ref[mi, :, cols] + jnp.dot(vt, p.astype(BF16), preferred_element_type=F32)
                m_new = jnp.maximum(m_old, jnp.max(s, axis=0, keepdims=True))
                alpha = jnp.exp2(m_old - m_new)
                l_ref[mi, :, cols] = l_new * alpha
                acc_ref[mi, :, cols] = acc_new * alpha
            else:
                m_new = jnp.maximum(m_old, jnp.max(s, axis=0, keepdims=True))
                alpha = jnp.exp2(m_old - m_new)
                p = jnp.exp2(s - m_new)
                l_ref[mi, :, cols] = alpha * l_ref[mi, :, cols] + jnp.sum(p, axis=0, keepdims=True)
                pv = jnp.dot(vt, p.astype(BF16), preferred_element_type=F32)
                acc_ref[mi, :, cols] = alpha * acc_ref[mi, :, cols] + pv
            m_ref[mi, :, cols] = m_new

    n_far = jnp.maximum(qi - 1, 0)
    n_quads = n_far // 4
    n_pairs = n_far // 2

    def all_tiles(lagged):
        reset(0.0 if lagged else MASK_VALUE)

        kv_span(qi, 1, 1, lagged, diagonal=True)

        def off_body(j, carry):
            kv_span(j, 1, 0, lagged)
            return carry
        lax.fori_loop(n_far, qi, off_body, 0)

        def far_body(width):
            def body(i, carry):
                kv_span(width * i, width, None, lagged)
                return carry
            return body
        lax.fori_loop(0, n_quads, far_body(4), 0)
        lax.fori_loop(2 * n_quads, n_pairs, far_body(2), 0)
        lax.fori_loop(2 * n_pairs, n_far, far_body(1), 0)

    lam = _lam_from_refs(lq1, lk1, lq2, lk2, lam0)

    def combine():
        o = acc_ref[0] * (1.0 / l_ref[0]) - lam * (acc_ref[1] * (1.0 / l_ref[1]))
        ms = jnp.mean(o * o, axis=0, keepdims=True)
        on_ref[...] = o * lax.rsqrt(ms + NORM_EPS)
        return ms

    all_tiles(True)
    ms = combine()
    finite = jnp.logical_and(jnp.all(jnp.isfinite(l_ref[...])), jnp.all(jnp.isfinite(ms)))

    @pl.when(jnp.logical_not(finite))
    def _():
        all_tiles(False)
        combine()

    o_ref[0] = (on_ref[...].T * sw_ref[...] * (1.0 - lam0) * _silu(g_ref[0])).astype(BF16)


def _toeplitz(fn, rows, cols):
    period = rows + cols
    d = jnp.arange(period, dtype=jnp.int32)
    g = jnp.moveaxis(fn(jnp.where(d < cols, d, d - period)), 0, -1)
    x = jnp.tile(g, rows)[..., :rows * (period - 1)].reshape(g.shape[:-1] + (rows, period - 1))
    return x[..., :cols]


def _prompt_bias_rows(rel_table, tile):
    assert tile >= REL_MAX_DIST and tile % CHUNK == 0
    table = rel_table.astype(F32) * LOG2E
    far = table[_rel_bucket(jnp.full((1,), -(tile + 1), jnp.int32))]
    d = jnp.arange(2 * tile, dtype=jnp.int32)
    dd = jnp.where(d < tile, d, d - 2 * tile)
    diag = table[_rel_bucket(-dd)] - far
    off = table[_rel_bucket(-dd - tile)] - far
    return jnp.transpose(jnp.stack([off, diag]), (0, 2, 1))[:, :, None, :]


def _prompt_attention(qkv, vt, p32, bias_rows, subln_w, lq1, lk1, lq2, lk2, *, tile, lam0):
    b, s, _ = qkv.shape
    hw = 2 * ATT_HEAD_DIM
    nh = N_ATT_HEADS
    kern = functools.partial(_prompt_attn_kernel, tile=tile, qblk=min(tile, 256), lam0=lam0)
    vec = pl.BlockSpec((1, ATT_HEAD_DIM), lambda bi, h, qi: (0, 0))
    return pl.pallas_call(
        kern,
        grid=(b, nh, s // tile),
        in_specs=[
            pl.BlockSpec((1, tile, hw), lambda bi, h, qi: (bi, qi, h)),
            pl.BlockSpec((1, s, hw), lambda bi, h, qi: (bi, 0, nh + h)),
            pl.BlockSpec((hw, s), lambda bi, h, qi: (h, bi)),
            pl.BlockSpec((1, tile, hw), lambda bi, h, qi: (bi, qi, COL_G // hw + h)),
            pl.BlockSpec((2, 1, 1, 2 * tile), lambda bi, h, qi: (0, h, 0, 0)),
            pl.BlockSpec((1, hw), lambda bi, h, qi: (0, 0)),
            vec, vec, vec, vec,
        ],
        out_specs=pl.BlockSpec((1, tile, hw), lambda bi, h, qi: (bi, qi, h)),
        out_shape=jax.ShapeDtypeStruct((b, s, D_ATT), BF16),
        scratch_shapes=[
            pltpu.VMEM((2, 1, tile), F32),
            pltpu.VMEM((2, 1, tile), F32),
            pltpu.VMEM((2, hw, tile), F32),
            pltpu.VMEM((hw, tile), BF16),
            pltpu.VMEM((hw, tile), F32),
            pltpu.VMEM((2, tile, tile), F32),
        ],
        compiler_params=pltpu.CompilerParams(
            dimension_semantics=("parallel", "parallel", "arbitrary"), vmem_limit_bytes=VMEM_LIMIT),
        name="prompt_attention",
    )(qkv, qkv, vt, p32, bias_rows, subln_w, lq1, lk1, lq2, lk2)


def _sample_attn_kernel(q_ref, kn_ref, vn_ref, kp_ref, vp_ref, g_ref, bp_ref, bn_ref, sw_ref,
                        lq1, lk1, lq2, lk2, o_ref, *, lam0):
    d = ATT_HEAD_DIM
    q = q_ref[0]
    kn = kn_ref[0]
    vn = vn_ref[0]
    kp = kp_ref[0]
    vp = vp_ref[0]
    nt = (((1,), (1,)), ((), ()))
    outs = []
    for mi in range(2):
        qm = q[:, mi * d:(mi + 1) * d]
        sp = lax.dot_general(qm, kp[:, mi * d:(mi + 1) * d], nt, preferred_element_type=F32)
        sn = lax.dot_general(qm, kn[:, mi * d:(mi + 1) * d], nt, preferred_element_type=F32)
        sp = sp + bp_ref[0]
        sn = sn + bn_ref[0]
        m = jnp.maximum(jnp.max(sp, axis=-1, keepdims=True), jnp.max(sn, axis=-1, keepdims=True))
        pp = jnp.exp2(sp - m)
        pn = jnp.exp2(sn - m)
        l = jnp.sum(pp, axis=-1, keepdims=True) + jnp.sum(pn, axis=-1, keepdims=True)
        acc = (jnp.dot(pp.astype(BF16), vp, preferred_element_type=F32)
               + jnp.dot(pn.astype(BF16), vn, preferred_element_type=F32))
        outs.append(acc / l)
    lam = _lam_from_refs(lq1, lk1, lq2, lk2, lam0)
    o_ref[0] = _attn_epilogue(outs[0], outs[1], lam, g_ref[0], sw_ref[...], lam0).astype(BF16)


def _sample_bias(rel_table, past_len, s):
    qpos = past_len + jnp.arange(s, dtype=jnp.int32)[:, None]
    kpos = jnp.arange(past_len + s, dtype=jnp.int32)[None, :]
    table = rel_table.astype(F32) * LOG2E
    bias = _toeplitz(lambda dd: table[_rel_bucket(dd - past_len)], s, past_len + s)
    visible = (kpos // CHUNK) <= (qpos // CHUNK)
    return jnp.where(visible[None], bias, MASK_VALUE)


def _sample_attention(qkv, p32, k_past, v_past, bias, subln_w, lq1, lk1, lq2, lk2, *, lam0):
    b, s, _ = qkv.shape
    past = k_past.shape[1]
    hw = 2 * ATT_HEAD_DIM
    nh = N_ATT_HEADS
    bias_p = bias[:, :, :past]
    bias_n = bias[:, :, past:]
    kern = functools.partial(_sample_attn_kernel, lam0=lam0)
    vec = pl.BlockSpec((1, ATT_HEAD_DIM), lambda bi, h: (0, 0))
    return pl.pallas_call(
        kern,
        grid=(b, nh),
        in_specs=[
            pl.BlockSpec((1, s, hw), lambda bi, h: (bi, 0, h)),
            pl.BlockSpec((1, s, hw), lambda bi, h: (bi, 0, nh + h)),
            pl.BlockSpec((1, s, hw), lambda bi, h: (bi, 0, 2 * nh + h)),
            pl.BlockSpec((1, past, hw), lambda bi, h: (bi, 0, h)),
            pl.BlockSpec((1, past, hw), lambda bi, h: (bi, 0, h)),
            pl.BlockSpec((1, s, hw), lambda bi, h: (bi, 0, COL_G // hw + h)),
            pl.BlockSpec((1, s, past), lambda bi, h: (h, 0, 0)),
            pl.BlockSpec((1, s, s), lambda bi, h: (h, 0, 0)),
            pl.BlockSpec((1, hw), lambda bi, h: (0, 0)),
            vec, vec, vec, vec,
        ],
        out_specs=pl.BlockSpec((1, s, hw), lambda bi, h: (bi, 0, h)),
        out_shape=jax.ShapeDtypeStruct((b, s, D_ATT), BF16),
        compiler_params=pltpu.CompilerParams(
            dimension_semantics=("parallel", "parallel"), vmem_limit_bytes=VMEM_LIMIT),
        name="sample_attention",
    )(qkv, qkv, qkv, k_past, v_past, p32, bias_p, bias_n, subln_w, lq1, lk1, lq2, lk2)


def _ssd_kernel(z_ref, xs_ref, b_ref, c_ref, dtT_ref, hist_ref, h0_ref, cw_ref, cb_ref, dtb_ref,
                alog_ref, dskip_ref, nw_ref, y_ref, hout_ref, h_s, xpad_s, xT_s, yT_s, *, valid, L):
    c = pl.program_id(1)
    n_rows = xs_ref.shape[1]

    @pl.when(c == 0)
    def _():
        h_s[...] = h0_ref[0]
        xpad_s[:SUBLANES, :] = hist_ref[0]

    xpad_s[SUBLANES:, 0:D_SSM] = xs_ref[0]
    xpad_s[SUBLANES:, D_SSM:D_SSM + D_BC] = b_ref[0]
    xpad_s[SUBLANES:, D_SSM + D_BC:D_CONV] = c_ref[0]

    s_idx = lax.broadcasted_iota(jnp.int32, (L, L), 0)
    t_idx = lax.broadcasted_iota(jnp.int32, (L, L), 1)
    causal = s_idx <= t_idx
    upper = jnp.where(causal, 1.0, 0.0).astype(BF16)

    for r0 in range(0, n_rows, L):
        _ssd_chunk(r0, L, valid, causal, upper, z_ref, dtT_ref, cw_ref, cb_ref, dtb_ref, alog_ref, dskip_ref,
                   nw_ref, y_ref, h_s, xpad_s, xT_s, yT_s)
    xpad_s[:SUBLANES, :] = xpad_s[n_rows:, :]

    @pl.when(c == pl.num_programs(1) - 1)
    def _():
        hout_ref[0] = h_s[...]


def _ssd_chunk(r0, L, valid, causal, upper, z_ref, dtT_ref, cw_ref, cb_ref, dtb_ref, alog_ref, dskip_ref,
               nw_ref, y_ref, h_s, xpad_s, xT_s, yT_s):
    P, N, R = SSM_HEAD_DIM, SSM_STATE, HEADS_PER_GROUP

    def conv_silu(lo, hi):
        acc = cb_ref[:, lo:hi]
        for k in range(CONV_WIDTH):
            acc = acc + (xpad_s[r0 + SUBLANES - k:r0 + SUBLANES - k + L, lo:hi]
                         * cw_ref[CONV_WIDTH - 1 - k:CONV_WIDTH - k, lo:hi])
        return _silu(acc)

    xs = conv_silu(0, D_SSM)
    bm = conv_silu(D_SSM, D_SSM + D_BC)
    cm = conv_silu(D_SSM + D_BC, D_CONV)

    dt = _softplus(dtT_ref[0, :, r0:r0 + L] + dtb_ref[...])
    if valid < r0 + L:
        dt = jnp.where(lax.broadcasted_iota(jnp.int32, dt.shape, 1) < valid - r0, dt, 0.0)
    a = dt * (-jnp.exp(alog_ref[...]))
    a1 = a.astype(BF16)
    r1 = a - a1.astype(F32)
    a2 = r1.astype(BF16)
    a3 = (r1 - a2.astype(F32)).astype(BF16)
    acs = (jnp.dot(a1, upper, preferred_element_type=F32)
           + jnp.dot(a2, upper, preferred_element_type=F32)
           + jnp.dot(a3, upper, preferred_element_type=F32))
    tot = acs[:, L - 1:L]
    e_row = jnp.exp(acs)
    w_row = dt * jnp.exp(tot - acs)
    d_row = jnp.broadcast_to(jnp.exp(tot), (N_SSM_HEADS, N))
    acs_col = jnp.concatenate([acs, jnp.zeros((L - N_SSM_HEADS, L), F32)], axis=0).T

    xT_s[...] = xs.T

    for g in range(N_SSM_GROUPS):
        bg = bm[:, g * N:(g + 1) * N].astype(BF16)
        cgT = cm[:, g * N:(g + 1) * N].T.astype(BF16)
        cbT = jnp.dot(bg, cgT, preferred_element_type=F32)
        for r8 in range(R):
            r = g * R + r8
            rows = slice(r * P, (r + 1) * P)
            seg = acs[r:r + 1, :] - acs_col[:, r:r + 1]
            decay = jnp.exp(jnp.where(causal, seg, MASK_VALUE))
            mT = (cbT * decay).astype(BF16)
            xr = xT_s[rows, :]
            hr = h_s[rows, :]
            y_intra = jnp.dot((xr * dt[r:r + 1, :]).astype(BF16), mT, preferred_element_type=F32)
            y_inter = jnp.dot(hr.astype(BF16), cgT, preferred_element_type=F32)
            yT_s[rows, :] = y_intra + y_inter * e_row[r:r + 1, :]
            upd = jnp.dot((xr * w_row[r:r + 1, :]).astype(BF16), bg, preferred_element_type=F32)
            h_s[rows, :] = hr * d_row[r:r + 1, :] + upd

    y = yT_s[...].T
    y = y + dskip_ref[...] * xs
    y = y * _silu(z_ref[0, r0:r0 + L])
    gs = D_SSM // N_SSM_GROUPS
    outs = []
    for g in range(N_SSM_GROUPS):
        yg = y[:, g * gs:(g + 1) * gs]
        ms = jnp.mean(yg * yg, axis=-1, keepdims=True)
        outs.append(yg * lax.rsqrt(ms + NORM_EPS) * nw_ref[:, g * gs:(g + 1) * gs])
    y_ref[0, r0:r0 + L] = jnp.concatenate(outs, axis=-1).astype(BF16)


def _ssd(src, cols, dtT, hist8, h0, conv_w, conv_b, dt_bias, a_log, d_full, norm_w, *, valid):
    b, s, _ = src.shape
    chunk = SSD_L
    L = 2 * chunk if s % (2 * chunk) == 0 else chunk
    col_z, col_xs, col_b, col_c = cols
    assert valid == s or s == L
    kern = functools.partial(_ssd_kernel, valid=min(valid, L), L=chunk)
    const2 = lambda shape: pl.BlockSpec(shape, lambda bi, c: (0, 0))
    hp = N_SSM_HEADS * SSM_HEAD_DIM
    return pl.pallas_call(
        kern,
        grid=(b, s // L),
        in_specs=[
            pl.BlockSpec((1, L, D_SSM), lambda bi, c: (bi, c, col_z // D_SSM)),
            pl.BlockSpec((1, L, D_SSM), lambda bi, c: (bi, c, col_xs // D_SSM)),
            pl.BlockSpec((1, L, D_BC), lambda bi, c: (bi, c, col_b // D_BC)),
            pl.BlockSpec((1, L, D_BC), lambda bi, c: (bi, c, col_c // D_BC)),
            pl.BlockSpec((1, N_SSM_HEADS, L), lambda bi, c: (bi, 0, c)),
            pl.BlockSpec((1, SUBLANES, D_CONV), lambda bi, c: (bi, 0, 0)),
            pl.BlockSpec((1, hp, SSM_STATE), lambda bi, c: (bi, 0, 0)),
            const2((CONV_WIDTH, D_CONV)),
            const2((1, D_CONV)),
            const2((N_SSM_HEADS, 1)),
            const2((N_SSM_HEADS, 1)),
            const2((1, D_SSM)),
            const2((1, D_SSM)),
        ],
        out_specs=[
            pl.BlockSpec((1, L, D_SSM), lambda bi, c: (bi, c, 0)),
            pl.BlockSpec((1, hp, SSM_STATE), lambda bi, c: (bi, 0, 0)),
        ],
        out_shape=[
            jax.ShapeDtypeStruct((b, s, D_SSM), BF16),
            jax.ShapeDtypeStruct((b, hp, SSM_STATE), F32),
        ],
        scratch_shapes=[
            pltpu.VMEM((hp, SSM_STATE), F32),
            pltpu.VMEM((SUBLANES + L, D_CONV), F32),
            pltpu.VMEM((D_SSM, chunk), F32),
            pltpu.VMEM((D_SSM, chunk), F32),
        ],
        compiler_params=pltpu.CompilerParams(
            dimension_semantics=("parallel", "arbitrary"), vmem_limit_bytes=VMEM_LIMIT),
        name="ssd",
    )(src, src, src, src, dtT, hist8, h0, conv_w, conv_b, dt_bias, a_log, d_full, norm_w)


def _outproj_kernel(a_ref, y_ref, x_ref, w1_ref, w2_ref, fw_ref, o_ref):
    acc = (jnp.dot(a_ref[...], w1_ref[...], preferred_element_type=F32)
           + jnp.dot(y_ref[...], w2_ref[...], preferred_element_type=F32))
    h = x_ref[...] + acc
    ms = jnp.mean(h * h, axis=-1, keepdims=True)
    o_ref[...] = h * lax.rsqrt(ms + NORM_EPS) * fw_ref[...]


def _out_proj(att, y, x2d, w1, w2, final_w, *, tm):
    m = x2d.shape[0]
    const = pl.BlockSpec((D_ATT, D_MODEL), lambda i: (0, 0), pipeline_mode=pl.Buffered(1))
    return pl.pallas_call(
        _outproj_kernel,
        grid=(m // tm,),
        in_specs=[
            pl.BlockSpec((tm, D_ATT), lambda i: (i, 0)),
            pl.BlockSpec((tm, D_SSM), lambda i: (i, 0)),
            pl.BlockSpec((tm, D_MODEL), lambda i: (i, 0)),
            const, const,
            pl.BlockSpec((1, D_MODEL), lambda i: (0, 0)),
        ],
        out_specs=pl.BlockSpec((tm, D_MODEL), lambda i: (i, 0)),
        out_shape=jax.ShapeDtypeStruct((m, D_MODEL), F32),
        compiler_params=pltpu.CompilerParams(
            dimension_semantics=("parallel",), vmem_limit_bytes=VMEM_LIMIT),
        name="out_proj",
    )(att, y, x2d, w1, w2, final_w)


def _merge_cache_kernel(k_ref, v_ref, ko_ref, vo_ref, *, nk, nv):
    tm = ko_ref.shape[0]
    for c in range(nk):
        ko_ref[:, c * LANES:(c + 1) * LANES] = k_ref[pl.ds(c, tm, stride=nk), :].astype(BF16)
    for half in range(v_ref.shape[2] // LANES):
        by_head = pltpu.einshape("mhd->hmd", v_ref[:, :, half * LANES:(half + 1) * LANES])
        for h in range(nv):
            lo = (2 * h + half) * LANES
            vo_ref[:, lo:lo + LANES] = by_head[h].astype(BF16)


def _merge_cache(k3, v3, *, tm):
    rows, nk, _ = k3.shape
    nv = v3.shape[1]
    assert v3.shape[2] == 2 * LANES
    out = jax.ShapeDtypeStruct((rows, D_ATT), BF16)
    return pl.pallas_call(
        functools.partial(_merge_cache_kernel, nk=nk, nv=nv),
        grid=(rows // tm,),
        in_specs=[pl.BlockSpec((tm * nk, LANES), lambda i: (i, 0)),
                  pl.BlockSpec((tm, nv, 2 * LANES), lambda i: (i, 0, 0))],
        out_specs=[pl.BlockSpec((tm, D_ATT), lambda i: (i, 0)), pl.BlockSpec((tm, D_ATT), lambda i: (i, 0))],
        out_shape=[out, out],
        compiler_params=pltpu.CompilerParams(
            dimension_semantics=("parallel",), vmem_limit_bytes=VMEM_LIMIT),
        name="merge_cache",
    )(k3.reshape(rows * nk, LANES), v3)


def _tiles(m):
    tm = min(m, 1024)
    assert m % tm == 0
    return tm


def _layer(h, k_past, v_past, conv_past, ssm_past, layer, rel_bias, norm_w, w_main, w_dt, lq1, lk1, lq2, lk2,
           subln_w, conv_w, conv_b, dt_bias, a_log, d_full, ssm_norm_w, w_out1, w_out2, out_norm_w):
    b, s, _ = h.shape
    m = b * s
    lam0 = _lambda_init(layer)
    x2d = h.reshape(m, D_MODEL)
    p32, k_new, v_new, qkv, vt, dt_raw = _in_proj(x2d, norm_w, w_main, w_dt, tm=_tiles(m))
    k_new = k_new.reshape(b, s, N_ATT_HEADS, 2, ATT_HEAD_DIM)
    v_new = v_new.reshape(b, s, N_ATT_HEADS, 2 * ATT_HEAD_DIM)
    p32 = p32.reshape(b, s, D_P32)
    qkv = qkv.reshape(b, s, D_QKV)

    if k_past is None:
        tile = min(s, 512)
        att = _prompt_attention(qkv, vt, p32, _prompt_bias_rows(rel_bias, tile), subln_w, lq1, lk1, lq2, lk2,
                                tile=tile, lam0=lam0)
    else:
        past = k_past.shape[1]
        kp, vp = _merge_cache(k_past.reshape(b * past, D_ATT // LANES, LANES),
                              v_past.reshape(b * past, N_ATT_HEADS, 2 * ATT_HEAD_DIM), tm=min(b * past, 512))
        att = _sample_attention(qkv, p32, kp.reshape(b, past, D_ATT), vp.reshape(b, past, D_ATT),
                                _sample_bias(rel_bias, past, s), subln_w, lq1, lk1, lq2, lk2, lam0=lam0)

    assert s >= CONV_WIDTH - 1
    conv_new = p32[:, s - (CONV_WIDTH - 1):, COL_XS:COL_XS + D_CONV]
    hist8 = jnp.pad(conv_past, ((0, 0), (SUBLANES - (CONV_WIDTH - 1), 0), (0, 0)))
    dtT = jnp.transpose(dt_raw[:, :N_SSM_HEADS].reshape(b, s, N_SSM_HEADS), (0, 2, 1))
    h0 = ssm_past.reshape(b, N_SSM_HEADS * SSM_HEAD_DIM, SSM_STATE)
    if s % SSD_L == 0:
        src, cols = p32, (COL_Z, COL_XS, COL_B, COL_C)
    else:
        assert s < SSD_L
        pad = SSD_L - s
        src = jnp.pad(p32[:, :, COL_Z:], ((0, 0), (0, pad), (0, 0)))
        cols = (0, D_SSM, 2 * D_SSM, 2 * D_SSM + D_BC)
        dtT = jnp.pad(dtT, ((0, 0), (0, 0), (0, pad)))
    y, ssm_new = _ssd(src, cols, dtT, hist8, h0, conv_w, conv_b, dt_bias, a_log, d_full, ssm_norm_w,
                      valid=s)
    y = y[:, :s].reshape(m, D_SSM)
    ssm_new = ssm_new.reshape(b, N_SSM_HEADS, SSM_HEAD_DIM, SSM_STATE)

    out = _out_proj(att.reshape(m, D_ATT), y, x2d, w_out1, w_out2, out_norm_w, tm=min(m, 512))
    return out.reshape(b, s, D_MODEL), k_new, v_new, conv_new, ssm_new


def kernel(x_prompt, x_sample, cache_k, cache_v, cache_conv, state_ssm, rel_bias, norm_w, w_in, lambda_q1,
           lambda_k1, lambda_q2, lambda_k2, subln_w, conv_w, conv_b, dt_bias, A_log, D_skip, ssm_norm_w, w_out,
           final_norm_w):
    depth = w_in.shape[0]
    assert depth == 1, "the final norm is fused into the (single) layer's output projection"
    bp = x_prompt.shape[0]
    l = 0
    col_q = 0
    col_dt = 4 * D_ATT + D_SSM + D_CONV
    w = w_in[l]
    w_main = jnp.transpose(w[:, col_q:col_dt].astype(BF16).reshape(D_MODEL, -1, IN_PROJ_TN), (1, 0, 2))
    w_dt = jnp.pad(w[:, col_dt:], ((0, 0), (0, LANES - N_SSM_HEADS))).astype(BF16)
    row = lambda t: t.reshape(1, -1).astype(F32)
    col = lambda t: t.reshape(-1, 1).astype(F32)
    params = (rel_bias, row(norm_w[l]), w_main, w_dt, row(lambda_q1[l]), row(lambda_k1[l]), row(lambda_q2[l]),
              row(lambda_k2[l]), row(subln_w[l]), conv_w[l].astype(F32), row(conv_b[l]), col(dt_bias[l]),
              col(A_log[l]), row(jnp.repeat(D_skip[l], SSM_HEAD_DIM)), row(ssm_norm_w[l]),
              w_out[l, :D_ATT].astype(BF16), w_out[l, D_ATT:].astype(BF16), row(final_norm_w))
    conv0 = jnp.zeros((bp, CONV_WIDTH - 1, D_CONV), x_prompt.dtype)
    ssm0 = jnp.zeros((bp, N_SSM_HEADS, SSM_HEAD_DIM, SSM_STATE), state_ssm.dtype)
    yp, k1, v1, c1, s1 = _layer(x_prompt, None, None, conv0, ssm0, l, *params)
    ys, k2, v2, c2, s2 = _layer(x_sample, cache_k[l], cache_v[l], cache_conv[l], state_ssm[l], l, *params)
    return (yp, ys, k1[None], v1[None], c1[None], s1[None], k2[None], v2[None], c2[None], s2[None])
```

```python
import functools
import math

import jax
import jax.numpy as jnp
from jax import lax
from jax.experimental import pallas as pl
from jax.experimental.pallas import tpu as pltpu

F32 = jnp.float32
BF16 = jnp.bfloat16

D_MODEL = 2048
CHUNK = 64
NORM_EPS = 1e-5
N_ATT_HEADS = 8
ATT_HEAD_DIM = 128
D_ATT = N_ATT_HEADS * 2 * ATT_HEAD_DIM
N_REL_BUCKETS = 32
REL_MAX_DIST = 128
D_SSM = 2048
SSM_HEAD_DIM = 64
N_SSM_HEADS = D_SSM // SSM_HEAD_DIM
N_SSM_GROUPS = 4
HEADS_PER_GROUP = N_SSM_HEADS // N_SSM_GROUPS
SSM_STATE = 128
CONV_WIDTH = 4
D_BC = N_SSM_GROUPS * SSM_STATE
D_CONV = D_SSM + 2 * D_BC
D_MIX = D_ATT + D_SSM
D_QKV = 3 * D_ATT
D_P32 = D_ATT + D_SSM + D_CONV
COL_G, COL_Z, COL_XS = 0, D_ATT, D_ATT + D_SSM
COL_B, COL_C = COL_XS + D_SSM, COL_XS + D_SSM + D_BC

LANES = 128
SUBLANES = 8
VMEM_LIMIT = 56 * 1024 * 1024
MASK_VALUE = -1e30
LOG2E = math.log2(math.e)
Q_SCALE = ATT_HEAD_DIM ** -0.5 * LOG2E
SSD_L = 128
IN_PROJ_TN = 1024
W_SLOTS = 3


def _silu(x):
    return x * (1.0 / (1.0 + jnp.exp(-x)))


def _softplus(x):
    return jnp.maximum(x, 0.0) + jnp.log1p(jnp.exp(-jnp.abs(x)))


def _rms_to_scratch(x_ref, nw_ref, u_ref, slab):
    def body(r, carry):
        rows = pl.ds(pl.multiple_of(r * slab, slab), slab)
        x = x_ref[rows, :]
        ms = jnp.mean(x * x, axis=-1, keepdims=True)
        u = x * lax.rsqrt(ms + NORM_EPS) * nw_ref[...]
        u_ref[rows, :] = u.astype(BF16)
        return carry
    lax.fori_loop(0, x_ref.shape[0] // slab, body, 0)


def _inproj_qkv_kernel(x_ref, nw_ref, w_ref, k_ref, v_ref, qkv_ref, vt_ref, u_ref, *, n_q_tiles, slab):
    j = pl.program_id(1)

    @pl.when(j == 0)
    def _():
        _rms_to_scratch(x_ref, nw_ref, u_ref, slab)

    def project():
        return jnp.dot(u_ref[...], w_ref[0], preferred_element_type=F32)

    @pl.when(j < n_q_tiles)
    def _():
        qkv_ref[...] = (project() * Q_SCALE).astype(BF16)

    @pl.when(jnp.logical_and(j >= n_q_tiles, j < 2 * n_q_tiles))
    def _():
        res = project()
        k_ref[...] = pltpu.einshape("m(cd)->mcd", res, d=LANES)
        qkv_ref[...] = res.astype(BF16)

    @pl.when(j >= 2 * n_q_tiles)
    def _():
        res = project()
        v_ref[...] = res
        qkv_ref[...] = res.astype(BF16)
        vt_ref[...] = res.T.astype(BF16)


def _inproj_rest_kernel(x_ref, nw_ref, w_hbm, wdt_ref, p_ref, dt_ref, u_ref, wbuf, sem, *, slab, first_tile):
    nj = pl.num_programs(1)
    total = pl.num_programs(0) * nj
    t = pl.program_id(0) * nj + pl.program_id(1)

    def tile_copy(step):
        slot = step % W_SLOTS
        return pltpu.make_async_copy(w_hbm.at[first_tile + step % nj], wbuf.at[slot], sem.at[slot])

    @pl.when(t == 0)
    def _():
        for step in range(W_SLOTS - 1):
            tile_copy(step).start()

    @pl.when(t + (W_SLOTS - 1) < total)
    def _():
        tile_copy(t + (W_SLOTS - 1)).start()

    @pl.when(pl.program_id(1) == 0)
    def _():
        _rms_to_scratch(x_ref, nw_ref, u_ref, slab)
        dt_ref[...] = jnp.dot(u_ref[...], wdt_ref[...], preferred_element_type=F32)

    tile_copy(t).wait()
    p_ref[...] = jnp.dot(u_ref[...], wbuf[t % W_SLOTS], preferred_element_type=F32)


def _in_proj(x2d, norm_w, w_tiles, w_dt, *, tm):
    m = x2d.shape[0]
    n_tiles, _, tn = w_tiles.shape
    n_q_tiles = D_ATT // tn
    n_qkv_tiles = 3 * n_q_tiles
    slab = min(tm, 256)
    clamp = lambda j, lo: jnp.clip(j - lo * n_q_tiles, 0, n_q_tiles - 1)
    params = pltpu.CompilerParams(dimension_semantics=("parallel", "arbitrary"), vmem_limit_bytes=VMEM_LIMIT)
    x_spec = pl.BlockSpec((tm, D_MODEL), lambda i, j: (i, 0))
    nw_spec = pl.BlockSpec((1, D_MODEL), lambda i, j: (0, 0))
    k_new, v_new, qkv, vt = pl.pallas_call(
        functools.partial(_inproj_qkv_kernel, n_q_tiles=n_q_tiles, slab=slab),
        grid=(m // tm, n_qkv_tiles),
        in_specs=[x_spec, nw_spec, pl.BlockSpec((1, D_MODEL, tn), lambda i, j: (j, 0, 0))],
        out_specs=[
            pl.BlockSpec((tm, tn // LANES, LANES), lambda i, j: (i, clamp(j, 1), 0)),
            pl.BlockSpec((tm, tn), lambda i, j: (i, clamp(j, 2))),
            pl.BlockSpec((tm, tn), lambda i, j: (i, j)),
            pl.BlockSpec((tn, tm), lambda i, j: (clamp(j, 2), i)),
        ],
        out_shape=[
            jax.ShapeDtypeStruct((m, D_ATT // LANES, LANES), F32),
            jax.ShapeDtypeStruct((m, D_ATT), F32),
            jax.ShapeDtypeStruct((m, D_QKV), BF16),
            jax.ShapeDtypeStruct((D_ATT, m), BF16),
        ],
        scratch_shapes=[pltpu.VMEM((tm, D_MODEL), BF16)],
        compiler_params=params,
        name="in_proj_qkv",
    )(x2d, norm_w, w_tiles)
    assert (m // tm) * (n_tiles - n_qkv_tiles) >= W_SLOTS - 1
    p32, dt_raw = pl.pallas_call(
        functools.partial(_inproj_rest_kernel, slab=slab, first_tile=n_qkv_tiles),
        grid=(m // tm, n_tiles - n_qkv_tiles),
        in_specs=[x_spec, nw_spec, pl.BlockSpec(memory_space=pl.ANY),
                  pl.BlockSpec((D_MODEL, LANES), lambda i, j: (0, 0))],
        out_specs=[
            pl.BlockSpec((tm, tn), lambda i, j: (i, j)),
            pl.BlockSpec((tm, LANES), lambda i, j: (i, 0)),
        ],
        out_shape=[
            jax.ShapeDtypeStruct((m, D_P32), F32),
            jax.ShapeDtypeStruct((m, LANES), F32),
        ],
        scratch_shapes=[pltpu.VMEM((tm, D_MODEL), BF16), pltpu.VMEM((W_SLOTS, D_MODEL, tn), BF16),
                        pltpu.SemaphoreType.DMA((W_SLOTS,))],
        compiler_params=pltpu.CompilerParams(dimension_semantics=("arbitrary", "arbitrary"),
                                             vmem_limit_bytes=VMEM_LIMIT),
        name="in_proj_rest",
    )(x2d, norm_w, w_tiles, w_dt)
    return p32, k_new, v_new, qkv, vt, dt_raw


def _rel_bucket(rel):
    half = N_REL_BUCKETS // 2
    max_exact = half // 2
    ret = jnp.where(rel > 0, half, 0)
    n = jnp.abs(rel)
    nf = jnp.maximum(n, 1).astype(F32)
    large = max_exact + (jnp.log(nf / max_exact) / math.log(REL_MAX_DIST / max_exact)
                         * (half - max_exact)).astype(jnp.int32)
    large = jnp.minimum(large, half - 1)
    return ret + jnp.where(n < max_exact, n, large)


def _lambda_init(layer):
    return 0.8 - 0.6 * math.exp(-0.3 * layer)


def _lam_from_refs(lq1, lk1, lq2, lk2, lam0):
    return (jnp.exp(jnp.sum(lq1[...] * lk1[...], axis=-1, keepdims=True))
            - jnp.exp(jnp.sum(lq2[...] * lk2[...], axis=-1, keepdims=True)) + lam0)


def _attn_epilogue(a1, a2, lam, g, subln_w, lam0):
    o = a1 - lam * a2
    ms = jnp.mean(o * o, axis=-1, keepdims=True)
    o = o * lax.rsqrt(ms + NORM_EPS) * subln_w
    o = o * (1.0 - lam0)
    return o * _silu(g)


def _prompt_attn_kernel(q_ref, k_ref, vt_ref, g_ref, brow_ref, sw_ref, lq1, lk1, lq2, lk2,
                        o_ref, m_ref, l_ref, acc_ref, qt_ref, on_ref, bias_ref, *, tile, qblk, lam0):
    qi = pl.program_id(2)
    d = ATT_HEAD_DIM
    units = [(c, mi) for c in range(tile // qblk) for mi in range(2)]
    qt_ref[...] = q_ref[0].astype(F32).T.astype(BF16)

    @pl.when(qi == 0)
    def _():
        for t in range(2):
            rows = jnp.broadcast_to(brow_ref[t, 0], (tile, 2 * tile))
            bias_ref[t] = pltpu.roll(rows, 0, 1, stride=1, stride_axis=0)[:, :tile]
        kc = lax.broadcasted_iota(jnp.int32, (tile, tile), 0) // CHUNK
        qc = lax.broadcasted_iota(jnp.int32, (tile, tile), 1) // CHUNK
        bias_ref[1] = jnp.where(kc <= qc, bias_ref[1], MASK_VALUE)

    def reset(m_init):
        m_ref[...] = jnp.full(m_ref.shape, m_init, F32)
        l_ref[...] = jnp.zeros(l_ref.shape, F32)
        acc_ref[...] = jnp.zeros(acc_ref.shape, F32)

    def kv_span(j, n, bias_idx, lagged, diagonal=False):
        keys = pl.ds(pl.multiple_of(j * tile, tile), n * tile)
        k = k_ref[0, keys, :]
        vt_all = vt_ref[:, keys]
        n_keys = [(c + 1) * qblk if diagonal else n * tile for c, _ in units]
        ss = [jnp.dot(k[:nk, mi * d:(mi + 1) * d], qt_ref[mi * d:(mi + 1) * d, c * qblk:(c + 1) * qblk],
                      preferred_element_type=F32) for (c, mi), nk in zip(units, n_keys)]
        for (c, mi), nk, s in zip(units, n_keys, ss):
            cols = slice(c * qblk, (c + 1) * qblk)
            vt = vt_all[:, :nk]
            if bias_idx is not None:
                s = s + bias_ref[bias_idx, :nk, cols]
            m_old = m_ref[mi, :, cols]
            if lagged:
                p = jnp.exp2(s - m_old)
                l_new = l_ref[mi, :, cols] + jnp.sum(p, axis=0, keepdims=True)
                acc_new = acc_ref[mi, :, cols] + jnp.dot(vt, p.astype(BF16), preferred_element_type=F32)
                m_new = jnp.maximum(m_old, jnp.max(s, axis=0, keepdims=True))
                alpha = jnp.exp2(m_old - m_new)
                l_ref[mi, :, cols] = l_new * alpha
                acc_ref[mi, :, cols] = acc_new * alpha
            else:
                m_new = jnp.maximum(m_old, jnp.max(s, axis=0, keepdims=True))
                alpha = jnp.exp2(m_old - m_new)
                p = jnp.exp2(s - m_new)
                l_ref[mi, :, cols] = alpha * l_ref[mi, :, cols] + jnp.sum(p, axis=0, keepdims=True)
                pv = jnp.dot(vt, p.astype(BF16), preferred_element_type=F32)
                acc_ref[mi, :, cols] = alpha * acc_ref[mi, :, cols] + pv
            m_ref[mi, :, cols] = m_new

    n_far = jnp.maximum(qi - 1, 0)
    n_quads = n_far // 4
    n_pairs = n_far // 2

    def all_tiles(lagged):
        reset(0.0 if lagged else MASK_VALUE)

        kv_span(qi, 1, 1, lagged, diagonal=True)

        def off_body(j, carry):
            kv_span(j, 1, 0, lagged)
            return carry
        lax.fori_loop(n_far, qi, off_body, 0)

        def far_body(width):
            def body(i, carry):
                kv_span(width * i, width, None, lagged)
                return carry
            return body
        lax.fori_loop(0, n_quads, far_body(4), 0)
        lax.fori_loop(2 * n_quads, n_pairs, far_body(2), 0)
        lax.fori_loop(2 * n_pairs, n_far, far_body(1), 0)

    lam = _lam_from_refs(lq1, lk1, lq2, lk2, lam0)

    def combine():
        o = acc_ref[0] * (1.0 / l_ref[0]) - lam * (acc_ref[1] * (1.0 / l_ref[1]))
        ms = jnp.mean(o * o, axis=0, keepdims=True)
        on_ref[...] = o * lax.rsqrt(ms + NORM_EPS)
        return ms

    all_tiles(True)
    ms = combine()
    finite = jnp.logical_and(jnp.all(jnp.isfinite(l_ref[...])), jnp.all(jnp.isfinite(ms)))

    @pl.when(jnp.logical_not(finite))
    def _():
        all_tiles(False)
        combine()

    o_ref[0] = (on_ref[...].T * sw_ref[...] * (1.0 - lam0) * _silu(g_ref[0])).astype(BF16)


def _toeplitz(fn, rows, cols):
    period = rows + cols
    d = jnp.arange(period, dtype=jnp.int32)
    g = jnp.moveaxis(fn(jnp.where(d < cols, d, d - period)), 0, -1)
    x = jnp.tile(g, rows)[..., :rows * (period - 1)].reshape(g.shape[:-1] + (rows, period - 1))
    return x[..., :cols]


def _prompt_bias_rows(rel_table, tile):
    assert tile >= REL_MAX_DIST and tile % CHUNK == 0
    table = rel_table.astype(F32) * LOG2E
    far = table[_rel_bucket(jnp.full((1,), -(tile + 1), jnp.int32))]
    d = jnp.arange(2 * tile, dtype=jnp.int32)
    dd = jnp.where(d < tile, d, d - 2 * tile)
    diag = table[_rel_bucket(-dd)] - far
    off = table[_rel_bucket(-dd - tile)] - far
    return jnp.transpose(jnp.stack([off, diag]), (0, 2, 1))[:, :, None, :]


def _prompt_attention(qkv, vt, p32, bias_rows, subln_w, lq1, lk1, lq2, lk2, *, tile, lam0):
    b, s, _ = qkv.shape
    hw = 2 * ATT_HEAD_DIM
    nh = N_ATT_HEADS
    kern = functools.partial(_prompt_attn_kernel, tile=tile, qblk=min(tile, 256), lam0=lam0)
    vec = pl.BlockSpec((1, ATT_HEAD_DIM), lambda bi, h, qi: (0, 0))
    return pl.pallas_call(
        kern,
        grid=(b, nh, s // tile),
        in_specs=[
            pl.BlockSpec((1, tile, hw), lambda bi, h, qi: (bi, qi, h)),
            pl.BlockSpec((1, s, hw), lambda bi, h, qi: (bi, 0, nh + h)),
            pl.BlockSpec((hw, s), lambda bi, h, qi: (h, bi)),
            pl.BlockSpec((1, tile, hw), lambda bi, h, qi: (bi, qi, COL_G // hw + h)),
            pl.BlockSpec((2, 1, 1, 2 * tile), lambda bi, h, qi: (0, h, 0, 0)),
            pl.BlockSpec((1, hw), lambda bi, h, qi: (0, 0)),
            vec, vec, vec, vec,
        ],
        out_specs=pl.BlockSpec((1, tile, hw), lambda bi, h, qi: (bi, qi, h)),
        out_shape=jax.ShapeDtypeStruct((b, s, D_ATT), BF16),
        scratch_shapes=[
            pltpu.VMEM((2, 1, tile), F32),
            pltpu.VMEM((2, 1, tile), F32),
            pltpu.VMEM((2, hw, tile), F32),
            pltpu.VMEM((hw, tile), BF16),
            pltpu.VMEM((hw, tile), F32),
            pltpu.VMEM((2, tile, tile), F32),
        ],
        compiler_params=pltpu.CompilerParams(
            dimension_semantics=("parallel", "parallel", "arbitrary"), vmem_limit_bytes=VMEM_LIMIT),
        name="prompt_attention",
    )(qkv, qkv, vt, p32, bias_rows, subln_w, lq1, lk1, lq2, lk2)


def _sample_attn_kernel(q_ref, kn_ref, vn_ref, kp_ref, vp_ref, g_ref, bp_ref, bn_ref, sw_ref,
                        lq1, lk1, lq2, lk2, o_ref, *, lam0):
    d = ATT_HEAD_DIM
    q = q_ref[0]
    kn = kn_ref[0]
    vn = vn_ref[0]
    kp = kp_ref[0]
    vp = vp_ref[0]
    nt = (((1,), (1,)), ((), ()))
    outs = []
    for mi in range(2):
        qm = q[:, mi * d:(mi + 1) * d]
        sp = lax.dot_general(qm, kp[:, mi * d:(mi + 1) * d], nt, preferred_element_type=F32)
        sn = lax.dot_general(qm, kn[:, mi * d:(mi + 1) * d], nt, preferred_element_type=F32)
        sp = sp + bp_ref[0]
        sn = sn + bn_ref[0]
        m = jnp.maximum(jnp.max(sp, axis=-1, keepdims=True), jnp.max(sn, axis=-1, keepdims=True))
        pp = jnp.exp2(sp - m)
        pn = jnp.exp2(sn - m)
        l = jnp.sum(pp, axis=-1, keepdims=True) + jnp.sum(pn, axis=-1, keepdims=True)
        acc = (jnp.dot(pp.astype(BF16), vp, preferred_element_type=F32)
               + jnp.dot(pn.astype(BF16), vn, preferred_element_type=F32))
        outs.append(acc / l)
    lam = _lam_from_refs(lq1, lk1, lq2, lk2, lam0)
    o_ref[0] = _attn_epilogue(outs[0], outs[1], lam, g_ref[0], sw_ref[...], lam0).astype(BF16)


def _sample_bias(rel_table, past_len, s):
    qpos = past_len + jnp.arange(s, dtype=jnp.int32)[:, None]
    kpos = jnp.arange(past_len + s, dtype=jnp.int32)[None, :]
    table = rel_table.astype(F32) * LOG2E
    bias = _toeplitz(lambda dd: table[_rel_bucket(dd - past_len)], s, past_len + s)
    visible = (kpos // CHUNK) <= (qpos // CHUNK)
    return jnp.where(visible[None], bias, MASK_VALUE)


def _sample_attention(qkv, p32, k_past, v_past, bias, subln_w, lq1, lk1, lq2, lk2, *, lam0):
    b, s, _ = qkv.shape
    past = k_past.shape[1]
    hw = 2 * ATT_HEAD_DIM
    nh = N_ATT_HEADS
    bias_p = bias[:, :, :past]
    bias_n = bias[:, :, past:]
    kern = functools.partial(_sample_attn_kernel, lam0=lam0)
    vec = pl.BlockSpec((1, ATT_HEAD_DIM), lambda bi, h: (0, 0))
    return pl.pallas_call(
        kern,
        grid=(b, nh),
        in_specs=[
            pl.BlockSpec((1, s, hw), lambda bi, h: (bi, 0, h)),
            pl.BlockSpec((1, s, hw), lambda bi, h: (bi, 0, nh + h)),
            pl.BlockSpec((1, s, hw), lambda bi, h: (bi, 0, 2 * nh + h)),
            pl.BlockSpec((1, past, hw), lambda bi, h: (bi, 0, h)),
            pl.BlockSpec((1, past, hw), lambda bi, h: (bi, 0, h)),
            pl.BlockSpec((1, s, hw), lambda bi, h: (bi, 0, COL_G // hw + h)),
            pl.BlockSpec((1, s, past), lambda bi, h: (h, 0, 0)),
            pl.BlockSpec((1, s, s), lambda bi, h: (h, 0, 0)),
            pl.BlockSpec((1, hw), lambda bi, h: (0, 0)),
            vec, vec, vec, vec,
        ],
        out_specs=pl.BlockSpec((1, s, hw), lambda bi, h: (bi, 0, h)),
        out_shape=jax.ShapeDtypeStruct((b, s, D_ATT), BF16),
        compiler_params=pltpu.CompilerParams(
            dimension_semantics=("parallel", "parallel"), vmem_limit_bytes=VMEM_LIMIT),
        name="sample_attention",
    )(qkv, qkv, qkv, k_past, v_past, p32, bias_p, bias_n, subln_w, lq1, lk1, lq2, lk2)


def _ssd_kernel(z_ref, xs_ref, b_ref, c_ref, dtT_ref, hist_ref, h0_ref, cw_ref, cb_ref, dtb_ref,
                alog_ref, dskip_ref, nw_ref, y_ref, hout_ref, h_s, xpad_s, xT_s, yT_s, *, valid):
    c = pl.program_id(1)
    L = xs_ref.shape[1]
    P, N, R = SSM_HEAD_DIM, SSM_STATE, HEADS_PER_GROUP

    @pl.when(c == 0)
    def _():
        h_s[...] = h0_ref[0]
        xpad_s[:SUBLANES, :] = hist_ref[0]

    def conv_silu(x_ref, lo, hi):
        xpad_s[SUBLANES:, lo:hi] = x_ref[0]
        acc = cb_ref[:, lo:hi]
        for k in range(CONV_WIDTH):
            acc = acc + (xpad_s[SUBLANES - k:SUBLANES - k + L, lo:hi]
                         * cw_ref[CONV_WIDTH - 1 - k:CONV_WIDTH - k, lo:hi])
        return _silu(acc)

    xs = conv_silu(xs_ref, 0, D_SSM)
    bm = conv_silu(b_ref, D_SSM, D_SSM + D_BC)
    cm = conv_silu(c_ref, D_SSM + D_BC, D_CONV)
    xpad_s[:SUBLANES, :] = xpad_s[L:, :]

    dt = _softplus(dtT_ref[0] + dtb_ref[...])
    if valid < L:
        dt = jnp.where(lax.broadcasted_iota(jnp.int32, dt.shape, 1) < valid, dt, 0.0)
    a = dt * (-jnp.exp(alog_ref[...]))
    s_idx = lax.broadcasted_iota(jnp.int32, (L, L), 0)
    t_idx = lax.broadcasted_iota(jnp.int32, (L, L), 1)
    causal = s_idx <= t_idx
    upper = jnp.where(causal, 1.0, 0.0).astype(BF16)
    a1 = a.astype(BF16)
    r1 = a - a1.astype(F32)
    a2 = r1.astype(BF16)
    a3 = (r1 - a2.astype(F32)).astype(BF16)
    acs = (jnp.dot(a1, upper, preferred_element_type=F32)
           + jnp.dot(a2, upper, preferred_element_type=F32)
           + jnp.dot(a3, upper, preferred_element_type=F32))
    tot = acs[:, L - 1:L]
    e_row = jnp.exp(acs)
    w_row = dt * jnp.exp(tot - acs)
    d_row = jnp.broadcast_to(jnp.exp(tot), (N_SSM_HEADS, N))
    acs_col = jnp.concatenate([acs, jnp.zeros((L - N_SSM_HEADS, L), F32)], axis=0).T

    xT_s[...] = xs.T

    for g in range(N_SSM_GROUPS):
        bg = bm[:, g * N:(g + 1) * N].astype(BF16)
        cgT = cm[:, g * N:(g + 1) * N].T.astype(BF16)
        cbT = jnp.dot(bg, cgT, preferred_element_type=F32)
        for r8 in range(R):
            r = g * R + r8
            rows = slice(r * P, (r + 1) * P)
            seg = acs[r:r + 1, :] - acs_col[:, r:r + 1]
            decay = jnp.exp(jnp.where(causal, seg, MASK_VALUE))
            mT = (cbT * decay).astype(BF16)
            xr = xT_s[rows, :]
            hr = h_s[rows, :]
            y_intra = jnp.dot((xr * dt[r:r + 1, :]).astype(BF16), mT, preferred_element_type=F32)
            y_inter = jnp.dot(hr.astype(BF16), cgT, preferred_element_type=F32)
            yT_s[rows, :] = y_intra + y_inter * e_row[r:r + 1, :]
            upd = jnp.dot((xr * w_row[r:r + 1, :]).astype(BF16), bg, preferred_element_type=F32)
            h_s[rows, :] = hr * d_row[r:r + 1, :] + upd

    y = yT_s[...].T
    y = y + dskip_ref[...] * xs
    y = y * _silu(z_ref[0])
    gs = D_SSM // N_SSM_GROUPS
    outs = []
    for g in range(N_SSM_GROUPS):
        yg = y[:, g * gs:(g + 1) * gs]
        ms = jnp.mean(yg * yg, axis=-1, keepdims=True)
        outs.append(yg * lax.rsqrt(ms + NORM_EPS) * nw_ref[:, g * gs:(g + 1) * gs])
    y_ref[0] = jnp.concatenate(outs, axis=-1).astype(BF16)

    @pl.when(c == pl.num_programs(1) - 1)
    def _():
        hout_ref[0] = h_s[...]


def _ssd(src, cols, dtT, hist8, h0, conv_w, conv_b, dt_bias, a_log, d_full, norm_w, *, valid):
    b, s, _ = src.shape
    L = SSD_L
    col_z, col_xs, col_b, col_c = cols
    kern = functools.partial(_ssd_kernel, valid=valid)
    const2 = lambda shape: pl.BlockSpec(shape, lambda bi, c: (0, 0))
    hp = N_SSM_HEADS * SSM_HEAD_DIM
    return pl.pallas_call(
        kern,
        grid=(b, s // L),
        in_specs=[
            pl.BlockSpec((1, L, D_SSM), lambda bi, c: (bi, c, col_z // D_SSM)),
            pl.BlockSpec((1, L, D_SSM), lambda bi, c: (bi, c, col_xs // D_SSM)),
            pl.BlockSpec((1, L, D_BC), lambda bi, c: (bi, c, col_b // D_BC)),
            pl.BlockSpec((1, L, D_BC), lambda bi, c: (bi, c, col_c // D_BC)),
            pl.BlockSpec((1, N_SSM_HEADS, L), lambda bi, c: (bi, 0, c)),
            pl.BlockSpec((1, SUBLANES, D_CONV), lambda bi, c: (bi, 0, 0)),
            pl.BlockSpec((1, hp, SSM_STATE), lambda bi, c: (bi, 0, 0)),
            const2((CONV_WIDTH, D_CONV)),
            const2((1, D_CONV)),
            const2((N_SSM_HEADS, 1)),
            const2((N_SSM_HEADS, 1)),
            const2((1, D_SSM)),
            const2((1, D_SSM)),
        ],
        out_specs=[
            pl.BlockSpec((1, L, D_SSM), lambda bi, c: (bi, c, 0)),
            pl.BlockSpec((1, hp, SSM_STATE), lambda bi, c: (bi, 0, 0)),
        ],
        out_shape=[
            jax.ShapeDtypeStruct((b, s, D_SSM), BF16),
            jax.ShapeDtypeStruct((b, hp, SSM_STATE), F32),
        ],
        scratch_shapes=[
            pltpu.VMEM((hp, SSM_STATE), F32),
            pltpu.VMEM((SUBLANES + L, D_CONV), F32),
            pltpu.VMEM((D_SSM, L), F32),
            pltpu.VMEM((D_SSM, L), F32),
        ],
        compiler_params=pltpu.CompilerParams(
            dimension_semantics=("parallel", "arbitrary"), vmem_limit_bytes=VMEM_LIMIT),
        name="ssd",
    )(src, src, src, src, dtT, hist8, h0, conv_w, conv_b, dt_bias, a_log, d_full, norm_w)


def _outproj_kernel(a_ref, y_ref, x_ref, w1_ref, w2_ref, fw_ref, o_ref):
    acc = (jnp.dot(a_ref[...], w1_ref[...], preferred_element_type=F32)
           + jnp.dot(y_ref[...], w2_ref[...], preferred_element_type=F32))
    h = x_ref[...] + acc
    ms = jnp.mean(h * h, axis=-1, keepdims=True)
    o_ref[...] = h * lax.rsqrt(ms + NORM_EPS) * fw_ref[...]


def _out_proj(att, y, x2d, w1, w2, final_w, *, tm):
    m = x2d.shape[0]
    const = pl.BlockSpec((D_ATT, D_MODEL), lambda i: (0, 0), pipeline_mode=pl.Buffered(1))
    return pl.pallas_call(
        _outproj_kernel,
        grid=(m // tm,),
        in_specs=[
            pl.BlockSpec((tm, D_ATT), lambda i: (i, 0)),
            pl.BlockSpec((tm, D_SSM), lambda i: (i, 0)),
            pl.BlockSpec((tm, D_MODEL), lambda i: (i, 0)),
            const, const,
            pl.BlockSpec((1, D_MODEL), lambda i: (0, 0)),
        ],
        out_specs=pl.BlockSpec((tm, D_MODEL), lambda i: (i, 0)),
        out_shape=jax.ShapeDtypeStruct((m, D_MODEL), F32),
        compiler_params=pltpu.CompilerParams(
            dimension_semantics=("parallel",), vmem_limit_bytes=VMEM_LIMIT),
        name="out_proj",
    )(att, y, x2d, w1, w2, final_w)


def _merge_cache_kernel(k_ref, v_ref, ko_ref, vo_ref, *, nk, nv):
    tm = ko_ref.shape[0]
    for c in range(nk):
        ko_ref[:, c * LANES:(c + 1) * LANES] = k_ref[pl.ds(c, tm, stride=nk), :].astype(BF16)
    for half in range(v_ref.shape[2] // LANES):
        by_head = pltpu.einshape("mhd->hmd", v_ref[:, :, half * LANES:(half + 1) * LANES])
        for h in range(nv):
            lo = (2 * h + half) * LANES
            vo_ref[:, lo:lo + LANES] = by_head[h].astype(BF16)


def _merge_cache(k3, v3, *, tm):
    rows, nk, _ = k3.shape
    nv = v3.shape[1]
    assert v3.shape[2] == 2 * LANES
    out = jax.ShapeDtypeStruct((rows, D_ATT), BF16)
    return pl.pallas_call(
        functools.partial(_merge_cache_kernel, nk=nk, nv=nv),
        grid=(rows // tm,),
        in_specs=[pl.BlockSpec((tm * nk, LANES), lambda i: (i, 0)),
                  pl.BlockSpec((tm, nv, 2 * LANES), lambda i: (i, 0, 0))],
        out_specs=[pl.BlockSpec((tm, D_ATT), lambda i: (i, 0)), pl.BlockSpec((tm, D_ATT), lambda i: (i, 0))],
        out_shape=[out, out],
        compiler_params=pltpu.CompilerParams(
            dimension_semantics=("parallel",), vmem_limit_bytes=VMEM_LIMIT),
        name="merge_cache",
    )(k3.reshape(rows * nk, LANES), v3)


def _tiles(m):
    tm = min(m, 1024)
    assert m % tm == 0
    return tm


def _layer(h, k_past, v_past, conv_past, ssm_past, layer, rel_bias, norm_w, w_main, w_dt, lq1, lk1, lq2, lk2,
           subln_w, conv_w, conv_b, dt_bias, a_log, d_full, ssm_norm_w, w_out1, w_out2, out_norm_w):
    b, s, _ = h.shape
    m = b * s
    lam0 = _lambda_init(layer)
    x2d = h.reshape(m, D_MODEL)
    p32, k_new, v_new, qkv, vt, dt_raw = _in_proj(x2d, norm_w, w_main, w_dt, tm=_tiles(m))
    k_new = k_new.reshape(b, s, N_ATT_HEADS, 2, ATT_HEAD_DIM)
    v_new = v_new.reshape(b, s, N_ATT_HEADS, 2 * ATT_HEAD_DIM)
    p32 = p32.reshape(b, s, D_P32)
    qkv = qkv.reshape(b, s, D_QKV)

    if k_past is None:
        tile = min(s, 512)
        att = _prompt_attention(qkv, vt, p32, _prompt_bias_rows(rel_bias, tile), subln_w, lq1, lk1, lq2, lk2,
                                tile=tile, lam0=lam0)
    else:
        past = k_past.shape[1]
        kp, vp = _merge_cache(k_past.reshape(b * past, D_ATT // LANES, LANES),
                              v_past.reshape(b * past, N_ATT_HEADS, 2 * ATT_HEAD_DIM), tm=min(b * past, 512))
        att = _sample_attention(qkv, p32, kp.reshape(b, past, D_ATT), vp.reshape(b, past, D_ATT),
                                _sample_bias(rel_bias, past, s), subln_w, lq1, lk1, lq2, lk2, lam0=lam0)

    assert s >= CONV_WIDTH - 1
    conv_new = p32[:, s - (CONV_WIDTH - 1):, COL_XS:COL_XS + D_CONV]
    hist8 = jnp.pad(conv_past, ((0, 0), (SUBLANES - (CONV_WIDTH - 1), 0), (0, 0)))
    dtT = jnp.transpose(dt_raw[:, :N_SSM_HEADS].reshape(b, s, N_SSM_HEADS), (0, 2, 1))
    h0 = ssm_past.reshape(b, N_SSM_HEADS * SSM_HEAD_DIM, SSM_STATE)
    if s % SSD_L == 0:
        src, cols = p32, (COL_Z, COL_XS, COL_B, COL_C)
    else:
        assert s < SSD_L
        pad = SSD_L - s
        src = jnp.pad(p32[:, :, COL_Z:], ((0, 0), (0, pad), (0, 0)))
        cols = (0, D_SSM, 2 * D_SSM, 2 * D_SSM + D_BC)
        dtT = jnp.pad(dtT, ((0, 0), (0, 0), (0, pad)))
    y, ssm_new = _ssd(src, cols, dtT, hist8, h0, conv_w, conv_b, dt_bias, a_log, d_full, ssm_norm_w,
                      valid=min(s, SSD_L))
    y = y[:, :s].reshape(m, D_SSM)
    ssm_new = ssm_new.reshape(b, N_SSM_HEADS, SSM_HEAD_DIM, SSM_STATE)

    out = _out_proj(att.reshape(m, D_ATT), y, x2d, w_out1, w_out2, out_norm_w, tm=min(m, 512))
    return out.reshape(b, s, D_MODEL), k_new, v_new, conv_new, ssm_new


def kernel(x_prompt, x_sample, cache_k, cache_v, cache_conv, state_ssm, rel_bias, norm_w, w_in, lambda_q1,
           lambda_k1, lambda_q2, lambda_k2, subln_w, conv_w, conv_b, dt_bias, A_log, D_skip, ssm_norm_w, w_out,
           final_norm_w):
    depth = w_in.shape[0]
    assert depth == 1, "the final norm is fused into the (single) layer's output projection"
    bp = x_prompt.shape[0]
    l = 0
    col_q = 0
    col_dt = 4 * D_ATT + D_SSM + D_CONV
    w = w_in[l]
    w_main = jnp.transpose(w[:, col_q:col_dt].astype(BF16).reshape(D_MODEL, -1, IN_PROJ_TN), (1, 0, 2))
    w_dt = jnp.pad(w[:, col_dt:], ((0, 0), (0, LANES - N_SSM_HEADS))).astype(BF16)
    row = lambda t: t.reshape(1, -1).astype(F32)
    col = lambda t: t.reshape(-1, 1).astype(F32)
    params = (rel_bias, row(norm_w[l]), w_main, w_dt, row(lambda_q1[l]), row(lambda_k1[l]), row(lambda_q2[l]),
              row(lambda_k2[l]), row(subln_w[l]), conv_w[l].astype(F32), row(conv_b[l]), col(dt_bias[l]),
              col(A_log[l]), row(jnp.repeat(D_skip[l], SSM_HEAD_DIM)), row(ssm_norm_w[l]),
              w_out[l, :D_ATT].astype(BF16), w_out[l, D_ATT:].astype(BF16), row(final_norm_w))
    conv0 = jnp.zeros((bp, CONV_WIDTH - 1, D_CONV), x_prompt.dtype)
    ssm0 = jnp.zeros((bp, N_SSM_HEADS, SSM_HEAD_DIM, SSM_STATE), state_ssm.dtype)
    yp, k1, v1, c1, s1 = _layer(x_prompt, None, None, conv0, ssm0, l, *params)
    ys, k2, v2, c2, s2 = _layer(x_sample, cache_k[l], cache_v[l], cache_conv[l], state_ssm[l], l, *params)
    return (yp, ys, k1[None], v1[None], c1[None], s1[None], k2[None], v2[None], c2[None], s2[None])
```
